```python
import jax, jax.numpy as jnp
from jax import lax
import numpy as np

D_MODEL = 1024
BATCH = 8
SEQ = 8192
DEPTH = 2

GRID_W = 64
CTX_LEN = 256
EPS = 1e-6
CHUNK = 64

A_HEADS = 4
A_DK = 64
A_DV = 128
A_RANK = 16
GATE_TAU = 16.0
B_HEADS = 4
B_DK = 128
B_DV = 128
N_B_LAYERS = (DEPTH + 1) // 2
C_HEADS = 4
C_DK = 128
C_DV = 128
CONV_W = 5
D_HEADS = 8
D_KV_HEADS = 2
D_HD = 64
WINDOW = 128
WBLK = 128
ROPE_BASE = 10000.0
N_EXPERTS = 32
TOP_K = 4
D_FF = 1024
SWIGLU_LIMIT = 7.0
SWIGLU_ALPHA = 1.702

EVEN_SPLITS = (A_HEADS * A_DK, A_HEADS * A_DK, A_HEADS * A_DV, A_RANK, A_RANK, A_HEADS * A_DV,
               B_HEADS * B_DK, B_HEADS * B_DK, B_HEADS * B_DK, B_HEADS * B_DV, B_HEADS * B_DV)
ODD_SPLITS = (C_HEADS * C_DK, C_HEADS * C_DK, C_HEADS * C_DV, C_HEADS, C_HEADS, C_HEADS, C_HEADS,
              C_HEADS * C_DV, D_HEADS * D_HD, D_KV_HEADS * D_HD, D_KV_HEADS * D_HD)
EVEN_IN = sum(EVEN_SPLITS)
ODD_IN = sum(ODD_SPLITS)
EVEN_MIX = A_HEADS * A_DV + B_HEADS * B_DV
ODD_MIX = C_HEADS * C_DV + D_HEADS * D_HD

kernel_name = 'hybrid_gla_hgrn2_deltanet_swa_moe_dit'


def rms_norm(x, w):
    xf = x.astype(jnp.float32)
    y = xf * lax.rsqrt(jnp.mean(xf * xf, axis=-1, keepdims=True) + EPS)
    return (y * w.astype(jnp.float32)).astype(x.dtype)


def modulate(h, shift, scale):
    return h * (1.0 + scale) + shift


def split_cols(h, sizes):
    return jnp.split(h, np.cumsum(sizes)[:-1].tolist(), axis=-1)


def heads(x, n):
    B, T, _ = x.shape
    return x.reshape(B, T, n, -1).transpose(0, 2, 1, 3)


def merge_heads(x):
    B, n, T, d = x.shape
    return x.transpose(0, 2, 1, 3).reshape(B, T, n * d)


def l2_normalize(x):
    return x * lax.rsqrt(jnp.sum(x * x, axis=-1, keepdims=True) + EPS)


def gated_head_norm(o, w, og):
    o = o * lax.rsqrt(jnp.mean(o * o, axis=-1, keepdims=True) + EPS) * w.astype(jnp.float32)
    return merge_heads(o) * jax.nn.silu(og)


def axial_rope(T):
    rows = T // GRID_W
    row = jnp.repeat(jnp.arange(rows, dtype=jnp.float32), GRID_W)
    col = jnp.tile(jnp.arange(GRID_W, dtype=jnp.float32), rows)
    n_freq = D_HD // 4
    inv = ROPE_BASE ** (-jnp.arange(n_freq, dtype=jnp.float32) / n_freq)
    ang = jnp.concatenate([row[:, None] * inv, col[:, None] * inv], axis=-1)
    return jnp.cos(ang), jnp.sin(ang)


def apply_rope(x, cos, sin):
    x1, x2 = jnp.split(x, 2, axis=-1)
    return jnp.concatenate([x1 * cos - x2 * sin, x2 * cos + x1 * sin], axis=-1)


def centred_conv(x, w):
    pad = CONV_W // 2
    return lax.conv_general_dilated(x, w[:, None, :].astype(x.dtype), window_strides=(1,),
                                    padding=[(pad, pad)], dimension_numbers=('NWC', 'WIO', 'NWC'),
                                    feature_group_count=x.shape[-1])


def chunked_gated_scan(q, k, v, log_f, s0, with_output):
    B, H, T, dk = q.shape
    n = T // CHUNK

    def to_chunks(a):
        return jnp.moveaxis(a.reshape(B, H, n, CHUNK, a.shape[-1]), 2, 0)

    incl = jnp.tril(jnp.ones((CHUNK, CHUNK), dtype=bool))[:, :, None]

    def step(S, inp):
        qc, kc, vc, fc = inp
        b = jnp.cumsum(fc, axis=-2)
        b_last = b[..., -1, :]
        o = None
        if with_output:
            diff = b[..., :, None, :] - b[..., None, :, :]
            decay = jnp.where(incl, jnp.exp(jnp.where(incl, diff, 0.0)), 0.0)
            scores = jnp.einsum('bhtk,bhsk,bhtsk->bhts', qc, kc, decay)
            o = (jnp.einsum('bhtk,bhkv->bhtv', qc * jnp.exp(b), S)
                 + jnp.einsum('bhts,bhsv->bhtv', scores, vc))
        S = (jnp.exp(b_last)[..., None] * S
             + jnp.einsum('bhsk,bhsv->bhkv', kc * jnp.exp(b_last[..., None, :] - b), vc))
        return S, o

    S, o = lax.scan(step, s0, (to_chunks(q), to_chunks(k), to_chunks(v), to_chunks(log_f)))
    if with_output:
        o = jnp.moveaxis(o, 0, 2).reshape(B, H, T, -1)
    return S, o


def chunked_delta_scan(q, k, v, g, beta, s0, with_output):
    B, H, T, dk = q.shape
    n = T // CHUNK
    q, k, v = [a.reshape(B, H, n, CHUNK, a.shape[-1]) for a in (q, k, v)]
    g = g.reshape(B, H, n, CHUNK)
    beta = beta.reshape(B, H, n, CHUNK)
    gam = jnp.cumsum(g, axis=-1)
    incl = jnp.tril(jnp.ones((CHUNK, CHUNK), dtype=bool))
    strict = jnp.tril(jnp.ones((CHUNK, CHUNK), dtype=bool), -1)
    diff = gam[..., :, None] - gam[..., None, :]
    L = jnp.where(incl, jnp.exp(jnp.where(incl, diff, 0.0)), 0.0)
    kb = k * beta[..., None]
    a_mat = jnp.where(strict, jnp.einsum('bhntk,bhnsk->bhnts', kb, k) * L, 0.0)
    eye = jnp.eye(CHUNK, dtype=q.dtype)
    t_inv = lax.linalg.triangular_solve(a_mat + eye, jnp.broadcast_to(eye, a_mat.shape),
                                        left_side=True, lower=True, unit_diagonal=True)
    u = jnp.einsum('bhnts,bhnsv->bhntv', t_inv, v * beta[..., None])
    w = jnp.einsum('bhnts,bhnsk->bhntk', t_inv, kb * jnp.exp(gam)[..., None])

    def step(S, inp):
        qc, kc, uc, wc, gc, Lc = inp
        v_new = uc - jnp.einsum('bhtk,bhkv->bhtv', wc, S)
        o = None
        if with_output:
            scores = jnp.einsum('bhtk,bhsk->bhts', qc, kc) * Lc
            o = (jnp.einsum('bhtk,bhkv->bhtv', qc * jnp.exp(gc)[..., None], S)
                 + jnp.einsum('bhts,bhsv->bhtv', scores, v_new))
        g_last = gc[..., -1]
        S = (jnp.exp(g_last)[..., None, None] * S
             + jnp.einsum('bhsk,bhsv->bhkv', kc * jnp.exp(g_last[..., None] - gc)[..., None], v_new))
        return S, o

    xs = tuple(jnp.moveaxis(a, 2, 0) for a in (q, k, u, w, gam, L))
    S, o = lax.scan(step, s0, xs)
    if with_output:
        o = jnp.moveaxis(o, 0, 2).reshape(B, H, T, -1)
    return S, o


def flip_t(a):
    return jnp.flip(a, axis=2)


def bidirectional(scan_fn, ctx_fwd, ctx_bwd, lat_fwd, lat_bwd, s0, ctx_out):
    sc_f, oc_f = scan_fn(*ctx_fwd, s0, ctx_out)
    _, ox_f = scan_fn(*lat_fwd, sc_f, True)
    sc_b, oc_b = scan_fn(*[flip_t(a) for a in ctx_bwd], s0, ctx_out)
    _, ox_b = scan_fn(*[flip_t(a) for a in lat_bwd], sc_b, True)
    ox = ox_f + flip_t(ox_b)
    oc = oc_f + flip_t(oc_b) if ctx_out else None
    return oc, ox


def window_attention(q, k, v, kc, vc, sinks):
    B, KVH, G, T, d = q.shape
    nb = T // WBLK
    qb = q.reshape(B, KVH, G, nb, WBLK, d)

    def band(a):
        ap = jnp.pad(a, ((0, 0), (0, 0), (WBLK, WBLK), (0, 0))).reshape(B, KVH, nb + 2, WBLK, d)
        return jnp.concatenate([ap[:, :, :-2], ap[:, :, 1:-1], ap[:, :, 2:]], axis=3)

    kband, vband = band(k), band(v)
    qpos = jnp.arange(WBLK)
    kpos = jnp.arange(3 * WBLK) - WBLK
    rel = kpos[None, :] - qpos[:, None]
    abs_k = (jnp.arange(nb) * WBLK)[:, None] + kpos[None, :]
    valid = ((jnp.abs(rel) <= WINDOW)[None]
             & ((abs_k >= 0) & (abs_k < T))[:, None, :])
    scale = d ** -0.5
    s_loc = jnp.einsum('bhgnqd,bhnkd->bhgnqk', qb, kband) * scale
    s_loc = jnp.where(valid, s_loc, -jnp.inf)
    s_ctx = jnp.einsum('bhgnqd,bhcd->bhgnqc', qb, kc) * scale
    sink = jnp.broadcast_to(sinks.astype(jnp.float32).reshape(KVH, G)[None, :, :, None, None, None],
                            (B, KVH, G, nb, WBLK, 1))
    p = jax.nn.softmax(jnp.concatenate([sink, s_loc, s_ctx], axis=-1), axis=-1)
    p_loc = p[..., 1:1 + 3 * WBLK]
    p_ctx = p[..., 1 + 3 * WBLK:]
    out = (jnp.einsum('bhgnqk,bhnkd->bhgnqd', p_loc, vband)
           + jnp.einsum('bhgnqc,bhcd->bhgnqd', p_ctx, vc))
    return out.reshape(B, KVH, G, T, d)


def context_attention(q, k, v, sinks):
    B, KVH, G, Tc, d = q.shape
    s = jnp.einsum('bhgqd,bhkd->bhgqk', q, k) * d ** -0.5
    sink = jnp.broadcast_to(sinks.astype(jnp.float32).reshape(KVH, G)[None, :, :, None, None],
                            (B, KVH, G, Tc, 1))
    p = jax.nn.softmax(jnp.concatenate([sink, s], axis=-1), axis=-1)
    return jnp.einsum('bhgqk,bhkd->bhgqd', p[..., 1:], v)


def mixer_gla_hgrn(hc, hx, ctx_out, w_in, w_out, gla_w2_f, gla_b_f, gla_w2_b, gla_b_b,
                   gla_norm_w, hgrn_norm_w, lb):
    f32 = jnp.float32
    lb = lb.astype(f32)

    def prep(h):
        parts = [p_.astype(f32) for p_ in split_cols(h @ w_in, EVEN_SPLITS)]
        aq, ak, av, ar_f, ar_b, aog, bq, bz_f, bz_b, bi, bog = parts
        aq = heads(aq, A_HEADS) * A_DK ** -0.5
        ak = heads(ak, A_HEADS)
        av = heads(av, A_HEADS)

        def a_dir(r, w2, b2):
            log_a = jax.nn.log_sigmoid(r @ w2.astype(f32) + b2.astype(f32)) / GATE_TAU
            return (aq, ak, av, heads(log_a, A_HEADS))

        bq = heads(bq, B_HEADS)
        bi = heads(bi, B_HEADS)

        def b_dir(z):
            log_f = jnp.logaddexp(jnp.log(lb), jnp.log1p(-lb) + jax.nn.log_sigmoid(z))
            key = (1.0 - lb) * jax.nn.sigmoid(-z)
            return (bq, heads(key, B_HEADS), bi, heads(log_f, B_HEADS))

        return (a_dir(ar_f, gla_w2_f, gla_b_f), a_dir(ar_b, gla_w2_b, gla_b_b),
                b_dir(bz_f), b_dir(bz_b), aog, bog)

    ca_f, ca_b, cb_f, cb_b, c_aog, c_bog = prep(hc)
    xa_f, xa_b, xb_f, xb_b, x_aog, x_bog = prep(hx)
    B = hx.shape[0]
    sa0 = jnp.zeros((B, A_HEADS, A_DK, A_DV), f32)
    sb0 = jnp.zeros((B, B_HEADS, B_DK, B_DV), f32)
    oc_a, ox_a = bidirectional(chunked_gated_scan, ca_f, ca_b, xa_f, xa_b, sa0, ctx_out)
    oc_b, ox_b = bidirectional(chunked_gated_scan, cb_f, cb_b, xb_f, xb_b, sb0, ctx_out)

    def readout(oa, ob, aog, bog, dtype):
        y = jnp.concatenate([gated_head_norm(oa, gla_norm_w, aog),
                             gated_head_norm(ob, hgrn_norm_w, bog)], axis=-1)
        return y.astype(dtype) @ w_out

    yx = readout(ox_a, ox_b, x_aog, x_bog, hx.dtype)
    yc = readout(oc_a, oc_b, c_aog, c_bog, hc.dtype) if ctx_out else None
    return yc, yx


def mixer_delta_swa(hc, hx, ctx_out, w_in, w_out, conv_w, a_log_f, dt_bias_f, a_log_b, dt_bias_b,
                    dn_norm_w, sinks, cos, sin):
    f32 = jnp.float32

    def prep(h, rope):
        cq, ck, cv, bt_f, bt_b, a_f, a_b, og, dq, dk, dv = split_cols(h @ w_in, ODD_SPLITS)
        qkv = jax.nn.silu(centred_conv(jnp.concatenate([cq, ck, cv], axis=-1), conv_w)).astype(f32)
        cq, ck, cv = split_cols(qkv, (C_HEADS * C_DK, C_HEADS * C_DK, C_HEADS * C_DV))
        cq = l2_normalize(heads(cq, C_HEADS)) * C_DK ** -0.5
        ck = l2_normalize(heads(ck, C_HEADS))
        cv = heads(cv, C_HEADS)

        def c_dir(bt, a, a_log, dt_bias):
            g = -jnp.exp(a_log.astype(f32)) * jax.nn.softplus(a.astype(f32) + dt_bias.astype(f32))
            beta = jax.nn.sigmoid(bt.astype(f32))
            return (cq, ck, cv, g.transpose(0, 2, 1), beta.transpose(0, 2, 1))

        dq = heads(dq.astype(f32), D_HEADS)
        dk = heads(dk.astype(f32), D_KV_HEADS)
        dv = heads(dv.astype(f32), D_KV_HEADS)
        if rope:
            dq = apply_rope(dq, cos, sin)
            dk = apply_rope(dk, cos, sin)
        B, _, T, _ = dq.shape
        dq = dq.reshape(B, D_KV_HEADS, D_HEADS // D_KV_HEADS, T, D_HD)
        return (c_dir(bt_f, a_f, a_log_f, dt_bias_f), c_dir(bt_b, a_b, a_log_b, dt_bias_b),
                og.astype(f32), dq, dk, dv)

    cc_f, cc_b, c_og, c_dq, c_dk, c_dv = prep(hc, False)
    xc_f, xc_b, x_og, x_dq, x_dk, x_dv = prep(hx, True)
    B = hx.shape[0]
    s0 = jnp.zeros((B, C_HEADS, C_DK, C_DV), f32)
    oc_c, ox_c = bidirectional(chunked_delta_scan, cc_f, cc_b, xc_f, xc_b, s0, ctx_out)
    ox_d = window_attention(x_dq, x_dk, x_dv, c_dk, c_dv, sinks)

    def readout(o_c, o_d, og, dtype):
        B_, _, T, _ = o_c.shape
        y = jnp.concatenate([gated_head_norm(o_c, dn_norm_w, og),
                             merge_heads(o_d.reshape(B_, D_HEADS, T, D_HD))], axis=-1)
        return y.astype(dtype) @ w_out

    yx = readout(ox_c, ox_d, x_og, hx.dtype)
    yc = None
    if ctx_out:
        oc_d = context_attention(c_dq, c_dk, c_dv, sinks)
        yc = readout(oc_c, oc_d, c_og, hc.dtype)
    return yc, yx


def moe_ffn(h, router_w, router_b, w_up, b_up, w_down, b_down):
    f32 = jnp.float32
    logits = (h @ router_w + router_b).astype(f32)
    top_val, top_idx = lax.top_k(logits, TOP_K)
    top_w = jax.nn.softmax(top_val, axis=-1)
    gates = jnp.sum(jax.nn.one_hot(top_idx, N_EXPERTS, dtype=f32) * top_w[..., None], axis=-2)
    out = jnp.zeros(h.shape, f32)
    for e in range(N_EXPERTS):
        u = (h @ w_up[e] + b_up[e]).astype(f32)
        glu, lin = jnp.split(u, 2, axis=-1)
        glu = jnp.minimum(glu, SWIGLU_LIMIT)
        lin = jnp.clip(lin, -SWIGLU_LIMIT, SWIGLU_LIMIT)
        act = glu * jax.nn.sigmoid(SWIGLU_ALPHA * glu) * (lin + 1.0)
        y = act.astype(h.dtype) @ w_down[e] + b_down[e]
        out = out + gates[..., e:e + 1] * y.astype(f32)
    return out.astype(h.dtype)


def setup_inputs(seed: int = 0) -> dict:
    key = jax.random.key(seed)
    ks = iter(jax.random.split(key, 64))
    f32 = jnp.float32
    d = D_MODEL

    def nrm(shape, scale):
        return jax.random.normal(next(ks), shape, f32) * scale

    def gain(n):
        return 1.0 + nrm((n,), 0.02)

    def a_log(n):
        return jnp.log(jax.random.uniform(next(ks), (n,), f32, 1.0, 16.0))

    def dt_bias(n):
        dt = jnp.exp(jax.random.uniform(next(ks), (n,), f32, float(np.log(1e-3)), float(np.log(1e-1))))
        return dt + jnp.log(-jnp.expm1(-dt))

    inp = {}
    inp['x'] = nrm((BATCH, SEQ, d), 1.0)
    inp['c'] = nrm((BATCH, d), 1.0)
    inp['ctx'] = nrm((BATCH, CTX_LEN, d), 1.0)
    inp['c_ctx'] = nrm((d,), 1.0)
    inp['l0_ada_w'] = nrm((d, 6 * d), 0.5 * d ** -0.5)
    inp['l0_ada_b'] = nrm((6 * d,), 0.02)
    inp['l0_norm_mix_w'] = gain(d)
    inp['l0_w_in'] = nrm((d, EVEN_IN), d ** -0.5)
    inp['l0_w_out'] = nrm((EVEN_MIX, d), EVEN_MIX ** -0.5)
    inp['l0_gla_w2_f'] = nrm((A_RANK, A_HEADS * A_DK), A_RANK ** -0.5)
    inp['l0_gla_b_f'] = nrm((A_HEADS * A_DK,), 0.1)
    inp['l0_gla_w2_b'] = nrm((A_RANK, A_HEADS * A_DK), A_RANK ** -0.5)
    inp['l0_gla_b_b'] = nrm((A_HEADS * A_DK,), 0.1)
    inp['l0_gla_norm_w'] = gain(A_DV)
    inp['l0_hgrn_norm_w'] = gain(B_DV)
    inp['hgrn_lb_logits'] = nrm((N_B_LAYERS + 1, B_HEADS * B_DK), 0.5)
    inp['l0_norm_ffn_w'] = gain(d)
    inp['l0_router_w'] = nrm((d, N_EXPERTS), d ** -0.5)
    inp['l0_router_b'] = nrm((N_EXPERTS,), 0.01)
    inp['l0_w_up'] = nrm((N_EXPERTS, d, 2 * D_FF), d ** -0.5)
    inp['l0_b_up'] = nrm((N_EXPERTS, 2 * D_FF), 0.02)
    inp['l0_w_down'] = nrm((N_EXPERTS, D_FF, d), D_FF ** -0.5)
    inp['l0_b_down'] = nrm((N_EXPERTS, d), 0.02)
    inp['l1_ada_w'] = nrm((d, 6 * d), 0.5 * d ** -0.5)
    inp['l1_ada_b'] = nrm((6 * d,), 0.02)
    inp['l1_norm_mix_w'] = gain(d)
    inp['l1_w_in'] = nrm((d, ODD_IN), d ** -0.5)
    inp['l1_w_out'] = nrm((ODD_MIX, d), ODD_MIX ** -0.5)
    inp['l1_conv_w'] = nrm((CONV_W, 2 * C_HEADS * C_DK + C_HEADS * C_DV), CONV_W ** -0.5)
    inp['l1_a_log_f'] = a_log(C_HEADS)
    inp['l1_dt_bias_f'] = dt_bias(C_HEADS)
    inp['l1_a_log_b'] = a_log(C_HEADS)
    inp['l1_dt_bias_b'] = dt_bias(C_HEADS)
    inp['l1_dn_norm_w'] = gain(C_DV)
    inp['l1_sinks'] = nrm((D_HEADS,), 1.0)
    inp['l1_norm_ffn_w'] = gain(d)
    inp['l1_router_w'] = nrm((d, N_EXPERTS), d ** -0.5)
    inp['l1_router_b'] = nrm((N_EXPERTS,), 0.01)
    inp['l1_w_up'] = nrm((N_EXPERTS, d, 2 * D_FF), d ** -0.5)
    inp['l1_b_up'] = nrm((N_EXPERTS, 2 * D_FF), 0.02)
    inp['l1_w_down'] = nrm((N_EXPERTS, D_FF, d), D_FF ** -0.5)
    inp['l1_b_down'] = nrm((N_EXPERTS, d), 0.02)
    inp['final_norm_w'] = gain(d)
    return inp


def reference(x, c, ctx, c_ctx,
              l0_ada_w, l0_ada_b, l0_norm_mix_w, l0_w_in, l0_w_out,
              l0_gla_w2_f, l0_gla_b_f, l0_gla_w2_b, l0_gla_b_b, l0_gla_norm_w, l0_hgrn_norm_w,
              hgrn_lb_logits, l0_norm_ffn_w,
              l0_router_w, l0_router_b, l0_w_up, l0_b_up, l0_w_down, l0_b_down,
              l1_ada_w, l1_ada_b, l1_norm_mix_w, l1_w_in, l1_w_out,
              l1_conv_w, l1_a_log_f, l1_dt_bias_f, l1_a_log_b, l1_dt_bias_b, l1_dn_norm_w, l1_sinks,
              l1_norm_ffn_w,
              l1_router_w, l1_router_b, l1_w_up, l1_b_up, l1_w_down, l1_b_down,
              final_norm_w):
    T = x.shape[1]
    cos, sin = axial_rope(T)
    lb_table = jnp.cumsum(jax.nn.softmax(hgrn_lb_logits.astype(jnp.float32), axis=0), axis=0)
    layers = (
        dict(ada=(l0_ada_w, l0_ada_b), norm_mix_w=l0_norm_mix_w, norm_ffn_w=l0_norm_ffn_w,
             mix=(l0_w_in, l0_w_out, l0_gla_w2_f, l0_gla_b_f, l0_gla_w2_b, l0_gla_b_b,
                  l0_gla_norm_w, l0_hgrn_norm_w),
             moe=(l0_router_w, l0_router_b, l0_w_up, l0_b_up, l0_w_down, l0_b_down)),
        dict(ada=(l1_ada_w, l1_ada_b), norm_mix_w=l1_norm_mix_w, norm_ffn_w=l1_norm_ffn_w,
             mix=(l1_w_in, l1_w_out, l1_conv_w, l1_a_log_f, l1_dt_bias_f, l1_a_log_b, l1_dt_bias_b,
                  l1_dn_norm_w, l1_sinks),
             moe=(l1_router_w, l1_router_b, l1_w_up, l1_b_up, l1_w_down, l1_b_down)),
    )
    sc = jax.nn.silu(c)
    scc = jax.nn.silu(c_ctx)
    for l in range(DEPTH):
        p = layers[l]
        ctx_out = l < DEPTH - 1
        ada_w, ada_b = p['ada']
        sh1, sc1, g1, sh2, sc2, g2 = jnp.split((sc @ ada_w + ada_b)[:, None, :], 6, axis=-1)
        csh1, csc1, cg1, csh2, csc2, cg2 = jnp.split(scc @ ada_w + ada_b, 6, axis=-1)
        hx = modulate(rms_norm(x, p['norm_mix_w']), sh1, sc1)
        hc = modulate(rms_norm(ctx, p['norm_mix_w']), csh1, csc1)
        if l % 2 == 0:
            yc, yx = mixer_gla_hgrn(hc, hx, ctx_out, *p['mix'], lb_table[l // 2])
        else:
            yc, yx = mixer_delta_swa(hc, hx, ctx_out, *p['mix'], cos, sin)
        x = x + g1 * yx
        x = x + g2 * moe_ffn(modulate(rms_norm(x, p['norm_ffn_w']), sh2, sc2), *p['moe'])
        if ctx_out:
            ctx = ctx + cg1 * yc
            ctx = ctx + cg2 * moe_ffn(modulate(rms_norm(ctx, p['norm_ffn_w']), csh2, csc2), *p['moe'])
    return rms_norm(x, final_norm_w)
```

```python
import functools

import numpy as np
import jax
import jax.numpy as jnp
from jax import lax
from jax.experimental import pallas as pl
from jax.experimental.pallas import tpu as pltpu

F32 = jnp.float32
BF16 = jnp.bfloat16
EPS = 1e-6

CHUNK = 64
A_HEADS, A_DK, A_DV, A_RANK = 4, 64, 128, 16
GATE_TAU = 16.0
B_HEADS, B_DK, B_DV = 4, 128, 128
C_HEADS, C_DK, C_DV = 4, 128, 128
CONV_W = 5
D_HEADS, D_KV_HEADS, D_HD = 8, 2, 64
WINDOW = 128
GRID_W = 64
ROPE_BASE = 10000.0
TOP_K = 4
SWIGLU_LIMIT = 7.0
SWIGLU_ALPHA = 1.702

TM = 256
HEAD_W = 128
EXPERT_TILE = 512
FF_CHUNK = 512
VMEM_LIMIT = 48 * 1024 * 1024


def _cp(sem, vmem=VMEM_LIMIT):
    return pltpu.CompilerParams(dimension_semantics=sem, vmem_limit_bytes=vmem)


def _const_spec(shape):
    nd = len(shape)
    return pl.BlockSpec(shape, lambda *_: (0,) * nd)


def _silu(x):
    return x * jax.nn.sigmoid(x)


def _log_sigmoid(z):
    return jnp.minimum(z, 0.0) - jnp.log1p(jnp.exp(-jnp.abs(z)))


def _norm_mod(x, nw, shift, scale):
    y = x * lax.rsqrt(jnp.mean(x * x, axis=-1, keepdims=True) + EPS) * nw
    return y * (1.0 + scale) + shift


def _mod_rows(mod_ref, batch, blk, n_batch, d, first, count):
    row = jnp.where(blk == 0, n_batch, batch)
    return [mod_ref[pl.ds(row, 1), (first + i) * d:(first + i + 1) * d] for i in range(count)]


def _ada_kernel(c_ref, w_ref, b_ref, o_ref):
    s = _silu(c_ref[...])
    o_ref[...] = jnp.dot(s, w_ref[...], precision=lax.Precision.HIGHEST,
                         preferred_element_type=F32) + b_ref[...]


def _ada_table(c_all, w, b):
    rows, d = c_all.shape
    n = w.shape[1]
    bn = d
    return pl.pallas_call(
        _ada_kernel,
        grid=(n // bn,),
        in_specs=[pl.BlockSpec((rows, d), lambda j: (0, 0)),
                  pl.BlockSpec((d, bn), lambda j: (0, j)),
                  pl.BlockSpec((1, bn), lambda j: (0, j))],
        out_specs=pl.BlockSpec((rows, bn), lambda j: (0, j)),
        out_shape=jax.ShapeDtypeStruct((rows, n), F32),
        compiler_params=_cp(("arbitrary",)),
        name="ada_table",
    )(c_all, w, b.reshape(1, n))


def _proj0_kernel(x_ref, mod_ref, nw_ref, w_ref, w2_ref, b2_ref, lbl_ref,
                  q_ref, kf_ref, kb_ref, v_ref, lff_ref, lfb_ref, og_ref, *, n_batch, d):
    b, j = pl.program_id(0), pl.program_id(1)
    shift, scale = _mod_rows(mod_ref, b, j, n_batch, d, 0, 2)
    h = _norm_mod(x_ref[0], nw_ref[...], shift, scale).astype(BF16)

    def mm(c0, c1):
        return jnp.dot(h, w_ref[:, c0:c1], preferred_element_type=F32)

    hw = A_HEADS * HEAD_W
    q_ref[0, :, 0:hw] = mm(0, hw) * (A_DK ** -0.5)
    q_ref[0, :, hw:2 * hw] = mm(hw, 2 * hw)
    v_ref[0] = mm(2 * hw, 4 * hw)
    og_ref[0] = mm(4 * hw, 6 * hw)
    kg = mm(6 * hw, 7 * hw)
    kf_ref[0, :, 0:hw] = kg
    kb_ref[0, :, 0:hw] = kg
    lg = lbl_ref[...]
    e = jnp.exp(lg - jnp.max(lg, axis=0, keepdims=True))
    lb = e[0:1] / jnp.sum(e, axis=0, keepdims=True)
    log_lb, log_1m = jnp.log(lb), jnp.log1p(-lb)
    ar = mm(9 * hw, 9 * hw + HEAD_W).astype(BF16)
    for di, (k_ref, lf_ref) in enumerate(((kf_ref, lff_ref), (kb_ref, lfb_ref))):
        z = mm((7 + di) * hw, (8 + di) * hw)
        s1 = log_1m + _log_sigmoid(z)
        lf_ref[0, :, hw:2 * hw] = jnp.maximum(log_lb, s1) + jnp.log1p(jnp.exp(-jnp.abs(log_lb - s1)))
        k_ref[0, :, hw:2 * hw] = (1.0 - lb) * jax.nn.sigmoid(-z)
        za = jnp.dot(ar, w2_ref[di], preferred_element_type=F32) + b2_ref[di]
        lf_ref[0, :, 0:hw] = _log_sigmoid(za) * (1.0 / GATE_TAU)


def _proj0(x, mod, nw, w, w2, b2, lbl, n_batch):
    bsz, l, d = x.shape
    n_blk = l // TM
    wide = 2 * A_HEADS * HEAD_W
    blk = pl.BlockSpec((1, TM, wide), lambda b, j: (b, j, 0))
    return pl.pallas_call(
        functools.partial(_proj0_kernel, n_batch=n_batch, d=d),
        grid=(bsz, n_blk),
        in_specs=[pl.BlockSpec((1, TM, d), lambda b, j: (b, j, 0)),
                  _const_spec(mod.shape), _const_spec(nw.shape), _const_spec(w.shape),
                  _const_spec(w2.shape), _const_spec(b2.shape), _const_spec(lbl.shape)],
        out_specs=[blk] * 7,
        out_shape=[jax.ShapeDtypeStruct((bsz, l, wide), F32)] * 7,
        compiler_params=_cp(("parallel", "arbitrary")),
        name="proj0",
    )(x, mod, nw, w, w2, b2, lbl)


_LEVELS = (32, 16, 8, 4, 2, 1)


def _level_masks():
    t = np.arange(CHUNK)[:, None]
    s = np.arange(CHUNK)[None, :]
    out = np.zeros((2, len(_LEVELS) + 1, CHUNK, CHUNK), np.float32)
    for li, m in enumerate(_LEVELS):
        same = (t // (2 * m)) == (s // (2 * m))
        fwd = same & (t % (2 * m) >= m) & (s % (2 * m) < m)
        out[0, li] = fwd
        out[1, li] = fwd.T
    out[:, -1] = np.eye(CHUNK)
    return out


def _cumsum_rows(x, reverse):
    n = x.shape[0]
    r = lax.broadcasted_iota(jnp.int32, x.shape, 0)
    sh = 1
    while sh < n:
        if reverse:
            x = x + jnp.where(r < n - sh, pltpu.roll(x, n - sh, 0), 0.0)
        else:
            x = x + jnp.where(r >= sh, pltpu.roll(x, sh, 0), 0.0)
        sh *= 2
    return x


def _level_ref(cum, m, reverse):
    n = cum.shape[0]
    tgt = m if reverse else m - 1
    if 2 * m >= 8:
        parts = [jnp.broadcast_to(cum[g + tgt:g + tgt + 1, :], (2 * m, cum.shape[1]))
                 for g in range(0, n, 2 * m)]
        return parts[0] if len(parts) == 1 else jnp.concatenate(parts, axis=0)
    pos = lax.broadcasted_iota(jnp.int32, cum.shape, 0) % (2 * m)
    out = cum
    for p in range(2 * m):
        if p == tgt:
            continue
        shift = (p - tgt) % n
        out = jnp.where(pos == p, pltpu.roll(cum, shift, 0), out)
    return out


def _dot_nt(a, b):
    return lax.dot_general(a, b, (((1,), (1,)), ((), ())), preferred_element_type=F32)


def _dot_tn(a, b):
    return lax.dot_general(a, b, (((0,), (0,)), ((), ())), preferred_element_type=F32)


def _gated_chunk(q, k, v, lf, st, masks, reverse):
    cum = _cumsum_rows(lf, reverse)
    scores = masks[len(_LEVELS)] * _dot_nt(q.astype(BF16), k.astype(BF16))
    for li, m in enumerate(_LEVELS):
        w = jnp.exp(-jnp.abs(cum - _level_ref(cum, m, reverse)))
        scores = scores + masks[li] * _dot_nt((q * w).astype(BF16), (k * w).astype(BF16))
    o = jnp.dot(scores.astype(BF16), v.astype(BF16), preferred_element_type=F32)
    o = o + _dot_nt((q * jnp.exp(cum)).astype(BF16), st.astype(BF16))
    edge = cum[0:1] if reverse else cum[CHUNK - 1:CHUNK]
    kd = k * jnp.exp(edge - cum)
    st_new = st * jnp.exp(edge) + _dot_tn(v.astype(BF16), kd.astype(BF16))
    return o, st_new


def _scan0_kernel(m_ref, qf_ref, kf_ref, vf_ref, lff_ref, qb_ref, kb_ref, vb_ref, lfb_ref,
                  of_ref, ob_ref, stf_ref, stb_ref):
    @pl.when(pl.program_id(2) == 0)
    def _():
        stf_ref[...] = jnp.zeros_like(stf_ref)
        stb_ref[...] = jnp.zeros_like(stb_ref)

    mf = [m_ref[0, i] for i in range(len(_LEVELS) + 1)]
    mb = [m_ref[1, i] for i in range(len(_LEVELS) + 1)]
    stf, stb = stf_ref[...], stb_ref[...]
    n_ch = TM // CHUNK
    for c in range(n_ch):
        rf = slice(c * CHUNK, (c + 1) * CHUNK)
        rb = slice((n_ch - 1 - c) * CHUNK, (n_ch - c) * CHUNK)
        o, stf = _gated_chunk(qf_ref[0, rf, :], kf_ref[0, rf, :], vf_ref[0, rf, :], lff_ref[0, rf, :], stf, mf, False)
        of_ref[0, rf, :] = o
        o, stb = _gated_chunk(qb_ref[0, rb, :], kb_ref[0, rb, :], vb_ref[0, rb, :], lfb_ref[0, rb, :], stb, mb, True)
        ob_ref[0, rb, :] = o
    stf_ref[...] = stf
    stb_ref[...] = stb


def _bwd_block(i, n_blk):
    return jnp.where(i == 0, 0, n_blk - i)


def _scan0(masks, q, kf, kb, v, lff, lfb):
    bsz, l, wide = q.shape
    n_blk = l // TM
    n_heads = wide // HEAD_W
    fwd = pl.BlockSpec((1, TM, HEAD_W), lambda b, h, i: (b, i, h))
    bwd = pl.BlockSpec((1, TM, HEAD_W), lambda b, h, i: (b, _bwd_block(i, n_blk), h))
    return pl.pallas_call(
        _scan0_kernel,
        grid=(bsz, n_heads, n_blk),
        in_specs=[_const_spec(masks.shape), fwd, fwd, fwd, fwd, bwd, bwd, bwd, bwd],
        out_specs=[fwd, bwd],
        out_shape=[jax.ShapeDtypeStruct((bsz, l, wide), F32)] * 2,
        scratch_shapes=[pltpu.VMEM((HEAD_W, HEAD_W), F32), pltpu.VMEM((HEAD_W, HEAD_W), F32)],
        compiler_params=_cp(("parallel", "parallel", "arbitrary")),
        name="scan0",
    )(masks, q, kf, v, lff, q, kb, v, lfb)


def _head_rms(o, w):
    return o * lax.rsqrt(jnp.mean(o * o, axis=-1, keepdims=True) + EPS) * w


def _read0_kernel(x_ref, of_ref, ob_ref, og_ref, nwa_ref, nwb_ref, wout_ref, mod_ref, xo_ref, *, n_batch, d):
    b, j = pl.program_id(0), pl.program_id(1)
    (g1,) = _mod_rows(mod_ref, b, j, n_batch, d, 2, 1)
    o = of_ref[0] + ob_ref[0]
    parts = []
    for hh in range(A_HEADS + B_HEADS):
        nw = nwa_ref[...] if hh < A_HEADS else nwb_ref[...]
        parts.append(_head_rms(o[:, hh * HEAD_W:(hh + 1) * HEAD_W], nw))
    y = jnp.concatenate(parts, axis=-1) * _silu(og_ref[0])
    yo = jnp.dot(y.astype(BF16), wout_ref[...], preferred_element_type=F32)
    xo_ref[0] = x_ref[0] + g1 * yo


def _read0(x, of, ob, og, nwa, nwb, wout, mod, n_batch):
    bsz, l, d = x.shape
    wide = of.shape[-1]
    xb = pl.BlockSpec((1, TM, d), lambda b, j: (b, j, 0))
    wb = pl.BlockSpec((1, TM, wide), lambda b, j: (b, j, 0))
    return pl.pallas_call(
        functools.partial(_read0_kernel, n_batch=n_batch, d=d),
        grid=(bsz, l // TM),
        in_specs=[xb, wb, wb, wb, _const_spec(nwa.shape), _const_spec(nwb.shape),
                  _const_spec(wout.shape), _const_spec(mod.shape)],
        out_specs=xb,
        out_shape=jax.ShapeDtypeStruct(x.shape, F32),
        compiler_params=_cp(("parallel", "arbitrary")),
        name="read0",
    )(x, of, ob, og, nwa, nwb, wout, mod)


def _route_kernel(x_ref, mod_ref, nw_ref, rwt_ref, rb_ref, h_ref, idx_ref, gate_ref, *, n_batch, d, blk_off):
    b, j = pl.program_id(0), pl.program_id(1) + blk_off
    shift, scale = _mod_rows(mod_ref, b, j, n_batch, d, 3, 2)
    h = _norm_mod(x_ref[0], nw_ref[...], shift, scale)
    h_ref[0] = h
    logits = lax.dot_general(rwt_ref[...], h, (((1,), (1,)), ((), ())), precision=lax.Precision.HIGHEST,
                             preferred_element_type=F32) + rb_ref[...]
    n_exp = logits.shape[0]
    rows = lax.broadcasted_iota(jnp.int32, logits.shape, 0)
    vals, idxs = [], []
    for _ in range(TOP_K):
        m = jnp.max(logits, axis=0, keepdims=True)
        i = jnp.min(jnp.where(logits == m, rows, n_exp), axis=0, keepdims=True)
        vals.append(m)
        idxs.append(i)
        logits = jnp.where(rows == i, -jnp.inf, logits)
    ex = [jnp.exp(v - vals[0]) for v in vals]
    tot = ex[0] + ex[1] + ex[2] + ex[3]
    idx_ref[0] = jnp.concatenate(idxs, axis=0)
    gate_ref[0] = jnp.concatenate([e / tot for e in ex], axis=0)


def _route(x, mod, nw, rwt, rb, n_batch, blk_off):
    bsz, l, d = x.shape
    n_blk = l // TM - blk_off
    ls = n_blk * TM
    n_exp = rwt.shape[0]
    return pl.pallas_call(
        functools.partial(_route_kernel, n_batch=n_batch, d=d, blk_off=blk_off),
        grid=(bsz, n_blk),
        in_specs=[pl.BlockSpec((1, TM, d), lambda b, j: (b, j + blk_off, 0)),
                  _const_spec(mod.shape), _const_spec(nw.shape), _const_spec(rwt.shape), _const_spec(rb.shape)],
        out_specs=[pl.BlockSpec((1, TM, d), lambda b, j: (b, j, 0)),
                   pl.BlockSpec((1, TOP_K, TM), lambda b, j: (b, 0, j)),
                   pl.BlockSpec((1, TOP_K, TM), lambda b, j: (b, 0, j))],
        out_shape=[jax.ShapeDtypeStruct((bsz, ls, d), F32),
                   jax.ShapeDtypeStruct((bsz, TOP_K, ls), jnp.int32),
                   jax.ShapeDtypeStruct((bsz, TOP_K, ls), F32)],
        compiler_params=_cp(("parallel", "arbitrary")),
        name="route",
    )(x, mod, nw, rwt, rb)


def _row_copy(src, dst, sem):
    return pltpu.make_async_copy(src, dst, sem)


def _dispatch_kernel(pos_ref, h_ref, hs_in_ref, hs_ref, sem):
    del hs_in_ref

    def issue(r, carry):
        for k in range(TOP_K):
            _row_copy(h_ref.at[0, pl.ds(r, 1), :], hs_ref.at[pl.ds(pos_ref[0, k, r], 1), :], sem).start()
        return carry

    lax.fori_loop(0, TM, issue, 0)

    def drain(r, carry):
        for k in range(TOP_K):
            _row_copy(h_ref.at[0, pl.ds(r, 1), :], hs_ref.at[pl.ds(pos_ref[0, k, r], 1), :], sem).wait()
        return carry

    lax.fori_loop(0, TM, drain, 0)


def _dispatch(pos, h, n_rows):
    bsz, ls, d = h.shape
    zeros = jnp.zeros((n_rows, d), F32)
    return pl.pallas_call(
        _dispatch_kernel,
        grid=(bsz, ls // TM),
        in_specs=[pl.BlockSpec((1, TOP_K, TM), lambda b, j: (b, 0, j), memory_space=pltpu.SMEM),
                  pl.BlockSpec((1, TM, d), lambda b, j: (b, j, 0)),
                  pl.BlockSpec(memory_space=pl.ANY)],
        out_specs=pl.BlockSpec(memory_space=pl.ANY),
        out_shape=jax.ShapeDtypeStruct((n_rows, d), F32),
        scratch_shapes=[pltpu.SemaphoreType.DMA(())],
        input_output_aliases={2: 0},
        compiler_params=_cp(("arbitrary", "arbitrary")),
        name="dispatch",
    )(pos, h, zeros)


def _expert_kernel(te_ref, nu_ref, hs_ref, wup_ref, bup_ref, wdn_ref, bdn_ref, y_ref):
    del te_ref
    i = pl.program_id(0)

    @pl.when(i < nu_ref[0])
    def _():
        h = hs_ref[...].astype(BF16)
        ff = wdn_ref.shape[1]
        y = jnp.zeros(y_ref.shape, F32) + bdn_ref[0]
        fc = min(FF_CHUNK, ff)
        for c in range(0, ff, fc):
            glu = jnp.dot(h, wup_ref[0, :, c:c + fc], preferred_element_type=F32) + bup_ref[0, :, c:c + fc]
            lin = (jnp.dot(h, wup_ref[0, :, ff + c:ff + c + fc], preferred_element_type=F32)
                   + bup_ref[0, :, ff + c:ff + c + fc])
            glu = jnp.minimum(glu, SWIGLU_LIMIT)
            lin = jnp.clip(lin, -SWIGLU_LIMIT, SWIGLU_LIMIT)
            act = glu * jax.nn.sigmoid(SWIGLU_ALPHA * glu) * (lin + 1.0)
            y = y + jnp.dot(act.astype(BF16), wdn_ref[0, c:c + fc, :], preferred_element_type=F32)
        y_ref[...] = y

    @pl.when(i >= nu_ref[0])
    def _():
        y_ref[...] = jnp.zeros_like(y_ref)


def _experts(tile_expert, n_used, hs, wup, bup, wdn, bdn):
    n_rows, d = hs.shape
    n_tiles = n_rows // EXPERT_TILE
    ff = wdn.shape[1]
    grid_spec = pltpu.PrefetchScalarGridSpec(
        num_scalar_prefetch=2,
        grid=(n_tiles,),
        in_specs=[pl.BlockSpec((EXPERT_TILE, d), lambda i, te, nu: (i, 0)),
                  pl.BlockSpec((1, d, 2 * ff), lambda i, te, nu: (te[i], 0, 0)),
                  pl.BlockSpec((1, 1, 2 * ff), lambda i, te, nu: (te[i], 0, 0)),
                  pl.BlockSpec((1, ff, d), lambda i, te, nu: (te[i], 0, 0)),
                  pl.BlockSpec((1, 1, d), lambda i, te, nu: (te[i], 0, 0))],
        out_specs=pl.BlockSpec((EXPERT_TILE, d), lambda i, te, nu: (i, 0)),
    )
    return pl.pallas_call(
        _expert_kernel,
        grid_spec=grid_spec,
        out_shape=jax.ShapeDtypeStruct((n_rows, d), F32),
        compiler_params=_cp(("arbitrary",)),
        name="experts",
    )(tile_expert, n_used, hs, wup, bup, wdn, bdn)


def _combine_kernel(pos_ref, x_ref, gate_ref, mod_ref, fw_ref, y_hbm, xo_ref, ybuf, sem, *, n_batch, d, blk_off, final):
    b, j = pl.program_id(0), pl.program_id(1) + blk_off

    def issue(r, carry):
        for k in range(TOP_K):
            _row_copy(y_hbm.at[pl.ds(pos_ref[0, k, r], 1), :], ybuf.at[k, pl.ds(r, 1), :], sem).start()
        return carry

    lax.fori_loop(0, TM, issue, 0)

    def drain(r, carry):
        for k in range(TOP_K):
            _row_copy(y_hbm.at[pl.ds(pos_ref[0, k, r], 1), :], ybuf.at[k, pl.ds(r, 1), :], sem).wait()
        return carry

    lax.fori_loop(0, TM, drain, 0)
    (g2,) = _mod_rows(mod_ref, b, j, n_batch, d, 5, 1)
    gate = gate_ref[0]
    acc = gate[:, 0:1] * ybuf[0]
    for k in range(1, TOP_K):
        acc = acc + gate[:, k:k + 1] * ybuf[k]
    xo = x_ref[0] + g2 * acc
    if final:
        xo = xo * lax.rsqrt(jnp.mean(xo * xo, axis=-1, keepdims=True) + EPS) * fw_ref[...]
    xo_ref[0] = xo


def _combine(pos, x, gate, mod, fw, y, n_batch, blk_off, final):
    bsz, l, d = x.shape
    n_blk = l // TM - blk_off
    ls = n_blk * TM
    return pl.pallas_call(
        functools.partial(_combine_kernel, n_batch=n_batch, d=d, blk_off=blk_off, final=final),
        grid=(bsz, n_blk),
        in_specs=[pl.BlockSpec((1, TOP_K, TM), lambda b, j: (b, 0, j), memory_space=pltpu.SMEM),
                  pl.BlockSpec((1, TM, d), lambda b, j: (b, j + blk_off, 0)),
                  pl.BlockSpec((1, TM, TOP_K), lambda b, j: (b, j, 0)),
                  _const_spec(mod.shape), _const_spec(fw.shape),
                  pl.BlockSpec(memory_space=pl.ANY)],
        out_specs=pl.BlockSpec((1, TM, d), lambda b, j: (b, j, 0)),
        out_shape=jax.ShapeDtypeStruct((bsz, ls, d), F32),
        scratch_shapes=[pltpu.VMEM((TOP_K, TM, d), F32), pltpu.SemaphoreType.DMA(())],
        compiler_params=_cp(("arbitrary", "arbitrary")),
        name="combine",
    )(pos, x, gate, mod, fw, y)


def _route_plan(idx, n_exp):
    shape = idx.shape
    e = idx.reshape(-1)
    onehot = (e[:, None] == jnp.arange(n_exp, dtype=jnp.int32)[None, :]).astype(jnp.int32)
    cs = jnp.cumsum(onehot, axis=0)
    rank = jnp.sum(onehot * cs, axis=1) - 1
    counts = cs[-1]
    padded = ((counts + EXPERT_TILE - 1) // EXPERT_TILE) * EXPERT_TILE
    ends = jnp.cumsum(padded)
    starts = ends - padded
    pos = (starts[e] + rank).reshape(shape).astype(jnp.int32)
    n_rows = e.shape[0] + n_exp * EXPERT_TILE
    n_tiles = n_rows // EXPERT_TILE
    n_used = (ends[-1] // EXPERT_TILE).astype(jnp.int32)
    tile_start = jnp.minimum(jnp.arange(n_tiles, dtype=jnp.int32), n_used - 1) * EXPERT_TILE
    tile_expert = jnp.minimum(jnp.searchsorted(ends, tile_start, side="right"), n_exp - 1).astype(jnp.int32)
    return pos, tile_expert, n_used.reshape(1), n_rows


def _moe(x, mod, nw, rw, rb, wup, bup, wdn, bdn, fw, n_batch, blk_off, final):
    n_exp = rw.shape[1]
    d = x.shape[-1]
    h, idx, gate = _route(x, mod, nw, rw.T, rb.reshape(n_exp, 1), n_batch, blk_off)
    pos, tile_expert, n_used, n_rows = _route_plan(idx, n_exp)
    hs = _dispatch(pos, h, n_rows)
    y = _experts(tile_expert, n_used, hs, wup.astype(BF16), bup.reshape(n_exp, 1, -1),
                 wdn.astype(BF16), bdn.reshape(n_exp, 1, d))
    return _combine(pos, x, jnp.transpose(gate, (0, 2, 1)), mod, fw, y, n_batch, blk_off, final)


def _pad_heads(w, n_heads, width):
    lead = w.shape[:-1]
    dh = w.shape[-1] // n_heads
    w = w.reshape(lead + (n_heads, dh))
    w = jnp.pad(w, [(0, 0)] * len(lead) + [(0, 0), (0, width - dh)])
    return w.reshape(lead + (n_heads * width,))


def _layer0_weights(w_in, w2_f, b2_f, w2_b, b2_b):
    d = w_in.shape[0]
    sizes = (A_HEADS * A_DK, A_HEADS * A_DK, A_HEADS * A_DV, A_RANK, A_RANK, A_HEADS * A_DV,
             B_HEADS * B_DK, B_HEADS * B_DK, B_HEADS * B_DK, B_HEADS * B_DV, B_HEADS * B_DV)
    aq, ak, av, ar_f, ar_b, aog, bq, bz_f, bz_b, bi, bog = jnp.split(w_in, np.cumsum(sizes)[:-1].tolist(), axis=1)
    ar = jnp.concatenate([ar_f, ar_b, jnp.zeros((d, HEAD_W - 2 * A_RANK), w_in.dtype)], axis=1)
    w = jnp.concatenate([_pad_heads(aq, A_HEADS, HEAD_W), bq, av, bi, aog, bog,
                         _pad_heads(ak, A_HEADS, HEAD_W), bz_f, bz_b, ar], axis=1).astype(BF16)
    hw = A_HEADS * HEAD_W
    w2 = jnp.zeros((2, HEAD_W, hw), F32)
    w2 = w2.at[0, 0:A_RANK].set(_pad_heads(w2_f, A_HEADS, HEAD_W))
    w2 = w2.at[1, A_RANK:2 * A_RANK].set(_pad_heads(w2_b, A_HEADS, HEAD_W))
    b2 = jnp.stack([_pad_heads(b2_f, A_HEADS, HEAD_W), _pad_heads(b2_b, A_HEADS, HEAD_W)]).reshape(2, 1, hw)
    return w, w2.astype(BF16), b2


def _t_heads(x, n):
    b, t, _ = x.shape
    return x.reshape(b, t, n, -1).transpose(0, 2, 1, 3)


def _t_merge(x):
    b, n, t, dd = x.shape
    return x.transpose(0, 2, 1, 3).reshape(b, t, n * dd)


def _t_l2(x):
    return x * lax.rsqrt(jnp.sum(x * x, axis=-1, keepdims=True) + EPS)


def _t_conv(x, w):
    pad = CONV_W // 2
    return lax.conv_general_dilated(x, w[:, None, :].astype(x.dtype), window_strides=(1,), padding=[(pad, pad)],
                                    dimension_numbers=('NWC', 'WIO', 'NWC'), feature_group_count=x.shape[-1])


def _t_delta(q, k, v, g, beta, s0, with_output):
    b_, h_, t_, dk = q.shape
    n = t_ // CHUNK
    q, k, v = [a.reshape(b_, h_, n, CHUNK, a.shape[-1]) for a in (q, k, v)]
    g = g.reshape(b_, h_, n, CHUNK)
    beta = beta.reshape(b_, h_, n, CHUNK)
    gam = jnp.cumsum(g, axis=-1)
    incl = jnp.tril(jnp.ones((CHUNK, CHUNK), dtype=bool))
    strict = jnp.tril(jnp.ones((CHUNK, CHUNK), dtype=bool), -1)
    diff = gam[..., :, None] - gam[..., None, :]
    lmat = jnp.where(incl, jnp.exp(jnp.where(incl, diff, 0.0)), 0.0)
    kb = k * beta[..., None]
    a_mat = jnp.where(strict, jnp.einsum('bhntk,bhnsk->bhnts', kb, k) * lmat, 0.0)
    eye = jnp.eye(CHUNK, dtype=q.dtype)
    t_inv = lax.linalg.triangular_solve(a_mat + eye, jnp.broadcast_to(eye, a_mat.shape), left_side=True, lower=True,
                                        unit_diagonal=True)
    u = jnp.einsum('bhnts,bhnsv->bhntv', t_inv, v * beta[..., None])
    w = jnp.einsum('bhnts,bhnsk->bhntk', t_inv, kb * jnp.exp(gam)[..., None])

    def step(s, inp):
        qc, kc, uc, wc, gc, lc = inp
        v_new = uc - jnp.einsum('bhtk,bhkv->bhtv', wc, s)
        o = None
        if with_output:
            scores = jnp.einsum('bhtk,bhsk->bhts', qc, kc) * lc
            o = (jnp.einsum('bhtk,bhkv->bhtv', qc * jnp.exp(gc)[..., None], s)
                 + jnp.einsum('bhts,bhsv->bhtv', scores, v_new))
        g_last = gc[..., -1]
        s = (jnp.exp(g_last)[..., None, None] * s
             + jnp.einsum('bhsk,bhsv->bhkv', kc * jnp.exp(g_last[..., None] - gc)[..., None], v_new))
        return s, o

    xs = tuple(jnp.moveaxis(a, 2, 0) for a in (q, k, u, w, gam, lmat))
    s, o = lax.scan(step, s0, xs)
    if with_output:
        o = jnp.moveaxis(o, 0, 2).reshape(b_, h_, t_, -1)
    return s, o


def _t_window_attention(q, k, v, kc, vc, sinks):
    b_, kvh, g_, t_, dd = q.shape
    nb = t_ // WINDOW
    qb = q.reshape(b_, kvh, g_, nb, WINDOW, dd)

    def band(a):
        ap = jnp.pad(a, ((0, 0), (0, 0), (WINDOW, WINDOW), (0, 0))).reshape(b_, kvh, nb + 2, WINDOW, dd)
        return jnp.concatenate([ap[:, :, :-2], ap[:, :, 1:-1], ap[:, :, 2:]], axis=3)

    kband, vband = band(k), band(v)
    qpos = jnp.arange(WINDOW)
    kpos = jnp.arange(3 * WINDOW) - WINDOW
    rel = kpos[None, :] - qpos[:, None]
    abs_k = (jnp.arange(nb) * WINDOW)[:, None] + kpos[None, :]
    valid = ((jnp.abs(rel) <= WINDOW)[None] & ((abs_k >= 0) & (abs_k < t_))[:, None, :])
    scale = dd ** -0.5
    s_loc = jnp.einsum('bhgnqd,bhnkd->bhgnqk', qb, kband) * scale
    s_loc = jnp.where(valid, s_loc, -jnp.inf)
    s_ctx = jnp.einsum('bhgnqd,bhcd->bhgnqc', qb, kc) * scale
    sink = jnp.broadcast_to(sinks.astype(F32).reshape(kvh, g_)[None, :, :, None, None, None], (b_, kvh, g_, nb, WINDOW, 1))
    p = jax.nn.softmax(jnp.concatenate([sink, s_loc, s_ctx], axis=-1), axis=-1)
    p_loc = p[..., 1:1 + 3 * WINDOW]
    p_ctx = p[..., 1 + 3 * WINDOW:]
    out = (jnp.einsum('bhgnqk,bhnkd->bhgnqd', p_loc, vband) + jnp.einsum('bhgnqc,bhcd->bhgnqd', p_ctx, vc))
    return out.reshape(b_, kvh, g_, t_, dd)


def _t_mixer1(hc, hx, w_in, w_out, conv_w, a_log_f, dt_bias_f, a_log_b, dt_bias_b, dn_norm_w, sinks):
    t_ = hx.shape[1]
    rows = t_ // GRID_W
    row = jnp.repeat(jnp.arange(rows, dtype=F32), GRID_W)
    col = jnp.tile(jnp.arange(GRID_W, dtype=F32), rows)
    n_freq = D_HD // 4
    inv = ROPE_BASE ** (-jnp.arange(n_freq, dtype=F32) / n_freq)
    ang = jnp.concatenate([row[:, None] * inv, col[:, None] * inv], axis=-1)
    cos, sin = jnp.cos(ang), jnp.sin(ang)
    splits = (C_HEADS * C_DK, C_HEADS * C_DK, C_HEADS * C_DV, C_HEADS, C_HEADS, C_HEADS, C_HEADS,
              C_HEADS * C_DV, D_HEADS * D_HD, D_KV_HEADS * D_HD, D_KV_HEADS * D_HD)

    def rope(x):
        x1, x2 = jnp.split(x, 2, axis=-1)
        return jnp.concatenate([x1 * cos - x2 * sin, x2 * cos + x1 * sin], axis=-1)

    def prep(h, use_rope):
        cq, ck, cv, bt_f, bt_b, a_f, a_b, og, dq, dk, dv = jnp.split(h @ w_in, np.cumsum(splits)[:-1].tolist(), axis=-1)
        qkv = _silu(_t_conv(jnp.concatenate([cq, ck, cv], axis=-1), conv_w))
        cq, ck, cv = jnp.split(qkv, [C_HEADS * C_DK, 2 * C_HEADS * C_DK], axis=-1)
        cq = _t_l2(_t_heads(cq, C_HEADS)) * C_DK ** -0.5
        ck = _t_l2(_t_heads(ck, C_HEADS))
        cv = _t_heads(cv, C_HEADS)

        def c_dir(bt, a, a_log, dt_bias):
            g = -jnp.exp(a_log) * jax.nn.softplus(a + dt_bias)
            return (cq, ck, cv, g.transpose(0, 2, 1), jax.nn.sigmoid(bt).transpose(0, 2, 1))

        dq = _t_heads(dq, D_HEADS)
        dk = _t_heads(dk, D_KV_HEADS)
        dv = _t_heads(dv, D_KV_HEADS)
        if use_rope:
            dq, dk = rope(dq), rope(dk)
        b_, _, tt, _ = dq.shape
        dq = dq.reshape(b_, D_KV_HEADS, D_HEADS // D_KV_HEADS, tt, D_HD)
        return c_dir(bt_f, a_f, a_log_f, dt_bias_f), c_dir(bt_b, a_b, a_log_b, dt_bias_b), og, dq, dk, dv

    cc_f, cc_b, _, _, c_dk, c_dv = prep(hc, False)
    xc_f, xc_b, x_og, x_dq, x_dk, x_dv = prep(hx, True)
    s0 = jnp.zeros((hx.shape[0], C_HEADS, C_DK, C_DV), F32)
    flip = lambda a: jnp.flip(a, axis=2)
    sc_f, _ = _t_delta(*cc_f, s0, False)
    _, ox_f = _t_delta(*xc_f, sc_f, True)
    sc_b, _ = _t_delta(*[flip(a) for a in cc_b], s0, False)
    _, ox_b = _t_delta(*[flip(a) for a in xc_b], sc_b, True)
    ox_c = ox_f + flip(ox_b)
    ox_d = _t_window_attention(x_dq, x_dk, x_dv, c_dk, c_dv, sinks)
    o = ox_c * lax.rsqrt(jnp.mean(ox_c * ox_c, axis=-1, keepdims=True) + EPS) * dn_norm_w
    y = jnp.concatenate([_t_merge(o) * _silu(x_og),
                         _t_merge(ox_d.reshape(hx.shape[0], D_HEADS, t_, D_HD))], axis=-1)
    return y @ w_out


def kernel(x, c, ctx, c_ctx, l0_ada_w, l0_ada_b, l0_norm_mix_w, l0_w_in, l0_w_out, l0_gla_w2_f, l0_gla_b_f, l0_gla_w2_b, l0_gla_b_b, l0_gla_norm_w, l0_hgrn_norm_w, hgrn_lb_logits, l0_norm_ffn_w, l0_router_w, l0_router_b, l0_w_up, l0_b_up, l0_w_down, l0_b_down, l1_ada_w, l1_ada_b, l1_norm_mix_w, l1_w_in, l1_w_out, l1_conv_w, l1_a_log_f, l1_dt_bias_f, l1_a_log_b, l1_dt_bias_b, l1_dn_norm_w, l1_sinks, l1_norm_ffn_w, l1_router_w, l1_router_b, l1_w_up, l1_b_up, l1_w_down, l1_b_down, final_norm_w):
    bsz, t, d = x.shape
    n_ctx = ctx.shape[1]
    assert n_ctx == TM and t % TM == 0
    xx = jnp.concatenate([ctx, x], axis=1)
    mod_rows = -(-(bsz + 1) // 8) * 8
    c_all = jnp.zeros((mod_rows, d), F32).at[:bsz].set(c).at[bsz].set(c_ctx)
    ones = jnp.ones((1, d), F32)

    mod0 = _ada_table(c_all, l0_ada_w, l0_ada_b)
    w0, w2, b2 = _layer0_weights(l0_w_in, l0_gla_w2_f, l0_gla_b_f, l0_gla_w2_b, l0_gla_b_b)
    q, kf, kb, v, lff, lfb, og = _proj0(xx, mod0, l0_norm_mix_w.reshape(1, d), w0, w2, b2, hgrn_lb_logits, bsz)
    of, ob = _scan0(jnp.asarray(_level_masks()), q, kf, kb, v, lff, lfb)
    xx = _read0(xx, of, ob, og, l0_gla_norm_w.reshape(1, -1), l0_hgrn_norm_w.reshape(1, -1), l0_w_out.astype(BF16), mod0, bsz)
    xx = _moe(xx, mod0, l0_norm_ffn_w.reshape(1, d), l0_router_w, l0_router_b, l0_w_up, l0_b_up, l0_w_down, l0_b_down,
              ones, bsz, 0, False)

    mod1 = _ada_table(c_all, l1_ada_w, l1_ada_b)
    nrm = lambda a: a * lax.rsqrt(jnp.mean(a * a, axis=-1, keepdims=True) + EPS) * l1_norm_mix_w
    hc = nrm(xx[:, :n_ctx]) * (1.0 + mod1[bsz, d:2 * d]) + mod1[bsz, 0:d]
    hx = nrm(xx[:, n_ctx:]) * (1.0 + mod1[:bsz, None, d:2 * d]) + mod1[:bsz, None, 0:d]
    yx = _t_mixer1(hc, hx, l1_w_in, l1_w_out, l1_conv_w, l1_a_log_f, l1_dt_bias_f, l1_a_log_b, l1_dt_bias_b,
                   l1_dn_norm_w, l1_sinks)
    xl = xx[:, n_ctx:] + mod1[:bsz, None, 2 * d:3 * d] * yx
    xx = jnp.concatenate([xx[:, :n_ctx], xl], axis=1)
    return _moe(xx, mod1, l1_norm_ffn_w.reshape(1, d), l1_router_w, l1_router_b, l1_w_up, l1_b_up, l1_w_down, l1_b_down,
                final_norm_w.reshape(1, d), bsz, 1, True)
```

```python
import functools

import numpy as np
import jax
import jax.numpy as jnp
from jax import lax
from jax.experimental import pallas as pl
from jax.experimental.pallas import tpu as pltpu

F32 = jnp.float32
BF16 = jnp.bfloat16
EPS = 1e-6

CHUNK = 64
A_HEADS, A_DK, A_DV, A_RANK = 4, 64, 128, 16
GATE_TAU = 16.0
B_HEADS, B_DK, B_DV = 4, 128, 128
C_HEADS, C_DK, C_DV = 4, 128, 128
CONV_W = 5
D_HEADS, D_KV_HEADS, D_HD = 8, 2, 64
WINDOW = 128
GRID_W = 64
ROPE_BASE = 10000.0
TOP_K = 4
SWIGLU_LIMIT = 7.0
SWIGLU_ALPHA = 1.702

TM = 256
HEAD_W = 128
EXPERT_TILE = 512
FF_CHUNK = 512
VMEM_LIMIT = 48 * 1024 * 1024


def _cp(sem, vmem=VMEM_LIMIT):
    return pltpu.CompilerParams(dimension_semantics=sem, vmem_limit_bytes=vmem)


def _const_spec(shape):
    nd = len(shape)
    return pl.BlockSpec(shape, lambda *_: (0,) * nd)


def _silu(x):
    return x * jax.nn.sigmoid(x)


def _log_sigmoid(z):
    return jnp.minimum(z, 0.0) - jnp.log1p(jnp.exp(-jnp.abs(z)))


def _norm_mod(x, nw, shift, scale):
    y = x * lax.rsqrt(jnp.mean(x * x, axis=-1, keepdims=True) + EPS) * nw
    return y * (1.0 + scale) + shift


def _mod_rows(mod_ref, batch, blk, n_batch, d, first, count):
    row = jnp.where(blk == 0, n_batch, batch)
    return [mod_ref[pl.ds(row, 1), (first + i) * d:(first + i + 1) * d] for i in range(count)]


def _ada_kernel(c_ref, w_ref, b_ref, o_ref):
    s = _silu(c_ref[...])
    o_ref[...] = jnp.dot(s, w_ref[...], precision=lax.Precision.HIGHEST,
                         preferred_element_type=F32) + b_ref[...]


def _ada_table(c_all, w, b):
    rows, d = c_all.shape
    n = w.shape[1]
    bn = d
    return pl.pallas_call(
        _ada_kernel,
        grid=(n // bn,),
        in_specs=[pl.BlockSpec((rows, d), lambda j: (0, 0)),
                  pl.BlockSpec((d, bn), lambda j: (0, j)),
                  pl.BlockSpec((1, bn), lambda j: (0, j))],
        out_specs=pl.BlockSpec((rows, bn), lambda j: (0, j)),
        out_shape=jax.ShapeDtypeStruct((rows, n), F32),
        compiler_params=_cp(("arbitrary",)),
        name="ada_table",
    )(c_all, w, b.reshape(1, n))


def _proj0_kernel(x_ref, mod_ref, nw_ref, w_ref, w2_ref, b2_ref, lbl_ref,
                  q_ref, kf_ref, kb_ref, v_ref, lff_ref, lfb_ref, og_ref, *, n_batch, d):
    b, j = pl.program_id(0), pl.program_id(1)
    shift, scale = _mod_rows(mod_ref, b, j, n_batch, d, 0, 2)
    h = _norm_mod(x_ref[0], nw_ref[...], shift, scale).astype(BF16)

    def mm(c0, c1):
        return jnp.dot(h, w_ref[:, c0:c1], preferred_element_type=F32)

    hw = A_HEADS * HEAD_W
    q_ref[0, :, 0:hw] = mm(0, hw) * (A_DK ** -0.5)
    q_ref[0, :, hw:2 * hw] = mm(hw, 2 * hw)
    v_ref[0] = mm(2 * hw, 4 * hw)
    og_ref[0] = mm(4 * hw, 6 * hw)
    kg = mm(6 * hw, 7 * hw)
    kf_ref[0, :, 0:hw] = kg
    kb_ref[0, :, 0:hw] = kg
    lg = lbl_ref[...]
    e = jnp.exp(lg - jnp.max(lg, axis=0, keepdims=True))
    lb = e[0:1] / jnp.sum(e, axis=0, keepdims=True)
    log_lb, log_1m = jnp.log(lb), jnp.log1p(-lb)
    ar = mm(9 * hw, 9 * hw + HEAD_W).astype(BF16)
    for di, (k_ref, lf_ref) in enumerate(((kf_ref, lff_ref), (kb_ref, lfb_ref))):
        z = mm((7 + di) * hw, (8 + di) * hw)
        s1 = log_1m + _log_sigmoid(z)
        lf_ref[0, :, hw:2 * hw] = jnp.maximum(log_lb, s1) + jnp.log1p(jnp.exp(-jnp.abs(log_lb - s1)))
        k_ref[0, :, hw:2 * hw] = (1.0 - lb) * jax.nn.sigmoid(-z)
        za = jnp.dot(ar, w2_ref[di], preferred_element_type=F32) + b2_ref[di]
        lf_ref[0, :, 0:hw] = _log_sigmoid(za) * (1.0 / GATE_TAU)


def _proj0(x, mod, nw, w, w2, b2, lbl, n_batch):
    bsz, l, d = x.shape
    n_blk = l // TM
    wide = 2 * A_HEADS * HEAD_W
    blk = pl.BlockSpec((1, TM, wide), lambda b, j: (b, j, 0))
    return pl.pallas_call(
        functools.partial(_proj0_kernel, n_batch=n_batch, d=d),
        grid=(bsz, n_blk),
        in_specs=[pl.BlockSpec((1, TM, d), lambda b, j: (b, j, 0)),
                  _const_spec(mod.shape), _const_spec(nw.shape), _const_spec(w.shape),
                  _const_spec(w2.shape), _const_spec(b2.shape), _const_spec(lbl.shape)],
        out_specs=[blk] * 7,
        out_shape=[jax.ShapeDtypeStruct((bsz, l, wide), F32)] * 7,
        compiler_params=_cp(("parallel", "arbitrary")),
        name="proj0",
    )(x, mod, nw, w, w2, b2, lbl)


_LEVELS = (32, 16, 8, 4, 2, 1)


def _level_masks():
    t = np.arange(CHUNK)[:, None]
    s = np.arange(CHUNK)[None, :]
    out = np.zeros((2, len(_LEVELS) + 1, CHUNK, CHUNK), np.float32)
    for li, m in enumerate(_LEVELS):
        same = (t // (2 * m)) == (s // (2 * m))
        fwd = same & (t % (2 * m) >= m) & (s % (2 * m) < m)
        out[0, li] = fwd
        out[1, li] = fwd.T
    out[:, -1] = np.eye(CHUNK)
    return out


def _cumsum_rows(x, reverse):
    n = x.shape[0]
    r = lax.broadcasted_iota(jnp.int32, x.shape, 0)
    sh = 1
    while sh < n:
        if reverse:
            x = x + jnp.where(r < n - sh, pltpu.roll(x, n - sh, 0), 0.0)
        else:
            x = x + jnp.where(r >= sh, pltpu.roll(x, sh, 0), 0.0)
        sh *= 2
    return x


def _level_ref(cum, m, reverse):
    n = cum.shape[0]
    tgt = m if reverse else m - 1
    if 2 * m >= 8:
        parts = [jnp.broadcast_to(cum[g + tgt:g + tgt + 1, :], (2 * m, cum.shape[1]))
                 for g in range(0, n, 2 * m)]
        return parts[0] if len(parts) == 1 else jnp.concatenate(parts, axis=0)
    pos = lax.broadcasted_iota(jnp.int32, cum.shape, 0) % (2 * m)
    out = cum
    for p in range(2 * m):
        if p == tgt:
            continue
        shift = (p - tgt) % n
        out = jnp.where(pos == p, pltpu.roll(cum, shift, 0), out)
    return out


def _dot_nt(a, b):
    return lax.dot_general(a, b, (((1,), (1,)), ((), ())), preferred_element_type=F32)


def _dot_tn(a, b):
    return lax.dot_general(a, b, (((0,), (0,)), ((), ())), preferred_element_type=F32)


def _gated_chunk(q, k, v, lf, st, masks, reverse):
    cum = _cumsum_rows(lf, reverse)
    scores = masks[len(_LEVELS)] * _dot_nt(q.astype(BF16), k.astype(BF16))
    for li, m in enumerate(_LEVELS):
        w = jnp.exp(-jnp.abs(cum - _level_ref(cum, m, reverse)))
        scores = scores + masks[li] * _dot_nt((q * w).astype(BF16), (k * w).astype(BF16))
    o = jnp.dot(scores.astype(BF16), v.astype(BF16), preferred_element_type=F32)
    o = o + _dot_nt((q * jnp.exp(cum)).astype(BF16), st.astype(BF16))
    edge = cum[0:1] if reverse else cum[CHUNK - 1:CHUNK]
    kd = k * jnp.exp(edge - cum)
    st_new = st * jnp.exp(edge) + _dot_tn(v.astype(BF16), kd.astype(BF16))
    return o, st_new


def _scan0_kernel(m_ref, qf_ref, kf_ref, vf_ref, lff_ref, qb_ref, kb_ref, vb_ref, lfb_ref,
                  of_ref, ob_ref, stf_ref, stb_ref):
    @pl.when(pl.program_id(2) == 0)
    def _():
        stf_ref[...] = jnp.zeros_like(stf_ref)
        stb_ref[...] = jnp.zeros_like(stb_ref)

    mf = [m_ref[0, i] for i in range(len(_LEVELS) + 1)]
    mb = [m_ref[1, i] for i in range(len(_LEVELS) + 1)]
    stf, stb = stf_ref[...], stb_ref[...]
    n_ch = TM // CHUNK
    for c in range(n_ch):
        rf = slice(c * CHUNK, (c + 1) * CHUNK)
        rb = slice((n_ch - 1 - c) * CHUNK, (n_ch - c) * CHUNK)
        o, stf = _gated_chunk(qf_ref[0, rf, :], kf_ref[0, rf, :], vf_ref[0, rf, :], lff_ref[0, rf, :], stf, mf, False)
        of_ref[0, rf, :] = o
        o, stb = _gated_chunk(qb_ref[0, rb, :], kb_ref[0, rb, :], vb_ref[0, rb, :], lfb_ref[0, rb, :], stb, mb, True)
        ob_ref[0, rb, :] = o
    stf_ref[...] = stf
    stb_ref[...] = stb


def _bwd_block(i, n_blk):
    return jnp.where(i == 0, 0, n_blk - i)


def _scan0(masks, q, kf, kb, v, lff, lfb):
    bsz, l, wide = q.shape
    n_blk = l // TM
    n_heads = wide // HEAD_W
    fwd = pl.BlockSpec((1, TM, HEAD_W), lambda b, h, i: (b, i, h))
    bwd = pl.BlockSpec((1, TM, HEAD_W), lambda b, h, i: (b, _bwd_block(i, n_blk), h))
    return pl.pallas_call(
        _scan0_kernel,
        grid=(bsz, n_heads, n_blk),
        in_specs=[_const_spec(masks.shape), fwd, fwd, fwd, fwd, bwd, bwd, bwd, bwd],
        out_specs=[fwd, bwd],
        out_shape=[jax.ShapeDtypeStruct((bsz, l, wide), F32)] * 2,
        scratch_shapes=[pltpu.VMEM((HEAD_W, HEAD_W), F32), pltpu.VMEM((HEAD_W, HEAD_W), F32)],
        compiler_params=_cp(("parallel", "parallel", "arbitrary")),
        name="scan0",
    )(masks, q, kf, v, lff, q, kb, v, lfb)


def _head_rms(o, w):
    return o * lax.rsqrt(jnp.mean(o * o, axis=-1, keepdims=True) + EPS) * w


def _read0_kernel(x_ref, of_ref, ob_ref, og_ref, nwa_ref, nwb_ref, wout_ref, mod_ref, xo_ref, *, n_batch, d):
    b, j = pl.program_id(0), pl.program_id(1)
    (g1,) = _mod_rows(mod_ref, b, j, n_batch, d, 2, 1)
    o = of_ref[0] + ob_ref[0]
    parts = []
    for hh in range(A_HEADS + B_HEADS):
        nw = nwa_ref[...] if hh < A_HEADS else nwb_ref[...]
        parts.append(_head_rms(o[:, hh * HEAD_W:(hh + 1) * HEAD_W], nw))
    y = jnp.concatenate(parts, axis=-1) * _silu(og_ref[0])
    yo = jnp.dot(y.astype(BF16), wout_ref[...], preferred_element_type=F32)
    xo_ref[0] = x_ref[0] + g1 * yo


def _read0(x, of, ob, og, nwa, nwb, wout, mod, n_batch):
    bsz, l, d = x.shape
    wide = of.shape[-1]
    xb = pl.BlockSpec((1, TM, d), lambda b, j: (b, j, 0))
    wb = pl.BlockSpec((1, TM, wide), lambda b, j: (b, j, 0))
    return pl.pallas_call(
        functools.partial(_read0_kernel, n_batch=n_batch, d=d),
        grid=(bsz, l // TM),
        in_specs=[xb, wb, wb, wb, _const_spec(nwa.shape), _const_spec(nwb.shape),
                  _const_spec(wout.shape), _const_spec(mod.shape)],
        out_specs=xb,
        out_shape=jax.ShapeDtypeStruct(x.shape, F32),
        compiler_params=_cp(("parallel", "arbitrary")),
        name="read0",
    )(x, of, ob, og, nwa, nwb, wout, mod)


def _route_kernel(x_ref, mod_ref, nw_ref, rwt_ref, rb_ref, h_ref, idx_ref, gate_ref, *, n_batch, d, blk_off):
    b, j = pl.program_id(0), pl.program_id(1) + blk_off
    shift, scale = _mod_rows(mod_ref, b, j, n_batch, d, 3, 2)
    h = _norm_mod(x_ref[0], nw_ref[...], shift, scale)
    h_ref[0] = h
    logits = lax.dot_general(rwt_ref[...], h, (((1,), (1,)), ((), ())), precision=lax.Precision.HIGHEST,
                             preferred_element_type=F32) + rb_ref[...]
    n_exp = logits.shape[0]
    rows = lax.broadcasted_iota(jnp.int32, logits.shape, 0)
    vals, idxs = [], []
    for _ in range(TOP_K):
        m = jnp.max(logits, axis=0, keepdims=True)
        i = jnp.min(jnp.where(logits == m, rows, n_exp), axis=0, keepdims=True)
        vals.append(m)
        idxs.append(i)
        logits = jnp.where(rows == i, -jnp.inf, logits)
    ex = [jnp.exp(v - vals[0]) for v in vals]
    tot = ex[0] + ex[1] + ex[2] + ex[3]
    idx_ref[0] = jnp.concatenate(idxs, axis=0)
    gate_ref[0] = jnp.concatenate([e / tot for e in ex], axis=0)


def _route(x, mod, nw, rwt, rb, n_batch, blk_off):
    bsz, l, d = x.shape
    n_blk = l // TM - blk_off
    ls = n_blk * TM
    n_exp = rwt.shape[0]
    return pl.pallas_call(
        functools.partial(_route_kernel, n_batch=n_batch, d=d, blk_off=blk_off),
        grid=(bsz, n_blk),
        in_specs=[pl.BlockSpec((1, TM, d), lambda b, j: (b, j + blk_off, 0)),
                  _const_spec(mod.shape), _const_spec(nw.shape), _const_spec(rwt.shape), _const_spec(rb.shape)],
        out_specs=[pl.BlockSpec((1, TM, d), lambda b, j: (b, j, 0)),
                   pl.BlockSpec((1, TOP_K, TM), lambda b, j: (b, 0, j)),
                   pl.BlockSpec((1, TOP_K, TM), lambda b, j: (b, 0, j))],
        out_shape=[jax.ShapeDtypeStruct((bsz, ls, d), F32),
                   jax.ShapeDtypeStruct((bsz, TOP_K, ls), jnp.int32),
                   jax.ShapeDtypeStruct((bsz, TOP_K, ls), F32)],
        compiler_params=_cp(("parallel", "arbitrary")),
        name="route",
    )(x, mod, nw, rwt, rb)


def _row_copy(src, dst, sem):
    return pltpu.make_async_copy(src, dst, sem)


def _dispatch_kernel(pos_ref, h_ref, hs_in_ref, hs_ref, sem):
    del hs_in_ref

    def issue(r, carry):
        for k in range(TOP_K):
            _row_copy(h_ref.at[0, pl.ds(r, 1), :], hs_ref.at[pl.ds(pos_ref[0, k, r], 1), :], sem).start()
        return carry

    lax.fori_loop(0, TM, issue, 0)

    def drain(r, carry):
        for k in range(TOP_K):
            _row_copy(h_ref.at[0, pl.ds(r, 1), :], hs_ref.at[pl.ds(pos_ref[0, k, r], 1), :], sem).wait()
        return carry

    lax.fori_loop(0, TM, drain, 0)


def _dispatch(pos, h, n_rows):
    bsz, ls, d = h.shape
    zeros = jnp.zeros((n_rows, d), F32)
    return pl.pallas_call(
        _dispatch_kernel,
        grid=(bsz, ls // TM),
        in_specs=[pl.BlockSpec((1, TOP_K, TM), lambda b, j: (b, 0, j), memory_space=pltpu.SMEM),
                  pl.BlockSpec((1, TM, d), lambda b, j: (b, j, 0)),
                  pl.BlockSpec(memory_space=pl.ANY)],
        out_specs=pl.BlockSpec(memory_space=pl.ANY),
        out_shape=jax.ShapeDtypeStruct((n_rows, d), F32),
        scratch_shapes=[pltpu.SemaphoreType.DMA(())],
        input_output_aliases={2: 0},
        compiler_params=_cp(("arbitrary", "arbitrary")),
        name="dispatch",
    )(pos, h, zeros)


def _expert_kernel(te_ref, nu_ref, hs_ref, wup_ref, bup_ref, wdn_ref, bdn_ref, y_ref):
    del te_ref
    i = pl.program_id(0)

    @pl.when(i < nu_ref[0])
    def _():
        h = hs_ref[...].astype(BF16)
        ff = wdn_ref.shape[1]
        y = jnp.zeros(y_ref.shape, F32) + bdn_ref[0]
        fc = min(FF_CHUNK, ff)
        for c in range(0, ff, fc):
            glu = jnp.dot(h, wup_ref[0, :, c:c + fc], preferred_element_type=F32) + bup_ref[0, :, c:c + fc]
            lin = (jnp.dot(h, wup_ref[0, :, ff + c:ff + c + fc], preferred_element_type=F32)
                   + bup_ref[0, :, ff + c:ff + c + fc])
            glu = jnp.minimum(glu, SWIGLU_LIMIT)
            lin = jnp.clip(lin, -SWIGLU_LIMIT, SWIGLU_LIMIT)
            act = glu * jax.nn.sigmoid(SWIGLU_ALPHA * glu) * (lin + 1.0)
            y = y + jnp.dot(act.astype(BF16), wdn_ref[0, c:c + fc, :], preferred_element_type=F32)
        y_ref[...] = y

    @pl.when(i >= nu_ref[0])
    def _():
        y_ref[...] = jnp.zeros_like(y_ref)


def _experts(tile_expert, n_used, hs, wup, bup, wdn, bdn):
    n_rows, d = hs.shape
    n_tiles = n_rows // EXPERT_TILE
    ff = wdn.shape[1]
    grid_spec = pltpu.PrefetchScalarGridSpec(
        num_scalar_prefetch=2,
        grid=(n_tiles,),
        in_specs=[pl.BlockSpec((EXPERT_TILE, d), lambda i, te, nu: (i, 0)),
                  pl.BlockSpec((1, d, 2 * ff), lambda i, te, nu: (te[i], 0, 0)),
                  pl.BlockSpec((1, 1, 2 * ff), lambda i, te, nu: (te[i], 0, 0)),
                  pl.BlockSpec((1, ff, d), lambda i, te, nu: (te[i], 0, 0)),
                  pl.BlockSpec((1, 1, d), lambda i, te, nu: (te[i], 0, 0))],
        out_specs=pl.BlockSpec((EXPERT_TILE, d), lambda i, te, nu: (i, 0)),
    )
    return pl.pallas_call(
        _expert_kernel,
        grid_spec=grid_spec,
        out_shape=jax.ShapeDtypeStruct((n_rows, d), F32),
        compiler_params=_cp(("arbitrary",)),
        name="experts",
    )(tile_expert, n_used, hs, wup, bup, wdn, bdn)


def _combine_kernel(pos_ref, x_ref, gate_ref, mod_ref, fw_ref, y_hbm, xo_ref, ybuf, sem, *, n_batch, d, blk_off, final):
    b, j = pl.program_id(0), pl.program_id(1) + blk_off

    def issue(r, carry):
        for k in range(TOP_K):
            _row_copy(y_hbm.at[pl.ds(pos_ref[0, k, r], 1), :], ybuf.at[k, pl.ds(r, 1), :], sem).start()
        return carry

    lax.fori_loop(0, TM, issue, 0)

    def drain(r, carry):
        for k in range(TOP_K):
            _row_copy(y_hbm.at[pl.ds(pos_ref[0, k, r], 1), :], ybuf.at[k, pl.ds(r, 1), :], sem).wait()
        return carry

    lax.fori_loop(0, TM, drain, 0)
    (g2,) = _mod_rows(mod_ref, b, j, n_batch, d, 5, 1)
    gate = gate_ref[0]
    acc = gate[:, 0:1] * ybuf[0]
    for k in range(1, TOP_K):
        acc = acc + gate[:, k:k + 1] * ybuf[k]
    xo = x_ref[0] + g2 * acc
    if final:
        xo = xo * lax.rsqrt(jnp.mean(xo * xo, axis=-1, keepdims=True) + EPS) * fw_ref[...]
    xo_ref[0] = xo


def _combine(pos, x, gate, mod, fw, y, n_batch, blk_off, final):
    bsz, l, d = x.shape
    n_blk = l // TM - blk_off
    ls = n_blk * TM
    return pl.pallas_call(
        functools.partial(_combine_kernel, n_batch=n_batch, d=d, blk_off=blk_off, final=final),
        grid=(bsz, n_blk),
        in_specs=[pl.BlockSpec((1, TOP_K, TM), lambda b, j: (b, 0, j), memory_space=pltpu.SMEM),
                  pl.BlockSpec((1, TM, d), lambda b, j: (b, j + blk_off, 0)),
                  pl.BlockSpec((1, TM, TOP_K), lambda b, j: (b, j, 0)),
                  _const_spec(mod.shape), _const_spec(fw.shape),
                  pl.BlockSpec(memory_space=pl.ANY)],
        out_specs=pl.BlockSpec((1, TM, d), lambda b, j: (b, j, 0)),
        out_shape=jax.ShapeDtypeStruct((bsz, ls, d), F32),
        scratch_shapes=[pltpu.VMEM((TOP_K, TM, d), F32), pltpu.SemaphoreType.DMA(())],
        compiler_params=_cp(("arbitrary", "arbitrary")),
        name="combine",
    )(pos, x, gate, mod, fw, y)


def _route_plan(idx, n_exp):
    shape = idx.shape
    e = idx.reshape(-1)
    onehot = (e[:, None] == jnp.arange(n_exp, dtype=jnp.int32)[None, :]).astype(jnp.int32)
    cs = jnp.cumsum(onehot, axis=0)
    rank = jnp.sum(onehot * cs, axis=1) - 1
    counts = cs[-1]
    padded = ((counts + EXPERT_TILE - 1) // EXPERT_TILE) * EXPERT_TILE
    ends = jnp.cumsum(padded)
    starts = ends - padded
    pos = (starts[e] + rank).reshape(shape).astype(jnp.int32)
    n_rows = e.shape[0] + n_exp * EXPERT_TILE
    n_tiles = n_rows // EXPERT_TILE
    n_used = (ends[-1] // EXPERT_TILE).astype(jnp.int32)
    tile_start = jnp.minimum(jnp.arange(n_tiles, dtype=jnp.int32), n_used - 1) * EXPERT_TILE
    tile_expert = jnp.minimum(jnp.searchsorted(ends, tile_start, side="right"), n_exp - 1).astype(jnp.int32)
    return pos, tile_expert, n_used.reshape(1), n_rows


def _moe(x, mod, nw, rw, rb, wup, bup, wdn, bdn, fw, n_batch, blk_off, final):
    n_exp = rw.shape[1]
    d = x.shape[-1]
    h, idx, gate = _route(x, mod, nw, rw.T, rb.reshape(n_exp, 1), n_batch, blk_off)
    pos, tile_expert, n_used, n_rows = _route_plan(idx, n_exp)
    hs = _dispatch(pos, h, n_rows)
    y = _experts(tile_expert, n_used, hs, wup.astype(BF16), bup.reshape(n_exp, 1, -1),
                 wdn.astype(BF16), bdn.reshape(n_exp, 1, d))
    return _combine(pos, x, jnp.transpose(gate, (0, 2, 1)), mod, fw, y, n_batch, blk_off, final)


def _pad_heads(w, n_heads, width):
    lead = w.shape[:-1]
    dh = w.shape[-1] // n_heads
    w = w.reshape(lead + (n_heads, dh))
    w = jnp.pad(w, [(0, 0)] * len(lead) + [(0, 0), (0, width - dh)])
    return w.reshape(lead + (n_heads * width,))


def _layer0_weights(w_in, w2_f, b2_f, w2_b, b2_b):
    d = w_in.shape[0]
    sizes = (A_HEADS * A_DK, A_HEADS * A_DK, A_HEADS * A_DV, A_RANK, A_RANK, A_HEADS * A_DV,
             B_HEADS * B_DK, B_HEADS * B_DK, B_HEADS * B_DK, B_HEADS * B_DV, B_HEADS * B_DV)
    aq, ak, av, ar_f, ar_b, aog, bq, bz_f, bz_b, bi, bog = jnp.split(w_in, np.cumsum(sizes)[:-1].tolist(), axis=1)
    ar = jnp.concatenate([ar_f, ar_b, jnp.zeros((d, HEAD_W - 2 * A_RANK), w_in.dtype)], axis=1)
    w = jnp.concatenate([_pad_heads(aq, A_HEADS, HEAD_W), bq, av, bi, aog, bog,
                         _pad_heads(ak, A_HEADS, HEAD_W), bz_f, bz_b, ar], axis=1).astype(BF16)
    hw = A_HEADS * HEAD_W
    w2 = jnp.zeros((2, HEAD_W, hw), F32)
    w2 = w2.at[0, 0:A_RANK].set(_pad_heads(w2_f, A_HEADS, HEAD_W))
    w2 = w2.at[1, A_RANK:2 * A_RANK].set(_pad_heads(w2_b, A_HEADS, HEAD_W))
    b2 = jnp.stack([_pad_heads(b2_f, A_HEADS, HEAD_W), _pad_heads(b2_b, A_HEADS, HEAD_W)]).reshape(2, 1, hw)
    return w, w2.astype(BF16), b2


QKV_W = 2 * C_HEADS * C_DK + C_HEADS * C_DV
OG_W = C_HEADS * C_DV
DQ_W = D_HEADS * D_HD
DKV_W = D_KV_HEADS * HEAD_W


def _proj1_kernel(x_ref, mod_ref, nw_ref, w_ref, cos_ref, sin_ref,
                  qkv_ref, gb_ref, og_ref, dq_ref, dk_ref, dv_ref, *, n_batch, d):
    b, j = pl.program_id(0), pl.program_id(1)
    shift, scale = _mod_rows(mod_ref, b, j, n_batch, d, 0, 2)
    h = _norm_mod(x_ref[0], nw_ref[...], shift, scale).astype(BF16)

    def mm(c0, c1):
        return jnp.dot(h, w_ref[:, c0:c1], preferred_element_type=F32)

    c = 0
    qkv_ref[0] = mm(c, c + QKV_W); c += QKV_W
    gb_ref[0] = mm(c, c + HEAD_W); c += HEAD_W
    og_ref[0] = mm(c, c + OG_W); c += OG_W
    cos, sin = cos_ref[...], sin_ref[...]
    for s in range(DQ_W // HEAD_W):
        xs = mm(c + s * HEAD_W, c + (s + 1) * HEAD_W)
        xp = mm(c + DQ_W + s * HEAD_W, c + DQ_W + (s + 1) * HEAD_W)
        dq_ref[0, :, s * HEAD_W:(s + 1) * HEAD_W] = (xs * cos + xp * sin) * (D_HD ** -0.5)
    c += 2 * DQ_W
    for s in range(DKV_W // HEAD_W):
        xs = mm(c + s * HEAD_W, c + (s + 1) * HEAD_W)
        xp = mm(c + DKV_W + s * HEAD_W, c + DKV_W + (s + 1) * HEAD_W)
        dk_ref[0, :, s * HEAD_W:(s + 1) * HEAD_W] = xs * cos + xp * sin
    c += 2 * DKV_W
    dv_ref[0] = mm(c, c + DKV_W)


def _proj1(x, mod, nw, w, cos, sin, n_batch):
    bsz, l, d = x.shape
    widths = (QKV_W, HEAD_W, OG_W, DQ_W, DKV_W, DKV_W)
    return pl.pallas_call(
        functools.partial(_proj1_kernel, n_batch=n_batch, d=d),
        grid=(bsz, l // TM),
        in_specs=[pl.BlockSpec((1, TM, d), lambda b, j: (b, j, 0)),
                  _const_spec(mod.shape), _const_spec(nw.shape), _const_spec(w.shape),
                  pl.BlockSpec((TM, HEAD_W), lambda b, j: (j, 0)),
                  pl.BlockSpec((TM, HEAD_W), lambda b, j: (j, 0))],
        out_specs=[pl.BlockSpec((1, TM, wd), lambda b, j: (b, j, 0)) for wd in widths],
        out_shape=[jax.ShapeDtypeStruct((bsz, l, wd), F32) for wd in widths],
        compiler_params=_cp(("parallel", "arbitrary")),
        name="proj1",
    )(x, mod, nw, w, cos, sin)


HALO = 8


def _conv1_kernel(cur_ref, prev_ref, next_ref, cw_ref, gbr_ref, alog_ref, dtb_ref,
                  q_ref, k_ref, v_ref, gb_ref, xe_ref):
    j, n_blk = pl.program_id(1), pl.num_programs(1)
    prev_ok = j >= 2
    next_ok = jnp.logical_and(j >= 1, j < n_blk - 1)
    xe_ref[0:HALO, :] = jnp.where(prev_ok, prev_ref[0], 0.0)
    xe_ref[HALO:HALO + TM, :] = cur_ref[0]
    xe_ref[HALO + TM:2 * HALO + TM, :] = jnp.where(next_ok, next_ref[0], 0.0)
    pad = CONV_W // 2
    hw = C_HEADS * HEAD_W
    for g, o_ref in enumerate((q_ref, k_ref, v_ref)):
        cols = slice(g * hw, (g + 1) * hw)
        acc = cw_ref[0:1, cols] * xe_ref[HALO - pad:HALO - pad + TM, cols]
        for i in range(1, CONV_W):
            acc = acc + cw_ref[i:i + 1, cols] * xe_ref[HALO - pad + i:HALO - pad + i + TM, cols]
        y = _silu(acc)
        if g < 2:
            parts = []
            for hh in range(C_HEADS):
                yh = y[:, hh * HEAD_W:(hh + 1) * HEAD_W]
                yh = yh * lax.rsqrt(jnp.sum(yh * yh, axis=-1, keepdims=True) + EPS)
                parts.append(yh * (C_DK ** -0.5) if g == 0 else yh)
            y = jnp.concatenate(parts, axis=-1)
        o_ref[0] = y
    raw = gbr_ref[0]
    z = raw + dtb_ref[...]
    softplus = jnp.maximum(z, 0.0) + jnp.log1p(jnp.exp(-jnp.abs(z)))
    lane = lax.broadcasted_iota(jnp.int32, raw.shape, 1)
    gb_ref[0] = jnp.where(lane < 2 * C_HEADS, jax.nn.sigmoid(raw), -jnp.exp(alog_ref[...]) * softplus)


def _conv1(qkv, conv_w, gb_raw, alog, dtb):
    bsz, l, wd = qkv.shape
    n_blk = l // TM
    per = TM // HALO
    last = l // HALO - 1
    hw = C_HEADS * HEAD_W
    ob = pl.BlockSpec((1, TM, hw), lambda b, j: (b, j, 0))
    gbs = pl.BlockSpec((1, TM, HEAD_W), lambda b, j: (b, j, 0))
    return pl.pallas_call(
        _conv1_kernel,
        grid=(bsz, n_blk),
        in_specs=[pl.BlockSpec((1, TM, wd), lambda b, j: (b, j, 0)),
                  pl.BlockSpec((1, HALO, wd), lambda b, j: (b, jnp.maximum(j * per - 1, 0), 0)),
                  pl.BlockSpec((1, HALO, wd), lambda b, j: (b, jnp.minimum((j + 1) * per, last), 0)),
                  _const_spec(conv_w.shape), gbs, _const_spec(alog.shape), _const_spec(dtb.shape)],
        out_specs=[ob, ob, ob, gbs],
        out_shape=[jax.ShapeDtypeStruct((bsz, l, hw), F32)] * 3 + [jax.ShapeDtypeStruct((bsz, l, HEAD_W), F32)],
        scratch_shapes=[pltpu.VMEM((TM + 2 * HALO, wd), F32)],
        compiler_params=_cp(("parallel", "arbitrary")),
        name="conv1",
    )(qkv, qkv, qkv, conv_w, gb_raw, alog, dtb)


_MERGE = (16, 32, 64)
_BASE = 8


def _delta_masks():
    t = np.arange(CHUNK)[:, None]
    s = np.arange(CHUNK)[None, :]
    out = [s <= t, s < t, s >= t, s > t, (t // _BASE) == (s // _BASE)]
    for m2 in _MERGE:
        out.append(((t // m2) == (s // m2)) & ((t // (m2 // 2)) != (s // (m2 // 2))))
    return np.stack(out).astype(np.float32)


def _bdot(a, b):
    return jnp.dot(a.astype(BF16), b.astype(BF16), preferred_element_type=F32)


def _unit_tri_inverse(a, dm):
    eye = dm[0] * dm[2]
    n0 = -(a * dm[4])
    n2 = _bdot(n0, n0)
    n4 = _bdot(n2, n2)
    t = eye + n0
    t = t + _bdot(t, n2)
    t = t + _bdot(t, n4)
    for li in range(len(_MERGE)):
        t = t - _bdot(t, _bdot(a * dm[5 + li], t))
    return t


def _delta_chunk(q, k, v, g, beta, st, dm, reverse):
    incl, strict = (dm[2], dm[3]) if reverse else (dm[0], dm[1])
    cum = _cumsum_rows(jnp.broadcast_to(g, q.shape), reverse)
    cum_s = cum.T[0:CHUNK, :]
    decay = incl * jnp.exp(jnp.where(incl > 0, cum[:, 0:CHUNK] - cum_s, 0.0))
    kb = k * beta
    a = strict * _dot_nt(kb.astype(BF16), k.astype(BF16)) * decay
    t_inv = _unit_tri_inverse(a, dm)
    uw = _bdot(t_inv, jnp.concatenate([v * beta, kb * jnp.exp(cum)], axis=1))
    v_new = uw[:, 0:HEAD_W] - _bdot(uw[:, HEAD_W:], st)
    scores = _dot_nt(q.astype(BF16), k.astype(BF16)) * decay
    o = _bdot(q * jnp.exp(cum), st) + _bdot(scores, v_new)
    edge = cum[0:1] if reverse else cum[CHUNK - 1:CHUNK]
    st_new = st * jnp.exp(edge) + _dot_tn((k * jnp.exp(edge - cum)).astype(BF16), v_new.astype(BF16))
    return o, st_new


def _lane_col(x, lane_idx):
    lane = lax.broadcasted_iota(jnp.int32, x.shape, 1)
    return jnp.sum(jnp.where(lane == lane_idx, x, 0.0), axis=-1, keepdims=True)


def _delta_kernel(dm_ref, qf_ref, kf_ref, vf_ref, gbf_ref, qb_ref, kb_ref, vb_ref, gbb_ref,
                  of_ref, ob_ref, sf_ref, sb_ref):
    h = pl.program_id(1)

    @pl.when(pl.program_id(2) == 0)
    def _():
        sf_ref[...] = jnp.zeros_like(sf_ref)
        sb_ref[...] = jnp.zeros_like(sb_ref)

    dm = [dm_ref[i] for i in range(dm_ref.shape[0])]
    beta_f, g_f = _lane_col(gbf_ref[0], h), _lane_col(gbf_ref[0], 2 * C_HEADS + h)
    beta_b, g_b = _lane_col(gbb_ref[0], C_HEADS + h), _lane_col(gbb_ref[0], 3 * C_HEADS + h)
    sf, sb = sf_ref[...], sb_ref[...]
    n_ch = TM // CHUNK
    for c in range(n_ch):
        rf = slice(c * CHUNK, (c + 1) * CHUNK)
        rb = slice((n_ch - 1 - c) * CHUNK, (n_ch - c) * CHUNK)
        o, sf = _delta_chunk(qf_ref[0, rf, :], kf_ref[0, rf, :], vf_ref[0, rf, :], g_f[rf], beta_f[rf], sf, dm, False)
        of_ref[0, rf, :] = o
        o, sb = _delta_chunk(qb_ref[0, rb, :], kb_ref[0, rb, :], vb_ref[0, rb, :], g_b[rb], beta_b[rb], sb, dm, True)
        ob_ref[0, rb, :] = o
    sf_ref[...] = sf
    sb_ref[...] = sb


def _delta(dmasks, q, k, v, gb):
    bsz, l, wide = q.shape
    n_blk = l // TM
    fwd = pl.BlockSpec((1, TM, HEAD_W), lambda b, h, i: (b, i, h))
    bwd = pl.BlockSpec((1, TM, HEAD_W), lambda b, h, i: (b, _bwd_block(i, n_blk), h))
    gf = pl.BlockSpec((1, TM, HEAD_W), lambda b, h, i: (b, i, 0))
    gbw = pl.BlockSpec((1, TM, HEAD_W), lambda b, h, i: (b, _bwd_block(i, n_blk), 0))
    return pl.pallas_call(
        _delta_kernel,
        grid=(bsz, wide // HEAD_W, n_blk),
        in_specs=[_const_spec(dmasks.shape), fwd, fwd, fwd, gf, bwd, bwd, bwd, gbw],
        out_specs=[fwd, bwd],
        out_shape=[jax.ShapeDtypeStruct((bsz, l, wide), F32)] * 2,
        scratch_shapes=[pltpu.VMEM((HEAD_W, HEAD_W), F32), pltpu.VMEM((HEAD_W, HEAD_W), F32)],
        compiler_params=_cp(("parallel", "parallel", "arbitrary")),
        name="delta",
    )(dmasks, q, k, v, gb, q, k, v, gb)


def _attn_kernel(q_ref, kp_ref, kc_ref, kn_ref, vp_ref, vc_ref, vn_ref, kx_ref, vx_ref, sink_ref, o_ref, *, t_len):
    i = pl.program_id(1)
    q = q_ref[0]
    kl = jnp.concatenate([kp_ref[0], kc_ref[0], kn_ref[0]], axis=0).astype(BF16)
    vl = jnp.concatenate([vp_ref[0], vc_ref[0], vn_ref[0]], axis=0).astype(BF16)
    kx, vx = kx_ref[0].astype(BF16), vx_ref[0].astype(BF16)
    qpos = lax.broadcasted_iota(jnp.int32, (WINDOW, 3 * WINDOW), 0)
    kpos = lax.broadcasted_iota(jnp.int32, (WINDOW, 3 * WINDOW), 1) - WINDOW
    k_abs = i * WINDOW + kpos
    valid = (jnp.abs(kpos - qpos) <= WINDOW) & (k_abs >= 0) & (k_abs < t_len)
    low = lax.broadcasted_iota(jnp.int32, (WINDOW, HEAD_W), 1) < D_HD
    group = D_HEADS // D_KV_HEADS
    for p in range(D_HEADS // 2):
        g = (2 * p) // group
        cols = slice(g * HEAD_W, (g + 1) * HEAD_W)
        qp = q[:, p * HEAD_W:(p + 1) * HEAD_W]
        outs = []
        for half in range(2):
            qm = jnp.where(low if half == 0 else jnp.logical_not(low), qp, 0.0).astype(BF16)
            s_l = jnp.where(valid, _dot_nt(qm, kl[:, cols]), -jnp.inf)
            s_x = _dot_nt(qm, kx[:, cols])
            sink = sink_ref[:, 2 * p + half:2 * p + half + 1]
            m = jnp.maximum(jnp.maximum(jnp.max(s_l, axis=-1, keepdims=True), jnp.max(s_x, axis=-1, keepdims=True)), sink)
            p_l, p_x = jnp.exp(s_l - m), jnp.exp(s_x - m)
            den = jnp.sum(p_l, axis=-1, keepdims=True) + jnp.sum(p_x, axis=-1, keepdims=True) + jnp.exp(sink - m)
            o = (jnp.dot(p_l.astype(BF16), vl[:, cols], preferred_element_type=F32)
                 + jnp.dot(p_x.astype(BF16), vx[:, cols], preferred_element_type=F32))
            outs.append(o / den)
        o_ref[0, :, p * HEAD_W:(p + 1) * HEAD_W] = jnp.where(low, outs[0], outs[1])


def _attn(dq, dk, dv, sinks, n_ctx):
    bsz, l, _ = dq.shape
    t_len = l - n_ctx
    nq = t_len // WINDOW
    off = n_ctx // WINDOW

    def kv(delta):
        return pl.BlockSpec((1, WINDOW, DKV_W), lambda b, i: (b, off + jnp.clip(i + delta, 0, nq - 1), 0))

    ctx = pl.BlockSpec((1, n_ctx, DKV_W), lambda b, i: (b, 0, 0))
    return pl.pallas_call(
        functools.partial(_attn_kernel, t_len=t_len),
        grid=(bsz, nq),
        in_specs=[pl.BlockSpec((1, WINDOW, DQ_W), lambda b, i: (b, off + i, 0)),
                  kv(-1), kv(0), kv(1), kv(-1), kv(0), kv(1), ctx, ctx, _const_spec(sinks.shape)],
        out_specs=pl.BlockSpec((1, WINDOW, DQ_W), lambda b, i: (b, i, 0)),
        out_shape=jax.ShapeDtypeStruct((bsz, t_len, DQ_W), F32),
        compiler_params=_cp(("parallel", "arbitrary")),
        name="attn",
    )(dq, dk, dk, dk, dv, dv, dv, dk, dv, sinks)


def _read1_kernel(x_ref, of_ref, ob_ref, og_ref, od_ref, nw_ref, wout_ref, mod_ref, xo_ref, *, n_batch, d):
    b, j = pl.program_id(0), pl.program_id(1) + 1
    (g1,) = _mod_rows(mod_ref, b, j, n_batch, d, 2, 1)
    o = of_ref[0] + ob_ref[0]
    parts = [_head_rms(o[:, hh * HEAD_W:(hh + 1) * HEAD_W], nw_ref[...]) for hh in range(C_HEADS)]
    y = jnp.concatenate(parts, axis=-1) * _silu(og_ref[0])
    y = jnp.concatenate([y, od_ref[0]], axis=-1)
    xo_ref[0] = x_ref[0] + g1 * jnp.dot(y.astype(BF16), wout_ref[...], preferred_element_type=F32)


def _read1(x, of, ob, og, od, nw, wout, mod, n_batch):
    bsz, l, d = x.shape
    lat = lambda wd: pl.BlockSpec((1, TM, wd), lambda b, j: (b, j + 1, 0))
    return pl.pallas_call(
        functools.partial(_read1_kernel, n_batch=n_batch, d=d),
        grid=(bsz, l // TM - 1),
        in_specs=[lat(d), lat(OG_W), lat(OG_W), lat(OG_W),
                  pl.BlockSpec((1, TM, DQ_W), lambda b, j: (b, j, 0)),
                  _const_spec(nw.shape), _const_spec(wout.shape), _const_spec(mod.shape)],
        out_specs=lat(d),
        out_shape=jax.ShapeDtypeStruct(x.shape, F32),
        input_output_aliases={0: 0},
        compiler_params=_cp(("parallel", "arbitrary")),
        name="read1",
    )(x, of, ob, og, od, nw, wout, mod)


def _swap_halves(w, n_heads):
    lead = w.shape[:-1]
    dh = w.shape[-1] // n_heads
    w = w.reshape(lead + (n_heads, 2, dh // 2))
    return w[..., ::-1, :].reshape(lead + (n_heads * dh,))


def _dup_heads(w, n_heads):
    lead = w.shape[:-1]
    dh = w.shape[-1] // n_heads
    w = w.reshape(lead + (n_heads, 1, dh))
    return jnp.concatenate([w, w], axis=-2).reshape(lead + (2 * n_heads * dh,))


def _layer1_weights(w_in):
    d = w_in.shape[0]
    sizes = (C_HEADS * C_DK, C_HEADS * C_DK, C_HEADS * C_DV, C_HEADS, C_HEADS, C_HEADS, C_HEADS,
             C_HEADS * C_DV, D_HEADS * D_HD, D_KV_HEADS * D_HD, D_KV_HEADS * D_HD)
    cq, ck, cv, bt_f, bt_b, a_f, a_b, og, dq, dk, dv = jnp.split(w_in, np.cumsum(sizes)[:-1].tolist(), axis=1)
    gates = jnp.concatenate([bt_f, bt_b, a_f, a_b, jnp.zeros((d, HEAD_W - 4 * C_HEADS), w_in.dtype)], axis=1)
    return jnp.concatenate([cq, ck, cv, gates, og, dq, _swap_halves(dq, D_HEADS),
                            _dup_heads(dk, D_KV_HEADS), _dup_heads(_swap_halves(dk, D_KV_HEADS), D_KV_HEADS),
                            _dup_heads(dv, D_KV_HEADS)], axis=1).astype(BF16)


def _rope_tables(n_ctx, t_len):
    rows = t_len // GRID_W
    row = jnp.repeat(jnp.arange(rows, dtype=F32), GRID_W)
    col = jnp.tile(jnp.arange(GRID_W, dtype=F32), rows)
    n_freq = D_HD // 4
    inv = ROPE_BASE ** (-jnp.arange(n_freq, dtype=F32) / n_freq)
    ang = jnp.concatenate([row[:, None] * inv, col[:, None] * inv], axis=-1)
    cos, sin = jnp.cos(ang), jnp.sin(ang)
    cos = jnp.concatenate([jnp.ones((n_ctx, D_HD // 2), F32), cos], axis=0)
    sin = jnp.concatenate([jnp.zeros((n_ctx, D_HD // 2), F32), sin], axis=0)
    return jnp.concatenate([cos] * 4, axis=1), jnp.concatenate([-sin, sin] * 2, axis=1)


def _mixer1(xx, mod, nw, w_in, w_out, conv_w, a_log_f, dt_bias_f, a_log_b, dt_bias_b, dn_norm_w, sinks, n_batch, n_ctx):
    l = xx.shape[1]
    cos, sin = _rope_tables(n_ctx, l - n_ctx)
    qkv, gb_raw, og, dq, dk, dv = _proj1(xx, mod, nw, _layer1_weights(w_in), cos, sin, n_batch)
    zero4 = jnp.zeros((C_HEADS,), F32)
    fill = jnp.zeros((HEAD_W - 4 * C_HEADS,), F32)
    alog = jnp.concatenate([zero4, zero4, a_log_f, a_log_b, fill]).reshape(1, HEAD_W)
    dtb = jnp.concatenate([zero4, zero4, dt_bias_f, dt_bias_b, fill]).reshape(1, HEAD_W)
    cq, ck, cv, gb = _conv1(qkv, conv_w, gb_raw, alog, dtb)
    of, ob = _delta(jnp.asarray(_delta_masks()), cq, ck, cv, gb)
    sink_row = jnp.concatenate([sinks, jnp.zeros((HEAD_W - D_HEADS,), F32)]).reshape(1, HEAD_W)
    od = _attn(dq, dk, dv, sink_row, n_ctx)
    return _read1(xx, of, ob, og, od, dn_norm_w.reshape(1, -1), w_out.astype(BF16), mod, n_batch)


def kernel(x, c, ctx, c_ctx, l0_ada_w, l0_ada_b, l0_norm_mix_w, l0_w_in, l0_w_out, l0_gla_w2_f, l0_gla_b_f, l0_gla_w2_b, l0_gla_b_b, l0_gla_norm_w, l0_hgrn_norm_w, hgrn_lb_logits, l0_norm_ffn_w, l0_router_w, l0_router_b, l0_w_up, l0_b_up, l0_w_down, l0_b_down, l1_ada_w, l1_ada_b, l1_norm_mix_w, l1_w_in, l1_w_out, l1_conv_w, l1_a_log_f, l1_dt_bias_f, l1_a_log_b, l1_dt_bias_b, l1_dn_norm_w, l1_sinks, l1_norm_ffn_w, l1_router_w, l1_router_b, l1_w_up, l1_b_up, l1_w_down, l1_b_down, final_norm_w):
    bsz, t, d = x.shape
    n_ctx = ctx.shape[1]
    assert n_ctx == TM and t % TM == 0
    xx = jnp.concatenate([ctx, x], axis=1)
    mod_rows = -(-(bsz + 1) // 8) * 8
    c_all = jnp.zeros((mod_rows, d), F32).at[:bsz].set(c).at[bsz].set(c_ctx)
    ones = jnp.ones((1, d), F32)

    mod0 = _ada_table(c_all, l0_ada_w, l0_ada_b)
    w0, w2, b2 = _layer0_weights(l0_w_in, l0_gla_w2_f, l0_gla_b_f, l0_gla_w2_b, l0_gla_b_b)
    q, kf, kb, v, lff, lfb, og = _proj0(xx, mod0, l0_norm_mix_w.reshape(1, d), w0, w2, b2, hgrn_lb_logits, bsz)
    of, ob = _scan0(jnp.asarray(_level_masks()), q, kf, kb, v, lff, lfb)
    xx = _read0(xx, of, ob, og, l0_gla_norm_w.reshape(1, -1), l0_hgrn_norm_w.reshape(1, -1), l0_w_out.astype(BF16), mod0, bsz)
    xx = _moe(xx, mod0, l0_norm_ffn_w.reshape(1, d), l0_router_w, l0_router_b, l0_w_up, l0_b_up, l0_w_down, l0_b_down,
              ones, bsz, 0, False)

    mod1 = _ada_table(c_all, l1_ada_w, l1_ada_b)
    xx = _mixer1(xx, mod1, l1_norm_mix_w.reshape(1, d), l1_w_in, l1_w_out, l1_conv_w, l1_a_log_f, l1_dt_bias_f,
                 l1_a_log_b, l1_dt_bias_b, l1_dn_norm_w, l1_sinks, bsz, n_ctx)
    return _moe(xx, mod1, l1_norm_ffn_w.reshape(1, d), l1_router_w, l1_router_b, l1_w_up, l1_b_up, l1_w_down, l1_b_down,
                final_norm_w.reshape(1, d), bsz, 1, True)
```

```python
import functools

import numpy as np
import jax
import jax.numpy as jnp
from jax import lax
from jax.experimental import pallas as pl
from jax.experimental.pallas import tpu as pltpu

F32 = jnp.float32
BF16 = jnp.bfloat16
EPS = 1e-6

CHUNK = 64
A_HEADS, A_DK, A_DV, A_RANK = 4, 64, 128, 16
GATE_TAU = 16.0
B_HEADS, B_DK, B_DV = 4, 128, 128
C_HEADS, C_DK, C_DV = 4, 128, 128
CONV_W = 5
D_HEADS, D_KV_HEADS, D_HD = 8, 2, 64
WINDOW = 128
GRID_W = 64
ROPE_BASE = 10000.0
TOP_K = 4
SWIGLU_LIMIT = 7.0
SWIGLU_ALPHA = 1.702

TM = 256
HEAD_W = 128
EXPERT_TILE = 512
FF_CHUNK = 512
VMEM_LIMIT = 48 * 1024 * 1024


def _cp(sem, vmem=VMEM_LIMIT):
    return pltpu.CompilerParams(dimension_semantics=sem, vmem_limit_bytes=vmem)


def _const_spec(shape):
    nd = len(shape)
    return pl.BlockSpec(shape, lambda *_: (0,) * nd)


def _silu(x):
    return x * jax.nn.sigmoid(x)


def _log_sigmoid(z):
    return jnp.minimum(z, 0.0) - jnp.log1p(jnp.exp(-jnp.abs(z)))


def _norm_mod(x, nw, shift, scale):
    y = x * lax.rsqrt(jnp.mean(x * x, axis=-1, keepdims=True) + EPS) * nw
    return y * (1.0 + scale) + shift


def _mod_rows(mod_ref, batch, blk, n_batch, d, first, count):
    row = jnp.where(blk == 0, n_batch, batch)
    return [mod_ref[pl.ds(row, 1), (first + i) * d:(first + i + 1) * d] for i in range(count)]


def _ada_kernel(c_ref, w_ref, b_ref, o_ref):
    s = _silu(c_ref[...])
    o_ref[...] = jnp.dot(s, w_ref[...], precision=lax.Precision.HIGHEST,
                         preferred_element_type=F32) + b_ref[...]


def _ada_table(c_all, w, b):
    rows, d = c_all.shape
    n = w.shape[1]
    bn = d
    return pl.pallas_call(
        _ada_kernel,
        grid=(n // bn,),
        in_specs=[pl.BlockSpec((rows, d), lambda j: (0, 0)),
                  pl.BlockSpec((d, bn), lambda j: (0, j)),
                  pl.BlockSpec((1, bn), lambda j: (0, j))],
        out_specs=pl.BlockSpec((rows, bn), lambda j: (0, j)),
        out_shape=jax.ShapeDtypeStruct((rows, n), F32),
        compiler_params=_cp(("arbitrary",)),
        name="ada_table",
    )(c_all, w, b.reshape(1, n))


def _proj0_kernel(x_ref, mod_ref, nw_ref, w_ref, w2_ref, b2_ref, lbl_ref,
                  q_ref, kf_ref, kb_ref, v_ref, lff_ref, lfb_ref, og_ref, *, n_batch, d):
    b, j = pl.program_id(0), pl.program_id(1)
    shift, scale = _mod_rows(mod_ref, b, j, n_batch, d, 0, 2)
    h = _norm_mod(x_ref[0], nw_ref[...], shift, scale).astype(BF16)

    def mm(c0, c1):
        return jnp.dot(h, w_ref[:, c0:c1], preferred_element_type=F32)

    hw = A_HEADS * HEAD_W
    q_ref[0, :, 0:hw] = mm(0, hw) * (A_DK ** -0.5)
    q_ref[0, :, hw:2 * hw] = mm(hw, 2 * hw)
    v_ref[0] = mm(2 * hw, 4 * hw)
    og_ref[0] = mm(4 * hw, 6 * hw)
    kg = mm(6 * hw, 7 * hw)
    kf_ref[0, :, 0:hw] = kg
    kb_ref[0, :, 0:hw] = kg
    lg = lbl_ref[...]
    e = jnp.exp(lg - jnp.max(lg, axis=0, keepdims=True))
    lb = e[0:1] / jnp.sum(e, axis=0, keepdims=True)
    log_lb, log_1m = jnp.log(lb), jnp.log1p(-lb)
    ar = mm(9 * hw, 9 * hw + HEAD_W).astype(BF16)
    for di, (k_ref, lf_ref) in enumerate(((kf_ref, lff_ref), (kb_ref, lfb_ref))):
        z = mm((7 + di) * hw, (8 + di) * hw)
        s1 = log_1m + _log_sigmoid(z)
        lf_ref[0, :, hw:2 * hw] = jnp.maximum(log_lb, s1) + jnp.log1p(jnp.exp(-jnp.abs(log_lb - s1)))
        k_ref[0, :, hw:2 * hw] = (1.0 - lb) * jax.nn.sigmoid(-z)
        za = jnp.dot(ar, w2_ref[di], preferred_element_type=F32) + b2_ref[di]
        lf_ref[0, :, 0:hw] = _log_sigmoid(za) * (1.0 / GATE_TAU)


def _proj0(x, mod, nw, w, w2, b2, lbl, n_batch):
    bsz, l, d = x.shape
    n_blk = l // TM
    wide = 2 * A_HEADS * HEAD_W
    blk = pl.BlockSpec((1, TM, wide), lambda b, j: (b, j, 0))
    return pl.pallas_call(
        functools.partial(_proj0_kernel, n_batch=n_batch, d=d),
        grid=(bsz, n_blk),
        in_specs=[pl.BlockSpec((1, TM, d), lambda b, j: (b, j, 0)),
                  _const_spec(mod.shape), _const_spec(nw.shape), _const_spec(w.shape),
                  _const_spec(w2.shape), _const_spec(b2.shape), _const_spec(lbl.shape)],
        out_specs=[blk] * 7,
        out_shape=[jax.ShapeDtypeStruct((bsz, l, wide), F32)] * 7,
        compiler_params=_cp(("parallel", "arbitrary")),
        name="proj0",
    )(x, mod, nw, w, w2, b2, lbl)


_LEVELS = (32, 16, 8, 4, 2, 1)


def _level_masks():
    t = np.arange(CHUNK)[:, None]
    s = np.arange(CHUNK)[None, :]
    out = np.zeros((2, len(_LEVELS) + 1, CHUNK, CHUNK), np.float32)
    for li, m in enumerate(_LEVELS):
        same = (t // (2 * m)) == (s // (2 * m))
        fwd = same & (t % (2 * m) >= m) & (s % (2 * m) < m)
        out[0, li] = fwd
        out[1, li] = fwd.T
    out[:, -1] = np.eye(CHUNK)
    return out


def _cumsum_rows(x, reverse):
    n = x.shape[0]
    r = lax.broadcasted_iota(jnp.int32, x.shape, 0)
    sh = 1
    while sh < n:
        if reverse:
            x = x + jnp.where(r < n - sh, pltpu.roll(x, n - sh, 0), 0.0)
        else:
            x = x + jnp.where(r >= sh, pltpu.roll(x, sh, 0), 0.0)
        sh *= 2
    return x


def _level_ref(cum, m, reverse):
    n = cum.shape[0]
    tgt = m if reverse else m - 1
    if 2 * m >= 8:
        parts = [jnp.broadcast_to(cum[g + tgt:g + tgt + 1, :], (2 * m, cum.shape[1]))
                 for g in range(0, n, 2 * m)]
        return parts[0] if len(parts) == 1 else jnp.concatenate(parts, axis=0)
    pos = lax.broadcasted_iota(jnp.int32, cum.shape, 0) % (2 * m)
    out = cum
    for p in range(2 * m):
        if p == tgt:
            continue
        shift = (p - tgt) % n
        out = jnp.where(pos == p, pltpu.roll(cum, shift, 0), out)
    return out


def _dot_nt(a, b):
    return lax.dot_general(a, b, (((1,), (1,)), ((), ())), preferred_element_type=F32)


def _dot_tn(a, b):
    return lax.dot_general(a, b, (((0,), (0,)), ((), ())), preferred_element_type=F32)


def _gated_chunk(q, k, v, lf, st, masks, reverse):
    cum = _cumsum_rows(lf, reverse)
    scores = masks[len(_LEVELS)] * _dot_nt(q.astype(BF16), k.astype(BF16))
    for li, m in enumerate(_LEVELS):
        w = jnp.exp(-jnp.abs(cum - _level_ref(cum, m, reverse)))
        scores = scores + masks[li] * _dot_nt((q * w).astype(BF16), (k * w).astype(BF16))
    o = jnp.dot(scores.astype(BF16), v.astype(BF16), preferred_element_type=F32)
    o = o + _dot_nt((q * jnp.exp(cum)).astype(BF16), st.astype(BF16))
    edge = cum[0:1] if reverse else cum[CHUNK - 1:CHUNK]
    kd = k * jnp.exp(edge - cum)
    st_new = st * jnp.exp(edge) + _dot_tn(v.astype(BF16), kd.astype(BF16))
    return o, st_new


def _scan0_kernel(m_ref, qf_ref, kf_ref, vf_ref, lff_ref, qb_ref, kb_ref, vb_ref, lfb_ref,
                  of_ref, ob_ref, stf_ref, stb_ref):
    @pl.when(pl.program_id(2) == 0)
    def _():
        stf_ref[...] = jnp.zeros_like(stf_ref)
        stb_ref[...] = jnp.zeros_like(stb_ref)

    mf = [m_ref[0, i] for i in range(len(_LEVELS) + 1)]
    mb = [m_ref[1, i] for i in range(len(_LEVELS) + 1)]
    stf, stb = stf_ref[...], stb_ref[...]
    n_ch = TM // CHUNK
    for c in range(n_ch):
        rf = slice(c * CHUNK, (c + 1) * CHUNK)
        rb = slice((n_ch - 1 - c) * CHUNK, (n_ch - c) * CHUNK)
        o, stf = _gated_chunk(qf_ref[0, rf, :], kf_ref[0, rf, :], vf_ref[0, rf, :], lff_ref[0, rf, :], stf, mf, False)
        of_ref[0, rf, :] = o
        o, stb = _gated_chunk(qb_ref[0, rb, :], kb_ref[0, rb, :], vb_ref[0, rb, :], lfb_ref[0, rb, :], stb, mb, True)
        ob_ref[0, rb, :] = o
    stf_ref[...] = stf
    stb_ref[...] = stb


def _bwd_block(i, n_blk):
    return jnp.where(i == 0, 0, n_blk - i)


def _scan0(masks, q, kf, kb, v, lff, lfb):
    bsz, l, wide = q.shape
    n_blk = l // TM
    n_heads = wide // HEAD_W
    fwd = pl.BlockSpec((1, TM, HEAD_W), lambda b, h, i: (b, i, h))
    bwd = pl.BlockSpec((1, TM, HEAD_W), lambda b, h, i: (b, _bwd_block(i, n_blk), h))
    return pl.pallas_call(
        _scan0_kernel,
        grid=(bsz, n_heads, n_blk),
        in_specs=[_const_spec(masks.shape), fwd, fwd, fwd, fwd, bwd, bwd, bwd, bwd],
        out_specs=[fwd, bwd],
        out_shape=[jax.ShapeDtypeStruct((bsz, l, wide), F32)] * 2,
        scratch_shapes=[pltpu.VMEM((HEAD_W, HEAD_W), F32), pltpu.VMEM((HEAD_W, HEAD_W), F32)],
        compiler_params=_cp(("parallel", "parallel", "arbitrary")),
        name="scan0",
    )(masks, q, kf, v, lff, q, kb, v, lfb)


def _head_rms(o, w):
    return o * lax.rsqrt(jnp.mean(o * o, axis=-1, keepdims=True) + EPS) * w


def _read0_kernel(x_ref, of_ref, ob_ref, og_ref, nwa_ref, nwb_ref, wout_ref, mod_ref, xo_ref, *, n_batch, d):
    b, j = pl.program_id(0), pl.program_id(1)
    (g1,) = _mod_rows(mod_ref, b, j, n_batch, d, 2, 1)
    o = of_ref[0] + ob_ref[0]
    parts = []
    for hh in range(A_HEADS + B_HEADS):
        nw = nwa_ref[...] if hh < A_HEADS else nwb_ref[...]
        parts.append(_head_rms(o[:, hh * HEAD_W:(hh + 1) * HEAD_W], nw))
    y = jnp.concatenate(parts, axis=-1) * _silu(og_ref[0])
    yo = jnp.dot(y.astype(BF16), wout_ref[...], preferred_element_type=F32)
    xo_ref[0] = x_ref[0] + g1 * yo


def _read0(x, of, ob, og, nwa, nwb, wout, mod, n_batch):
    bsz, l, d = x.shape
    wide = of.shape[-1]
    xb = pl.BlockSpec((1, TM, d), lambda b, j: (b, j, 0))
    wb = pl.BlockSpec((1, TM, wide), lambda b, j: (b, j, 0))
    return pl.pallas_call(
        functools.partial(_read0_kernel, n_batch=n_batch, d=d),
        grid=(bsz, l // TM),
        in_specs=[xb, wb, wb, wb, _const_spec(nwa.shape), _const_spec(nwb.shape),
                  _const_spec(wout.shape), _const_spec(mod.shape)],
        out_specs=xb,
        out_shape=jax.ShapeDtypeStruct(x.shape, F32),
        compiler_params=_cp(("parallel", "arbitrary")),
        name="read0",
    )(x, of, ob, og, nwa, nwb, wout, mod)


def _route_kernel(x_ref, mod_ref, nw_ref, rwt_ref, rb_ref, h_ref, idx_ref, gate_ref, *, n_batch, d, blk_off):
    b, j = pl.program_id(0), pl.program_id(1) + blk_off
    shift, scale = _mod_rows(mod_ref, b, j, n_batch, d, 3, 2)
    h = _norm_mod(x_ref[0], nw_ref[...], shift, scale)
    h_ref[0] = h
    logits = lax.dot_general(rwt_ref[...], h, (((1,), (1,)), ((), ())), precision=lax.Precision.HIGHEST,
                             preferred_element_type=F32) + rb_ref[...]
    n_exp = logits.shape[0]
    rows = lax.broadcasted_iota(jnp.int32, logits.shape, 0)
    vals, idxs = [], []
    for _ in range(TOP_K):
        m = jnp.max(logits, axis=0, keepdims=True)
        i = jnp.min(jnp.where(logits == m, rows, n_exp), axis=0, keepdims=True)
        vals.append(m)
        idxs.append(i)
        logits = jnp.where(rows == i, -jnp.inf, logits)
    ex = [jnp.exp(v - vals[0]) for v in vals]
    tot = ex[0] + ex[1] + ex[2] + ex[3]
    idx_ref[0] = jnp.concatenate(idxs, axis=0)
    gate_ref[0] = jnp.concatenate([e / tot for e in ex], axis=0)


def _route(x, mod, nw, rwt, rb, n_batch, blk_off):
    bsz, l, d = x.shape
    n_blk = l // TM - blk_off
    ls = n_blk * TM
    n_exp = rwt.shape[0]
    return pl.pallas_call(
        functools.partial(_route_kernel, n_batch=n_batch, d=d, blk_off=blk_off),
        grid=(bsz, n_blk),
        in_specs=[pl.BlockSpec((1, TM, d), lambda b, j: (b, j + blk_off, 0)),
                  _const_spec(mod.shape), _const_spec(nw.shape), _const_spec(rwt.shape), _const_spec(rb.shape)],
        out_specs=[pl.BlockSpec((1, TM, d), lambda b, j: (b, j, 0)),
                   pl.BlockSpec((1, TOP_K, TM), lambda b, j: (b, 0, j)),
                   pl.BlockSpec((1, TOP_K, TM), lambda b, j: (b, 0, j))],
        out_shape=[jax.ShapeDtypeStruct((bsz, ls, d), F32),
                   jax.ShapeDtypeStruct((bsz, TOP_K, ls), jnp.int32),
                   jax.ShapeDtypeStruct((bsz, TOP_K, ls), F32)],
        compiler_params=_cp(("parallel", "arbitrary")),
        name="route",
    )(x, mod, nw, rwt, rb)


def _row_copy(src, dst, sem):
    return pltpu.make_async_copy(src, dst, sem)


def _dispatch_kernel(pos_ref, h_ref, hs_in_ref, hs_ref, sem):
    del hs_in_ref

    def issue(r, carry):
        for k in range(TOP_K):
            _row_copy(h_ref.at[0, pl.ds(r, 1), :], hs_ref.at[pl.ds(pos_ref[0, k, r], 1), :], sem).start()
        return carry

    lax.fori_loop(0, TM, issue, 0)

    def drain(r, carry):
        for k in range(TOP_K):
            _row_copy(h_ref.at[0, pl.ds(r, 1), :], hs_ref.at[pl.ds(pos_ref[0, k, r], 1), :], sem).wait()
        return carry

    lax.fori_loop(0, TM, drain, 0)


def _dispatch(pos, h, n_rows):
    bsz, ls, d = h.shape
    zeros = jnp.zeros((n_rows, d), F32)
    return pl.pallas_call(
        _dispatch_kernel,
        grid=(bsz, ls // TM),
        in_specs=[pl.BlockSpec((1, TOP_K, TM), lambda b, j: (b, 0, j), memory_space=pltpu.SMEM),
                  pl.BlockSpec((1, TM, d), lambda b, j: (b, j, 0)),
                  pl.BlockSpec(memory_space=pl.ANY)],
        out_specs=pl.BlockSpec(memory_space=pl.ANY),
        out_shape=jax.ShapeDtypeStruct((n_rows, d), F32),
        scratch_shapes=[pltpu.SemaphoreType.DMA(())],
        input_output_aliases={2: 0},
        compiler_params=_cp(("arbitrary", "arbitrary")),
        name="dispatch",
    )(pos, h, zeros)


def _expert_kernel(te_ref, nu_ref, hs_ref, wup_ref, bup_ref, wdn_ref, bdn_ref, y_ref):
    del te_ref
    i = pl.program_id(0)

    @pl.when(i < nu_ref[0])
    def _():
        h = hs_ref[...].astype(BF16)
        ff = wdn_ref.shape[1]
        y = jnp.zeros(y_ref.shape, F32) + bdn_ref[0]
        fc = min(FF_CHUNK, ff)
        for c in range(0, ff, fc):
            glu = jnp.dot(h, wup_ref[0, :, c:c + fc], preferred_element_type=F32) + bup_ref[0, :, c:c + fc]
            lin = (jnp.dot(h, wup_ref[0, :, ff + c:ff + c + fc], preferred_element_type=F32)
                   + bup_ref[0, :, ff + c:ff + c + fc])
            glu = jnp.minimum(glu, SWIGLU_LIMIT)
            lin = jnp.clip(lin, -SWIGLU_LIMIT, SWIGLU_LIMIT)
            act = glu * jax.nn.sigmoid(SWIGLU_ALPHA * glu) * (lin + 1.0)
            y = y + jnp.dot(act.astype(BF16), wdn_ref[0, c:c + fc, :], preferred_element_type=F32)
        y_ref[...] = y

    @pl.when(i >= nu_ref[0])
    def _():
        y_ref[...] = jnp.zeros_like(y_ref)


def _experts(tile_expert, n_used, hs, wup, bup, wdn, bdn):
    n_rows, d = hs.shape
    n_tiles = n_rows // EXPERT_TILE
    ff = wdn.shape[1]
    grid_spec = pltpu.PrefetchScalarGridSpec(
        num_scalar_prefetch=2,
        grid=(n_tiles,),
        in_specs=[pl.BlockSpec((EXPERT_TILE, d), lambda i, te, nu: (i, 0)),
                  pl.BlockSpec((1, d, 2 * ff), lambda i, te, nu: (te[i], 0, 0)),
                  pl.BlockSpec((1, 1, 2 * ff), lambda i, te, nu: (te[i], 0, 0)),
                  pl.BlockSpec((1, ff, d), lambda i, te, nu: (te[i], 0, 0)),
                  pl.BlockSpec((1, 1, d), lambda i, te, nu: (te[i], 0, 0))],
        out_specs=pl.BlockSpec((EXPERT_TILE, d), lambda i, te, nu: (i, 0)),
    )
    return pl.pallas_call(
        _expert_kernel,
        grid_spec=grid_spec,
        out_shape=jax.ShapeDtypeStruct((n_rows, d), F32),
        compiler_params=_cp(("arbitrary",)),
        name="experts",
    )(tile_expert, n_used, hs, wup, bup, wdn, bdn)


def _combine_kernel(pos_ref, x_ref, gate_ref, mod_ref, fw_ref, y_hbm, xo_ref, ybuf, sem, *, n_batch, d, blk_off, final):
    b, j = pl.program_id(0), pl.program_id(1) + blk_off

    def issue(r, carry):
        for k in range(TOP_K):
            _row_copy(y_hbm.at[pl.ds(pos_ref[0, k, r], 1), :], ybuf.at[k, pl.ds(r, 1), :], sem).start()
        return carry

    lax.fori_loop(0, TM, issue, 0)

    def drain(r, carry):
        for k in range(TOP_K):
            _row_copy(y_hbm.at[pl.ds(pos_ref[0, k, r], 1), :], ybuf.at[k, pl.ds(r, 1), :], sem).wait()
        return carry

    lax.fori_loop(0, TM, drain, 0)
    (g2,) = _mod_rows(mod_ref, b, j, n_batch, d, 5, 1)
    gate = gate_ref[0]
    acc = gate[:, 0:1] * ybuf[0]
    for k in range(1, TOP_K):
        acc = acc + gate[:, k:k + 1] * ybuf[k]
    xo = x_ref[0] + g2 * acc
    if final:
        xo = xo * lax.rsqrt(jnp.mean(xo * xo, axis=-1, keepdims=True) + EPS) * fw_ref[...]
    xo_ref[0] = xo


def _combine(pos, x, gate, mod, fw, y, n_batch, blk_off, final):
    bsz, l, d = x.shape
    n_blk = l // TM - blk_off
    ls = n_blk * TM
    return pl.pallas_call(
        functools.partial(_combine_kernel, n_batch=n_batch, d=d, blk_off=blk_off, final=final),
        grid=(bsz, n_blk),
        in_specs=[pl.BlockSpec((1, TOP_K, TM), lambda b, j: (b, 0, j), memory_space=pltpu.SMEM),
                  pl.BlockSpec((1, TM, d), lambda b, j: (b, j + blk_off, 0)),
                  pl.BlockSpec((1, TM, TOP_K), lambda b, j: (b, j, 0)),
                  _const_spec(mod.shape), _const_spec(fw.shape),
                  pl.BlockSpec(memory_space=pl.ANY)],
        out_specs=pl.BlockSpec((1, TM, d), lambda b, j: (b, j, 0)),
        out_shape=jax.ShapeDtypeStruct((bsz, ls, d), F32),
        scratch_shapes=[pltpu.VMEM((TOP_K, TM, d), F32), pltpu.SemaphoreType.DMA(())],
        compiler_params=_cp(("arbitrary", "arbitrary")),
        name="combine",
    )(pos, x, gate, mod, fw, y)


def _route_plan(idx, n_exp):
    shape = idx.shape
    e = idx.reshape(-1)
    onehot = (e[:, None] == jnp.arange(n_exp, dtype=jnp.int32)[None, :]).astype(jnp.int32)
    cs = jnp.cumsum(onehot, axis=0)
    rank = jnp.sum(onehot * cs, axis=1) - 1
    counts = cs[-1]
    padded = ((counts + EXPERT_TILE - 1) // EXPERT_TILE) * EXPERT_TILE
    ends = jnp.cumsum(padded)
    starts = ends - padded
    pos = (starts[e] + rank).reshape(shape).astype(jnp.int32)
    n_rows = e.shape[0] + n_exp * EXPERT_TILE
    n_tiles = n_rows // EXPERT_TILE
    n_used = (ends[-1] // EXPERT_TILE).astype(jnp.int32)
    tile_start = jnp.minimum(jnp.arange(n_tiles, dtype=jnp.int32), n_used - 1) * EXPERT_TILE
    tile_expert = jnp.minimum(jnp.searchsorted(ends, tile_start, side="right"), n_exp - 1).astype(jnp.int32)
    return pos, tile_expert, n_used.reshape(1), n_rows


def _moe(x, mod, nw, rw, rb, wup, bup, wdn, bdn, fw, n_batch, blk_off, final):
    n_exp = rw.shape[1]
    d = x.shape[-1]
    h, idx, gate = _route(x, mod, nw, rw.T, rb.reshape(n_exp, 1), n_batch, blk_off)
    pos, tile_expert, n_used, n_rows = _route_plan(idx, n_exp)
    hs = _dispatch(pos, h, n_rows)
    y = _experts(tile_expert, n_used, hs, wup.astype(BF16), bup.reshape(n_exp, 1, -1),
                 wdn.astype(BF16), bdn.reshape(n_exp, 1, d))
    return _combine(pos, x, jnp.transpose(gate, (0, 2, 1)), mod, fw, y, n_batch, blk_off, final)


def _pad_heads(w, n_heads, width):
    lead = w.shape[:-1]
    dh = w.shape[-1] // n_heads
    w = w.reshape(lead + (n_heads, dh))
    w = jnp.pad(w, [(0, 0)] * len(lead) + [(0, 0), (0, width - dh)])
    return w.reshape(lead + (n_heads * width,))


def _layer0_weights(w_in, w2_f, b2_f, w2_b, b2_b):
    d = w_in.shape[0]
    sizes = (A_HEADS * A_DK, A_HEADS * A_DK, A_HEADS * A_DV, A_RANK, A_RANK, A_HEADS * A_DV,
             B_HEADS * B_DK, B_HEADS * B_DK, B_HEADS * B_DK, B_HEADS * B_DV, B_HEADS * B_DV)
    aq, ak, av, ar_f, ar_b, aog, bq, bz_f, bz_b, bi, bog = jnp.split(w_in, np.cumsum(sizes)[:-1].tolist(), axis=1)
    ar = jnp.concatenate([ar_f, ar_b, jnp.zeros((d, HEAD_W - 2 * A_RANK), w_in.dtype)], axis=1)
    w = jnp.concatenate([_pad_heads(aq, A_HEADS, HEAD_W), bq, av, bi, aog, bog,
                         _pad_heads(ak, A_HEADS, HEAD_W), bz_f, bz_b, ar], axis=1).astype(BF16)
    hw = A_HEADS * HEAD_W
    w2 = jnp.zeros((2, HEAD_W, hw), F32)
    w2 = w2.at[0, 0:A_RANK].set(_pad_heads(w2_f, A_HEADS, HEAD_W))
    w2 = w2.at[1, A_RANK:2 * A_RANK].set(_pad_heads(w2_b, A_HEADS, HEAD_W))
    b2 = jnp.stack([_pad_heads(b2_f, A_HEADS, HEAD_W), _pad_heads(b2_b, A_HEADS, HEAD_W)]).reshape(2, 1, hw)
    return w, w2.astype(BF16), b2


QKV_W = 2 * C_HEADS * C_DK + C_HEADS * C_DV
OG_W = C_HEADS * C_DV
DQ_W = D_HEADS * D_HD
DKV_W = D_KV_HEADS * HEAD_W


def _proj1_kernel(x_ref, mod_ref, nw_ref, w_ref, cos_ref, sin_ref,
                  qkv_ref, gb_ref, og_ref, dq_ref, dk_ref, dv_ref, *, n_batch, d):
    b, j = pl.program_id(0), pl.program_id(1)
    shift, scale = _mod_rows(mod_ref, b, j, n_batch, d, 0, 2)
    h = _norm_mod(x_ref[0], nw_ref[...], shift, scale).astype(BF16)

    def mm(c0, c1):
        return jnp.dot(h, w_ref[:, c0:c1], preferred_element_type=F32)

    c = 0
    qkv_ref[0] = mm(c, c + QKV_W); c += QKV_W
    gb_ref[0] = mm(c, c + HEAD_W); c += HEAD_W
    og_ref[0] = mm(c, c + OG_W); c += OG_W
    cos, sin = cos_ref[...], sin_ref[...]
    for s in range(DQ_W // HEAD_W):
        xs = mm(c + s * HEAD_W, c + (s + 1) * HEAD_W)
        xp = mm(c + DQ_W + s * HEAD_W, c + DQ_W + (s + 1) * HEAD_W)
        dq_ref[0, :, s * HEAD_W:(s + 1) * HEAD_W] = (xs * cos + xp * sin) * (D_HD ** -0.5)
    c += 2 * DQ_W
    for s in range(DKV_W // HEAD_W):
        xs = mm(c + s * HEAD_W, c + (s + 1) * HEAD_W)
        xp = mm(c + DKV_W + s * HEAD_W, c + DKV_W + (s + 1) * HEAD_W)
        dk_ref[0, :, s * HEAD_W:(s + 1) * HEAD_W] = xs * cos + xp * sin
    c += 2 * DKV_W
    dv_ref[0] = mm(c, c + DKV_W)


def _proj1(x, mod, nw, w, cos, sin, n_batch):
    bsz, l, d = x.shape
    widths = (QKV_W, HEAD_W, OG_W, DQ_W, DKV_W, DKV_W)
    return pl.pallas_call(
        functools.partial(_proj1_kernel, n_batch=n_batch, d=d),
        grid=(bsz, l // TM),
        in_specs=[pl.BlockSpec((1, TM, d), lambda b, j: (b, j, 0)),
                  _const_spec(mod.shape), _const_spec(nw.shape), _const_spec(w.shape),
                  pl.BlockSpec((TM, HEAD_W), lambda b, j: (j, 0)),
                  pl.BlockSpec((TM, HEAD_W), lambda b, j: (j, 0))],
        out_specs=[pl.BlockSpec((1, TM, wd), lambda b, j: (b, j, 0)) for wd in widths],
        out_shape=[jax.ShapeDtypeStruct((bsz, l, wd), F32) for wd in widths],
        compiler_params=_cp(("parallel", "arbitrary")),
        name="proj1",
    )(x, mod, nw, w, cos, sin)


HALO = 8


def _conv1_kernel(cur_ref, prev_ref, next_ref, cw_ref, gbr_ref, alog_ref, dtb_ref,
                  q_ref, k_ref, v_ref, gb_ref, xe_ref):
    j, n_blk = pl.program_id(1), pl.num_programs(1)
    prev_ok = j >= 2
    next_ok = jnp.logical_and(j >= 1, j < n_blk - 1)
    xe_ref[0:HALO, :] = jnp.where(prev_ok, prev_ref[0], 0.0)
    xe_ref[HALO:HALO + TM, :] = cur_ref[0]
    xe_ref[HALO + TM:2 * HALO + TM, :] = jnp.where(next_ok, next_ref[0], 0.0)
    pad = CONV_W // 2
    hw = C_HEADS * HEAD_W
    for g, o_ref in enumerate((q_ref, k_ref, v_ref)):
        cols = slice(g * hw, (g + 1) * hw)
        acc = cw_ref[0:1, cols] * xe_ref[HALO - pad:HALO - pad + TM, cols]
        for i in range(1, CONV_W):
            acc = acc + cw_ref[i:i + 1, cols] * xe_ref[HALO - pad + i:HALO - pad + i + TM, cols]
        y = _silu(acc)
        if g < 2:
            parts = []
            for hh in range(C_HEADS):
                yh = y[:, hh * HEAD_W:(hh + 1) * HEAD_W]
                yh = yh * lax.rsqrt(jnp.sum(yh * yh, axis=-1, keepdims=True) + EPS)
                parts.append(yh * (C_DK ** -0.5) if g == 0 else yh)
            y = jnp.concatenate(parts, axis=-1)
        o_ref[0] = y
    raw = gbr_ref[0]
    z = raw + dtb_ref[...]
    softplus = jnp.maximum(z, 0.0) + jnp.log1p(jnp.exp(-jnp.abs(z)))
    lane = lax.broadcasted_iota(jnp.int32, raw.shape, 1)
    gb_ref[0] = jnp.where(lane < 2 * C_HEADS, jax.nn.sigmoid(raw), -jnp.exp(alog_ref[...]) * softplus)


def _conv1(qkv, conv_w, gb_raw, alog, dtb):
    bsz, l, wd = qkv.shape
    n_blk = l // TM
    per = TM // HALO
    last = l // HALO - 1
    hw = C_HEADS * HEAD_W
    ob = pl.BlockSpec((1, TM, hw), lambda b, j: (b, j, 0))
    gbs = pl.BlockSpec((1, TM, HEAD_W), lambda b, j: (b, j, 0))
    return pl.pallas_call(
        _conv1_kernel,
        grid=(bsz, n_blk),
        in_specs=[pl.BlockSpec((1, TM, wd), lambda b, j: (b, j, 0)),
                  pl.BlockSpec((1, HALO, wd), lambda b, j: (b, jnp.maximum(j * per - 1, 0), 0)),
                  pl.BlockSpec((1, HALO, wd), lambda b, j: (b, jnp.minimum((j + 1) * per, last), 0)),
                  _const_spec(conv_w.shape), gbs, _const_spec(alog.shape), _const_spec(dtb.shape)],
        out_specs=[ob, ob, ob, gbs],
        out_shape=[jax.ShapeDtypeStruct((bsz, l, hw), F32)] * 3 + [jax.ShapeDtypeStruct((bsz, l, HEAD_W), F32)],
        scratch_shapes=[pltpu.VMEM((TM + 2 * HALO, wd), F32)],
        compiler_params=_cp(("parallel", "arbitrary")),
        name="conv1",
    )(qkv, qkv, qkv, conv_w, gb_raw, alog, dtb)


_MERGE = (16, 32, 64)
_BASE = 8


def _delta_masks():
    t = np.arange(CHUNK)[:, None]
    s = np.arange(CHUNK)[None, :]
    out = [s <= t, s < t, s >= t, s > t, (t // _BASE) == (s // _BASE)]
    for m2 in _MERGE:
        out.append(((t // m2) == (s // m2)) & ((t // (m2 // 2)) != (s // (m2 // 2))))
    return np.stack(out).astype(np.float32)


def _bdot(a, b):
    return jnp.dot(a.astype(BF16), b.astype(BF16), preferred_element_type=F32)


DELTA_HEADS = 4


def _delta_prepare(chains, dm):
    n = range(len(chains))
    qs, ks, vs, gs, betas, revs = zip(*chains)
    incl = [dm[2] if r else dm[0] for r in revs]
    strict = [dm[3] if r else dm[1] for r in revs]
    cum = [_cumsum_rows(jnp.broadcast_to(gs[i], qs[i].shape), revs[i]) for i in n]
    decay = [incl[i] * jnp.exp(jnp.where(incl[i] > 0, cum[i][:, 0:CHUNK] - cum[i].T[0:CHUNK, :], 0.0)) for i in n]
    kbeta = [ks[i] * betas[i] for i in n]
    kbf = [ks[i].astype(BF16) for i in n]
    a = [strict[i] * _dot_nt(kbeta[i].astype(BF16), kbf[i]) * decay[i] for i in n]
    eye = dm[0] * dm[2]
    n0 = [-(a[i] * dm[4]) for i in n]
    n2 = [_bdot(n0[i], n0[i]) for i in n]
    n4 = [_bdot(n2[i], n2[i]) for i in n]
    t = [eye + n0[i] for i in n]
    t = [t[i] + _bdot(t[i], n2[i]) for i in n]
    t = [t[i] + _bdot(t[i], n4[i]) for i in n]
    for li in range(len(_MERGE)):
        inner = [_bdot(a[i] * dm[5 + li], t[i]) for i in n]
        t = [t[i] - _bdot(t[i], inner[i]) for i in n]
    uw = [_bdot(t[i], jnp.concatenate([vs[i] * betas[i], kbeta[i] * jnp.exp(cum[i])], axis=1)) for i in n]
    scores = [(_dot_nt(qs[i].astype(BF16), kbf[i]) * decay[i]).astype(BF16) for i in n]
    edge = [cum[i][0:1] if revs[i] else cum[i][CHUNK - 1:CHUNK] for i in n]
    q_in = [(qs[i] * jnp.exp(cum[i])).astype(BF16) for i in n]
    k_out = [(ks[i] * jnp.exp(edge[i] - cum[i])).astype(BF16) for i in n]
    carry = [jnp.exp(edge[i]) for i in n]
    return [(uw[i][:, 0:HEAD_W], uw[i][:, HEAD_W:].astype(BF16), scores[i], q_in[i], k_out[i], carry[i]) for i in n]


def _delta_step(prep, states):
    n = range(len(prep))
    sb = [states[i].astype(BF16) for i in n]
    v_new = [prep[i][0] - jnp.dot(prep[i][1], sb[i], preferred_element_type=F32) for i in n]
    vb = [v_new[i].astype(BF16) for i in n]
    o = [jnp.dot(prep[i][3], sb[i], preferred_element_type=F32)
         + jnp.dot(prep[i][2], vb[i], preferred_element_type=F32) for i in n]
    new = [states[i] * prep[i][5] + _dot_tn(prep[i][4], vb[i]) for i in n]
    return o, new


def _lane_col(x, lane_idx):
    lane = lax.broadcasted_iota(jnp.int32, x.shape, 1)
    return jnp.sum(jnp.where(lane == lane_idx, x, 0.0), axis=-1, keepdims=True)


def _delta_kernel(dm_ref, qf_ref, kf_ref, vf_ref, gbf_ref, qb_ref, kb_ref, vb_ref, gbb_ref,
                  of_ref, ob_ref, st_ref):
    @pl.when(pl.program_id(2) == 0)
    def _():
        st_ref[...] = jnp.zeros_like(st_ref)

    dm = [dm_ref[i] for i in range(dm_ref.shape[0])]
    n_ch = TM // CHUNK
    gbf, gbb = gbf_ref[0], gbb_ref[0]
    chains = []
    cols = []
    for hh in range(DELTA_HEADS):
        h = pl.program_id(1) * DELTA_HEADS + hh
        cols.append((_lane_col(gbf, h), _lane_col(gbf, 2 * C_HEADS + h),
                     _lane_col(gbb, C_HEADS + h), _lane_col(gbb, 3 * C_HEADS + h)))
    for c in range(n_ch):
        rf = slice(c * CHUNK, (c + 1) * CHUNK)
        rb = slice((n_ch - 1 - c) * CHUNK, (n_ch - c) * CHUNK)
        for hh in range(DELTA_HEADS):
            lanes = slice(hh * HEAD_W, (hh + 1) * HEAD_W)
            beta_f, g_f, beta_b, g_b = cols[hh]
            chains.append((qf_ref[0, rf, lanes], kf_ref[0, rf, lanes], vf_ref[0, rf, lanes], g_f[rf], beta_f[rf], False))
            chains.append((qb_ref[0, rb, lanes], kb_ref[0, rb, lanes], vb_ref[0, rb, lanes], g_b[rb], beta_b[rb], True))
    prep = _delta_prepare(chains, dm)
    per = 2 * DELTA_HEADS
    states = [st_ref[i] for i in range(per)]
    for c in range(n_ch):
        rf = slice(c * CHUNK, (c + 1) * CHUNK)
        rb = slice((n_ch - 1 - c) * CHUNK, (n_ch - c) * CHUNK)
        outs, states = _delta_step(prep[c * per:(c + 1) * per], states)
        for hh in range(DELTA_HEADS):
            lanes = slice(hh * HEAD_W, (hh + 1) * HEAD_W)
            of_ref[0, rf, lanes] = outs[2 * hh]
            ob_ref[0, rb, lanes] = outs[2 * hh + 1]
    for i in range(per):
        st_ref[i] = states[i]


def _delta(dmasks, q, k, v, gb):
    bsz, l, wide = q.shape
    n_blk = l // TM
    bw = DELTA_HEADS * HEAD_W
    fwd = pl.BlockSpec((1, TM, bw), lambda b, h, i: (b, i, h))
    bwd = pl.BlockSpec((1, TM, bw), lambda b, h, i: (b, _bwd_block(i, n_blk), h))
    gf = pl.BlockSpec((1, TM, HEAD_W), lambda b, h, i: (b, i, 0))
    gbw = pl.BlockSpec((1, TM, HEAD_W), lambda b, h, i: (b, _bwd_block(i, n_blk), 0))
    return pl.pallas_call(
        _delta_kernel,
        grid=(bsz, wide // bw, n_blk),
        in_specs=[_const_spec(dmasks.shape), fwd, fwd, fwd, gf, bwd, bwd, bwd, gbw],
        out_specs=[fwd, bwd],
        out_shape=[jax.ShapeDtypeStruct((bsz, l, wide), F32)] * 2,
        scratch_shapes=[pltpu.VMEM((2 * DELTA_HEADS, HEAD_W, HEAD_W), F32)],
        compiler_params=_cp(("parallel", "parallel", "arbitrary")),
        name="delta",
    )(dmasks, q, k, v, gb, q, k, v, gb)


def _attn_kernel(q_ref, kp_ref, kc_ref, kn_ref, vp_ref, vc_ref, vn_ref, kx_ref, vx_ref, sink_ref, o_ref, *, t_len):
    i = pl.program_id(1)
    q = q_ref[0]
    kl = jnp.concatenate([kp_ref[0], kc_ref[0], kn_ref[0]], axis=0).astype(BF16)
    vl = jnp.concatenate([vp_ref[0], vc_ref[0], vn_ref[0]], axis=0).astype(BF16)
    kx, vx = kx_ref[0].astype(BF16), vx_ref[0].astype(BF16)
    qpos = lax.broadcasted_iota(jnp.int32, (WINDOW, 3 * WINDOW), 0)
    kpos = lax.broadcasted_iota(jnp.int32, (WINDOW, 3 * WINDOW), 1) - WINDOW
    k_abs = i * WINDOW + kpos
    valid = (jnp.abs(kpos - qpos) <= WINDOW) & (k_abs >= 0) & (k_abs < t_len)
    low = lax.broadcasted_iota(jnp.int32, (WINDOW, HEAD_W), 1) < D_HD
    group = D_HEADS // D_KV_HEADS
    heads = range(D_HEADS)
    cols = [slice((h // group) * HEAD_W, (h // group + 1) * HEAD_W) for h in heads]
    qm = [jnp.where(low if h % 2 == 0 else jnp.logical_not(low), q[:, (h // 2) * HEAD_W:(h // 2 + 1) * HEAD_W], 0.0)
          .astype(BF16) for h in heads]
    s_l = [jnp.where(valid, _dot_nt(qm[h], kl[:, cols[h]]), -jnp.inf) for h in heads]
    s_x = [_dot_nt(qm[h], kx[:, cols[h]]) for h in heads]
    sink = [sink_ref[:, h:h + 1] for h in heads]
    m = [jnp.maximum(jnp.maximum(jnp.max(s_l[h], axis=-1, keepdims=True), jnp.max(s_x[h], axis=-1, keepdims=True)),
                     sink[h]) for h in heads]
    p_l = [jnp.exp(s_l[h] - m[h]) for h in heads]
    p_x = [jnp.exp(s_x[h] - m[h]) for h in heads]
    den = [jnp.sum(p_l[h], axis=-1, keepdims=True) + jnp.sum(p_x[h], axis=-1, keepdims=True) + jnp.exp(sink[h] - m[h])
           for h in heads]
    o = [(jnp.dot(p_l[h].astype(BF16), vl[:, cols[h]], preferred_element_type=F32)
          + jnp.dot(p_x[h].astype(BF16), vx[:, cols[h]], preferred_element_type=F32)) / den[h] for h in heads]
    for p in range(D_HEADS // 2):
        o_ref[0, :, p * HEAD_W:(p + 1) * HEAD_W] = jnp.where(low, o[2 * p], o[2 * p + 1])


def _attn(dq, dk, dv, sinks, n_ctx):
    bsz, l, _ = dq.shape
    t_len = l - n_ctx
    nq = t_len // WINDOW
    off = n_ctx // WINDOW

    def kv(delta):
        return pl.BlockSpec((1, WINDOW, DKV_W), lambda b, i: (b, off + jnp.clip(i + delta, 0, nq - 1), 0))

    ctx = pl.BlockSpec((1, n_ctx, DKV_W), lambda b, i: (b, 0, 0))
    return pl.pallas_call(
        functools.partial(_attn_kernel, t_len=t_len),
        grid=(bsz, nq),
        in_specs=[pl.BlockSpec((1, WINDOW, DQ_W), lambda b, i: (b, off + i, 0)),
                  kv(-1), kv(0), kv(1), kv(-1), kv(0), kv(1), ctx, ctx, _const_spec(sinks.shape)],
        out_specs=pl.BlockSpec((1, WINDOW, DQ_W), lambda b, i: (b, i, 0)),
        out_shape=jax.ShapeDtypeStruct((bsz, t_len, DQ_W), F32),
        compiler_params=_cp(("parallel", "arbitrary")),
        name="attn",
    )(dq, dk, dk, dk, dv, dv, dv, dk, dv, sinks)


def _read1_kernel(x_ref, of_ref, ob_ref, og_ref, od_ref, nw_ref, wout_ref, mod_ref, xo_ref, *, n_batch, d):
    b, j = pl.program_id(0), pl.program_id(1) + 1
    (g1,) = _mod_rows(mod_ref, b, j, n_batch, d, 2, 1)
    o = of_ref[0] + ob_ref[0]
    parts = [_head_rms(o[:, hh * HEAD_W:(hh + 1) * HEAD_W], nw_ref[...]) for hh in range(C_HEADS)]
    y = jnp.concatenate(parts, axis=-1) * _silu(og_ref[0])
    y = jnp.concatenate([y, od_ref[0]], axis=-1)
    xo_ref[0] = x_ref[0] + g1 * jnp.dot(y.astype(BF16), wout_ref[...], preferred_element_type=F32)


def _read1(x, of, ob, og, od, nw, wout, mod, n_batch):
    bsz, l, d = x.shape
    lat = lambda wd: pl.BlockSpec((1, TM, wd), lambda b, j: (b, j + 1, 0))
    return pl.pallas_call(
        functools.partial(_read1_kernel, n_batch=n_batch, d=d),
        grid=(bsz, l // TM - 1),
        in_specs=[lat(d), lat(OG_W), lat(OG_W), lat(OG_W),
                  pl.BlockSpec((1, TM, DQ_W), lambda b, j: (b, j, 0)),
                  _const_spec(nw.shape), _const_spec(wout.shape), _const_spec(mod.shape)],
        out_specs=lat(d),
        out_shape=jax.ShapeDtypeStruct(x.shape, F32),
        input_output_aliases={0: 0},
        compiler_params=_cp(("parallel", "arbitrary")),
        name="read1",
    )(x, of, ob, og, od, nw, wout, mod)


def _swap_halves(w, n_heads):
    lead = w.shape[:-1]
    dh = w.shape[-1] // n_heads
    w = w.reshape(lead + (n_heads, 2, dh // 2))
    return w[..., ::-1, :].reshape(lead + (n_heads * dh,))


def _dup_heads(w, n_heads):
    lead = w.shape[:-1]
    dh = w.shape[-1] // n_heads
    w = w.reshape(lead + (n_heads, 1, dh))
    return jnp.concatenate([w, w], axis=-2).reshape(lead + (2 * n_heads * dh,))


def _layer1_weights(w_in):
    d = w_in.shape[0]
    sizes = (C_HEADS * C_DK, C_HEADS * C_DK, C_HEADS * C_DV, C_HEADS, C_HEADS, C_HEADS, C_HEADS,
             C_HEADS * C_DV, D_HEADS * D_HD, D_KV_HEADS * D_HD, D_KV_HEADS * D_HD)
    cq, ck, cv, bt_f, bt_b, a_f, a_b, og, dq, dk, dv = jnp.split(w_in, np.cumsum(sizes)[:-1].tolist(), axis=1)
    gates = jnp.concatenate([bt_f, bt_b, a_f, a_b, jnp.zeros((d, HEAD_W - 4 * C_HEADS), w_in.dtype)], axis=1)
    return jnp.concatenate([cq, ck, cv, gates, og, dq, _swap_halves(dq, D_HEADS),
                            _dup_heads(dk, D_KV_HEADS), _dup_heads(_swap_halves(dk, D_KV_HEADS), D_KV_HEADS),
                            _dup_heads(dv, D_KV_HEADS)], axis=1).astype(BF16)


def _rope_tables(n_ctx, t_len):
    rows = t_len // GRID_W
    row = jnp.repeat(jnp.arange(rows, dtype=F32), GRID_W)
    col = jnp.tile(jnp.arange(GRID_W, dtype=F32), rows)
    n_freq = D_HD // 4
    inv = ROPE_BASE ** (-jnp.arange(n_freq, dtype=F32) / n_freq)
    ang = jnp.concatenate([row[:, None] * inv, col[:, None] * inv], axis=-1)
    cos, sin = jnp.cos(ang), jnp.sin(ang)
    cos = jnp.concatenate([jnp.ones((n_ctx, D_HD // 2), F32), cos], axis=0)
    sin = jnp.concatenate([jnp.zeros((n_ctx, D_HD // 2), F32), sin], axis=0)
    return jnp.concatenate([cos] * 4, axis=1), jnp.concatenate([-sin, sin] * 2, axis=1)


def _mixer1(xx, mod, nw, w_in, w_out, conv_w, a_log_f, dt_bias_f, a_log_b, dt_bias_b, dn_norm_w, sinks, n_batch, n_ctx):
    l = xx.shape[1]
    cos, sin = _rope_tables(n_ctx, l - n_ctx)
    qkv, gb_raw, og, dq, dk, dv = _proj1(xx, mod, nw, _layer1_weights(w_in), cos, sin, n_batch)
    zero4 = jnp.zeros((C_HEADS,), F32)
    fill = jnp.zeros((HEAD_W - 4 * C_HEADS,), F32)
    alog = jnp.concatenate([zero4, zero4, a_log_f, a_log_b, fill]).reshape(1, HEAD_W)
    dtb = jnp.concatenate([zero4, zero4, dt_bias_f, dt_bias_b, fill]).reshape(1, HEAD_W)
    cq, ck, cv, gb = _conv1(qkv, conv_w, gb_raw, alog, dtb)
    of, ob = _delta(jnp.asarray(_delta_masks()), cq, ck, cv, gb)
    sink_row = jnp.concatenate([sinks, jnp.zeros((HEAD_W - D_HEADS,), F32)]).reshape(1, HEAD_W)
    od = _attn(dq, dk, dv, sink_row, n_ctx)
    return _read1(xx, of, ob, og, od, dn_norm_w.reshape(1, -1), w_out.astype(BF16), mod, n_batch)


def kernel(x, c, ctx, c_ctx, l0_ada_w, l0_ada_b, l0_norm_mix_w, l0_w_in, l0_w_out, l0_gla_w2_f, l0_gla_b_f, l0_gla_w2_b, l0_gla_b_b, l0_gla_norm_w, l0_hgrn_norm_w, hgrn_lb_logits, l0_norm_ffn_w, l0_router_w, l0_router_b, l0_w_up, l0_b_up, l0_w_down, l0_b_down, l1_ada_w, l1_ada_b, l1_norm_mix_w, l1_w_in, l1_w_out, l1_conv_w, l1_a_log_f, l1_dt_bias_f, l1_a_log_b, l1_dt_bias_b, l1_dn_norm_w, l1_sinks, l1_norm_ffn_w, l1_router_w, l1_router_b, l1_w_up, l1_b_up, l1_w_down, l1_b_down, final_norm_w):
    bsz, t, d = x.shape
    n_ctx = ctx.shape[1]
    assert n_ctx == TM and t % TM == 0
    xx = jnp.concatenate([ctx, x], axis=1)
    mod_rows = -(-(bsz + 1) // 8) * 8
    c_all = jnp.zeros((mod_rows, d), F32).at[:bsz].set(c).at[bsz].set(c_ctx)
    ones = jnp.ones((1, d), F32)

    mod0 = _ada_table(c_all, l0_ada_w, l0_ada_b)
    w0, w2, b2 = _layer0_weights(l0_w_in, l0_gla_w2_f, l0_gla_b_f, l0_gla_w2_b, l0_gla_b_b)
    q, kf, kb, v, lff, lfb, og = _proj0(xx, mod0, l0_norm_mix_w.reshape(1, d), w0, w2, b2, hgrn_lb_logits, bsz)
    of, ob = _scan0(jnp.asarray(_level_masks()), q, kf, kb, v, lff, lfb)
    xx = _read0(xx, of, ob, og, l0_gla_norm_w.reshape(1, -1), l0_hgrn_norm_w.reshape(1, -1), l0_w_out.astype(BF16), mod0, bsz)
    xx = _moe(xx, mod0, l0_norm_ffn_w.reshape(1, d), l0_router_w, l0_router_b, l0_w_up, l0_b_up, l0_w_down, l0_b_down,
              ones, bsz, 0, False)

    mod1 = _ada_table(c_all, l1_ada_w, l1_ada_b)
    xx = _mixer1(xx, mod1, l1_norm_mix_w.reshape(1, d), l1_w_in, l1_w_out, l1_conv_w, l1_a_log_f, l1_dt_bias_f,
                 l1_a_log_b, l1_dt_bias_b, l1_dn_norm_w, l1_sinks, bsz, n_ctx)
    return _moe(xx, mod1, l1_norm_ffn_w.reshape(1, d), l1_router_w, l1_router_b, l1_w_up, l1_b_up, l1_w_down, l1_b_down,
                final_norm_w.reshape(1, d), bsz, 1, True)
```

```python
import functools

import numpy as np
import jax
import jax.numpy as jnp
from jax import lax
from jax.experimental import pallas as pl
from jax.experimental.pallas import tpu as pltpu

F32 = jnp.float32
BF16 = jnp.bfloat16
EPS = 1e-6

CHUNK = 64
A_HEADS, A_DK, A_DV, A_RANK = 4, 64, 128, 16
GATE_TAU = 16.0
B_HEADS, B_DK, B_DV = 4, 128, 128
C_HEADS, C_DK, C_DV = 4, 128, 128
CONV_W = 5
D_HEADS, D_KV_HEADS, D_HD = 8, 2, 64
WINDOW = 128
GRID_W = 64
ROPE_BASE = 10000.0
TOP_K = 4
SWIGLU_LIMIT = 7.0
SWIGLU_ALPHA = 1.702

TM = 256
HEAD_W = 128
EXPERT_TILE = 512
FF_CHUNK = 512
VMEM_LIMIT = 48 * 1024 * 1024


def _cp(sem, vmem=VMEM_LIMIT):
    return pltpu.CompilerParams(dimension_semantics=sem, vmem_limit_bytes=vmem)


def _const_spec(shape):
    nd = len(shape)
    return pl.BlockSpec(shape, lambda *_: (0,) * nd)


def _silu(x):
    return x * jax.nn.sigmoid(x)


def _log_sigmoid(z):
    return jnp.minimum(z, 0.0) - jnp.log1p(jnp.exp(-jnp.abs(z)))


def _norm_mod(x, nw, shift, scale):
    y = x * lax.rsqrt(jnp.mean(x * x, axis=-1, keepdims=True) + EPS) * nw
    return y * (1.0 + scale) + shift


def _mod_rows(mod_ref, batch, blk, n_batch, d, first, count):
    row = jnp.where(blk == 0, n_batch, batch)
    return [mod_ref[pl.ds(row, 1), (first + i) * d:(first + i + 1) * d] for i in range(count)]


def _ada_kernel(c_ref, w_ref, b_ref, o_ref):
    s = _silu(c_ref[...])
    o_ref[...] = jnp.dot(s, w_ref[...], precision=lax.Precision.HIGHEST,
                         preferred_element_type=F32) + b_ref[...]


def _ada_table(c_all, w, b):
    rows, d = c_all.shape
    n = w.shape[1]
    bn = d
    return pl.pallas_call(
        _ada_kernel,
        grid=(n // bn,),
        in_specs=[pl.BlockSpec((rows, d), lambda j: (0, 0)),
                  pl.BlockSpec((d, bn), lambda j: (0, j)),
                  pl.BlockSpec((1, bn), lambda j: (0, j))],
        out_specs=pl.BlockSpec((rows, bn), lambda j: (0, j)),
        out_shape=jax.ShapeDtypeStruct((rows, n), F32),
        compiler_params=_cp(("arbitrary",)),
        name="ada_table",
    )(c_all, w, b.reshape(1, n))


def _proj0_kernel(x_ref, mod_ref, nw_ref, w_ref, w2_ref, b2_ref, lbl_ref,
                  q_ref, kf_ref, kb_ref, v_ref, lff_ref, lfb_ref, og_ref, *, n_batch, d):
    b, j = pl.program_id(0), pl.program_id(1)
    shift, scale = _mod_rows(mod_ref, b, j, n_batch, d, 0, 2)
    h = _norm_mod(x_ref[0], nw_ref[...], shift, scale).astype(BF16)

    def mm(c0, c1):
        return jnp.dot(h, w_ref[:, c0:c1], preferred_element_type=F32)

    hw = A_HEADS * HEAD_W
    q_ref[0, :, 0:hw] = mm(0, hw) * (A_DK ** -0.5)
    q_ref[0, :, hw:2 * hw] = mm(hw, 2 * hw)
    v_ref[0] = mm(2 * hw, 4 * hw)
    og_ref[0] = mm(4 * hw, 6 * hw)
    kg = mm(6 * hw, 7 * hw)
    kf_ref[0, :, 0:hw] = kg
    kb_ref[0, :, 0:hw] = kg
    lg = lbl_ref[...]
    e = jnp.exp(lg - jnp.max(lg, axis=0, keepdims=True))
    lb = e[0:1] / jnp.sum(e, axis=0, keepdims=True)
    log_lb, log_1m = jnp.log(lb), jnp.log1p(-lb)
    ar = mm(9 * hw, 9 * hw + HEAD_W).astype(BF16)
    for di, (k_ref, lf_ref) in enumerate(((kf_ref, lff_ref), (kb_ref, lfb_ref))):
        z = mm((7 + di) * hw, (8 + di) * hw)
        s1 = log_1m + _log_sigmoid(z)
        lf_ref[0, :, hw:2 * hw] = jnp.maximum(log_lb, s1) + jnp.log1p(jnp.exp(-jnp.abs(log_lb - s1)))
        k_ref[0, :, hw:2 * hw] = (1.0 - lb) * jax.nn.sigmoid(-z)
        za = jnp.dot(ar, w2_ref[di], preferred_element_type=F32) + b2_ref[di]
        lf_ref[0, :, 0:hw] = _log_sigmoid(za) * (1.0 / GATE_TAU)


def _proj0(x, mod, nw, w, w2, b2, lbl, n_batch):
    bsz, l, d = x.shape
    n_blk = l // TM
    wide = 2 * A_HEADS * HEAD_W
    blk = pl.BlockSpec((1, TM, wide), lambda b, j: (b, j, 0))
    return pl.pallas_call(
        functools.partial(_proj0_kernel, n_batch=n_batch, d=d),
        grid=(bsz, n_blk),
        in_specs=[pl.BlockSpec((1, TM, d), lambda b, j: (b, j, 0)),
                  _const_spec(mod.shape), _const_spec(nw.shape), _const_spec(w.shape),
                  _const_spec(w2.shape), _const_spec(b2.shape), _const_spec(lbl.shape)],
        out_specs=[blk] * 7,
        out_shape=[jax.ShapeDtypeStruct((bsz, l, wide), F32)] * 7,
        compiler_params=_cp(("parallel", "arbitrary")),
        name="proj0",
    )(x, mod, nw, w, w2, b2, lbl)


_LEVELS = (32, 16, 8, 4, 2, 1)


def _level_masks():
    t = np.arange(CHUNK)[:, None]
    s = np.arange(CHUNK)[None, :]
    out = np.zeros((2, len(_LEVELS) + 1, CHUNK, CHUNK), np.float32)
    for li, m in enumerate(_LEVELS):
        same = (t // (2 * m)) == (s // (2 * m))
        fwd = same & (t % (2 * m) >= m) & (s % (2 * m) < m)
        out[0, li] = fwd
        out[1, li] = fwd.T
    out[:, -1] = np.eye(CHUNK)
    return out


def _cumsum_rows(x, reverse):
    n = x.shape[0]
    r = lax.broadcasted_iota(jnp.int32, x.shape, 0)
    sh = 1
    while sh < n:
        if reverse:
            x = x + jnp.where(r < n - sh, pltpu.roll(x, n - sh, 0), 0.0)
        else:
            x = x + jnp.where(r >= sh, pltpu.roll(x, sh, 0), 0.0)
        sh *= 2
    return x


def _level_ref(cum, m, reverse):
    n = cum.shape[0]
    tgt = m if reverse else m - 1
    if 2 * m >= 8:
        parts = [jnp.broadcast_to(cum[g + tgt:g + tgt + 1, :], (2 * m, cum.shape[1]))
                 for g in range(0, n, 2 * m)]
        return parts[0] if len(parts) == 1 else jnp.concatenate(parts, axis=0)
    pos = lax.broadcasted_iota(jnp.int32, cum.shape, 0) % (2 * m)
    out = cum
    for p in range(2 * m):
        if p == tgt:
            continue
        shift = (p - tgt) % n
        out = jnp.where(pos == p, pltpu.roll(cum, shift, 0), out)
    return out


def _dot_nt(a, b):
    return lax.dot_general(a, b, (((1,), (1,)), ((), ())), preferred_element_type=F32)


def _dot_tn(a, b):
    return lax.dot_general(a, b, (((0,), (0,)), ((), ())), preferred_element_type=F32)


def _gated_chunk(q, k, v, lf, st, masks, reverse):
    cum = _cumsum_rows(lf, reverse)
    scores = masks[len(_LEVELS)] * _dot_nt(q.astype(BF16), k.astype(BF16))
    for li, m in enumerate(_LEVELS):
        w = jnp.exp(-jnp.abs(cum - _level_ref(cum, m, reverse)))
        scores = scores + masks[li] * _dot_nt((q * w).astype(BF16), (k * w).astype(BF16))
    o = jnp.dot(scores.astype(BF16), v.astype(BF16), preferred_element_type=F32)
    o = o + _dot_nt((q * jnp.exp(cum)).astype(BF16), st.astype(BF16))
    edge = cum[0:1] if reverse else cum[CHUNK - 1:CHUNK]
    kd = k * jnp.exp(edge - cum)
    st_new = st * jnp.exp(edge) + _dot_tn(v.astype(BF16), kd.astype(BF16))
    return o, st_new


def _scan0_kernel(m_ref, qf_ref, kf_ref, vf_ref, lff_ref, qb_ref, kb_ref, vb_ref, lfb_ref,
                  of_ref, ob_ref, stf_ref, stb_ref):
    @pl.when(pl.program_id(2) == 0)
    def _():
        stf_ref[...] = jnp.zeros_like(stf_ref)
        stb_ref[...] = jnp.zeros_like(stb_ref)

    mf = [m_ref[0, i] for i in range(len(_LEVELS) + 1)]
    mb = [m_ref[1, i] for i in range(len(_LEVELS) + 1)]
    stf, stb = stf_ref[...], stb_ref[...]
    n_ch = TM // CHUNK
    for c in range(n_ch):
        rf = slice(c * CHUNK, (c + 1) * CHUNK)
        rb = slice((n_ch - 1 - c) * CHUNK, (n_ch - c) * CHUNK)
        o, stf = _gated_chunk(qf_ref[0, rf, :], kf_ref[0, rf, :], vf_ref[0, rf, :], lff_ref[0, rf, :], stf, mf, False)
        of_ref[0, rf, :] = o
        o, stb = _gated_chunk(qb_ref[0, rb, :], kb_ref[0, rb, :], vb_ref[0, rb, :], lfb_ref[0, rb, :], stb, mb, True)
        ob_ref[0, rb, :] = o
    stf_ref[...] = stf
    stb_ref[...] = stb


def _bwd_block(i, n_blk):
    return jnp.where(i == 0, 0, n_blk - i)


def _scan0(masks, q, kf, kb, v, lff, lfb):
    bsz, l, wide = q.shape
    n_blk = l // TM
    n_heads = wide // HEAD_W
    fwd = pl.BlockSpec((1, TM, HEAD_W), lambda b, h, i: (b, i, h))
    bwd = pl.BlockSpec((1, TM, HEAD_W), lambda b, h, i: (b, _bwd_block(i, n_blk), h))
    return pl.pallas_call(
        _scan0_kernel,
        grid=(bsz, n_heads, n_blk),
        in_specs=[_const_spec(masks.shape), fwd, fwd, fwd, fwd, bwd, bwd, bwd, bwd],
        out_specs=[fwd, bwd],
        out_shape=[jax.ShapeDtypeStruct((bsz, l, wide), F32)] * 2,
        scratch_shapes=[pltpu.VMEM((HEAD_W, HEAD_W), F32), pltpu.VMEM((HEAD_W, HEAD_W), F32)],
        compiler_params=_cp(("parallel", "parallel", "arbitrary")),
        name="scan0",
    )(masks, q, kf, v, lff, q, kb, v, lfb)


def _head_rms(o, w):
    return o * lax.rsqrt(jnp.mean(o * o, axis=-1, keepdims=True) + EPS) * w


def _read0_kernel(x_ref, of_ref, ob_ref, og_ref, nwa_ref, nwb_ref, wout_ref, mod_ref, xo_ref, *, n_batch, d):
    b, j = pl.program_id(0), pl.program_id(1)
    (g1,) = _mod_rows(mod_ref, b, j, n_batch, d, 2, 1)
    o = of_ref[0] + ob_ref[0]
    parts = []
    for hh in range(A_HEADS + B_HEADS):
        nw = nwa_ref[...] if hh < A_HEADS else nwb_ref[...]
        parts.append(_head_rms(o[:, hh * HEAD_W:(hh + 1) * HEAD_W], nw))
    y = jnp.concatenate(parts, axis=-1) * _silu(og_ref[0])
    yo = jnp.dot(y.astype(BF16), wout_ref[...], preferred_element_type=F32)
    xo_ref[0] = x_ref[0] + g1 * yo


def _read0(x, of, ob, og, nwa, nwb, wout, mod, n_batch):
    bsz, l, d = x.shape
    wide = of.shape[-1]
    xb = pl.BlockSpec((1, TM, d), lambda b, j: (b, j, 0))
    wb = pl.BlockSpec((1, TM, wide), lambda b, j: (b, j, 0))
    return pl.pallas_call(
        functools.partial(_read0_kernel, n_batch=n_batch, d=d),
        grid=(bsz, l // TM),
        in_specs=[xb, wb, wb, wb, _const_spec(nwa.shape), _const_spec(nwb.shape),
                  _const_spec(wout.shape), _const_spec(mod.shape)],
        out_specs=xb,
        out_shape=jax.ShapeDtypeStruct(x.shape, F32),
        compiler_params=_cp(("parallel", "arbitrary")),
        name="read0",
    )(x, of, ob, og, nwa, nwb, wout, mod)


def _route_kernel(x_ref, mod_ref, nw_ref, rwt_ref, rb_ref, h_ref, idx_ref, gate_ref, *, n_batch, d, blk_off):
    b, j = pl.program_id(0), pl.program_id(1) + blk_off
    shift, scale = _mod_rows(mod_ref, b, j, n_batch, d, 3, 2)
    h = _norm_mod(x_ref[0], nw_ref[...], shift, scale)
    h_ref[0] = h
    logits = lax.dot_general(rwt_ref[...], h, (((1,), (1,)), ((), ())), precision=lax.Precision.HIGHEST,
                             preferred_element_type=F32) + rb_ref[...]
    n_exp = logits.shape[0]
    rows = lax.broadcasted_iota(jnp.int32, logits.shape, 0)
    vals, idxs = [], []
    for _ in range(TOP_K):
        m = jnp.max(logits, axis=0, keepdims=True)
        i = jnp.min(jnp.where(logits == m, rows, n_exp), axis=0, keepdims=True)
        vals.append(m)
        idxs.append(i)
        logits = jnp.where(rows == i, -jnp.inf, logits)
    ex = [jnp.exp(v - vals[0]) for v in vals]
    tot = ex[0] + ex[1] + ex[2] + ex[3]
    idx_ref[0] = jnp.concatenate(idxs, axis=0)
    gate_ref[0] = jnp.concatenate([e / tot for e in ex], axis=0)


def _route(x, mod, nw, rwt, rb, n_batch, blk_off):
    bsz, l, d = x.shape
    n_blk = l // TM - blk_off
    ls = n_blk * TM
    n_exp = rwt.shape[0]
    return pl.pallas_call(
        functools.partial(_route_kernel, n_batch=n_batch, d=d, blk_off=blk_off),
        grid=(bsz, n_blk),
        in_specs=[pl.BlockSpec((1, TM, d), lambda b, j: (b, j + blk_off, 0)),
                  _const_spec(mod.shape), _const_spec(nw.shape), _const_spec(rwt.shape), _const_spec(rb.shape)],
        out_specs=[pl.BlockSpec((1, TM, d), lambda b, j: (b, j, 0)),
                   pl.BlockSpec((1, TOP_K, TM), lambda b, j: (b, 0, j)),
                   pl.BlockSpec((1, TOP_K, TM), lambda b, j: (b, 0, j))],
        out_shape=[jax.ShapeDtypeStruct((bsz, ls, d), F32),
                   jax.ShapeDtypeStruct((bsz, TOP_K, ls), jnp.int32),
                   jax.ShapeDtypeStruct((bsz, TOP_K, ls), F32)],
        compiler_params=_cp(("parallel", "arbitrary")),
        name="route",
    )(x, mod, nw, rwt, rb)


def _row_copy(src, dst, sem):
    return pltpu.make_async_copy(src, dst, sem)


def _dispatch_kernel(pos_ref, h_ref, hs_in_ref, hs_ref, sem):
    del hs_in_ref

    def issue(r, carry):
        for k in range(TOP_K):
            _row_copy(h_ref.at[0, pl.ds(r, 1), :], hs_ref.at[pl.ds(pos_ref[0, k, r], 1), :], sem).start()
        return carry

    lax.fori_loop(0, TM, issue, 0)

    def drain(r, carry):
        for k in range(TOP_K):
            _row_copy(h_ref.at[0, pl.ds(r, 1), :], hs_ref.at[pl.ds(pos_ref[0, k, r], 1), :], sem).wait()
        return carry

    lax.fori_loop(0, TM, drain, 0)


def _dispatch(pos, h, n_rows):
    bsz, ls, d = h.shape
    zeros = jnp.zeros((n_rows, d), F32)
    return pl.pallas_call(
        _dispatch_kernel,
        grid=(bsz, ls // TM),
        in_specs=[pl.BlockSpec((1, TOP_K, TM), lambda b, j: (b, 0, j), memory_space=pltpu.SMEM),
                  pl.BlockSpec((1, TM, d), lambda b, j: (b, j, 0)),
                  pl.BlockSpec(memory_space=pl.ANY)],
        out_specs=pl.BlockSpec(memory_space=pl.ANY),
        out_shape=jax.ShapeDtypeStruct((n_rows, d), F32),
        scratch_shapes=[pltpu.SemaphoreType.DMA(())],
        input_output_aliases={2: 0},
        compiler_params=_cp(("arbitrary", "arbitrary")),
        name="dispatch",
    )(pos, h, zeros)


def _expert_kernel(te_ref, nu_ref, hs_ref, wup_ref, bup_ref, wdn_ref, bdn_ref, y_ref):
    del te_ref
    i = pl.program_id(0)

    @pl.when(i < nu_ref[0])
    def _():
        h = hs_ref[...].astype(BF16)
        ff = wdn_ref.shape[1]
        y = jnp.zeros(y_ref.shape, F32) + bdn_ref[0]
        fc = min(FF_CHUNK, ff)
        for c in range(0, ff, fc):
            glu = jnp.dot(h, wup_ref[0, :, c:c + fc], preferred_element_type=F32) + bup_ref[0, :, c:c + fc]
            lin = (jnp.dot(h, wup_ref[0, :, ff + c:ff + c + fc], preferred_element_type=F32)
                   + bup_ref[0, :, ff + c:ff + c + fc])
            glu = jnp.minimum(glu, SWIGLU_LIMIT)
            lin = jnp.clip(lin, -SWIGLU_LIMIT, SWIGLU_LIMIT)
            act = glu * jax.nn.sigmoid(SWIGLU_ALPHA * glu) * (lin + 1.0)
            y = y + jnp.dot(act.astype(BF16), wdn_ref[0, c:c + fc, :], preferred_element_type=F32)
        y_ref[...] = y

    @pl.when(i >= nu_ref[0])
    def _():
        y_ref[...] = jnp.zeros_like(y_ref)


def _experts(tile_expert, n_used, hs, wup, bup, wdn, bdn):
    n_rows, d = hs.shape
    n_tiles = n_rows // EXPERT_TILE
    ff = wdn.shape[1]
    grid_spec = pltpu.PrefetchScalarGridSpec(
        num_scalar_prefetch=2,
        grid=(n_tiles,),
        in_specs=[pl.BlockSpec((EXPERT_TILE, d), lambda i, te, nu: (i, 0)),
                  pl.BlockSpec((1, d, 2 * ff), lambda i, te, nu: (te[i], 0, 0)),
                  pl.BlockSpec((1, 1, 2 * ff), lambda i, te, nu: (te[i], 0, 0)),
                  pl.BlockSpec((1, ff, d), lambda i, te, nu: (te[i], 0, 0)),
                  pl.BlockSpec((1, 1, d), lambda i, te, nu: (te[i], 0, 0))],
        out_specs=pl.BlockSpec((EXPERT_TILE, d), lambda i, te, nu: (i, 0)),
    )
    return pl.pallas_call(
        _expert_kernel,
        grid_spec=grid_spec,
        out_shape=jax.ShapeDtypeStruct((n_rows, d), F32),
        compiler_params=_cp(("arbitrary",)),
        name="experts",
    )(tile_expert, n_used, hs, wup, bup, wdn, bdn)


def _combine_kernel(pos_ref, x_ref, gate_ref, mod_ref, fw_ref, y_hbm, xo_ref, ybuf, sem, *, n_batch, d, blk_off, final):
    b, j = pl.program_id(0), pl.program_id(1) + blk_off

    def issue(r, carry):
        for k in range(TOP_K):
            _row_copy(y_hbm.at[pl.ds(pos_ref[0, k, r], 1), :], ybuf.at[k, pl.ds(r, 1), :], sem).start()
        return carry

    lax.fori_loop(0, TM, issue, 0)

    def drain(r, carry):
        for k in range(TOP_K):
            _row_copy(y_hbm.at[pl.ds(pos_ref[0, k, r], 1), :], ybuf.at[k, pl.ds(r, 1), :], sem).wait()
        return carry

    lax.fori_loop(0, TM, drain, 0)
    (g2,) = _mod_rows(mod_ref, b, j, n_batch, d, 5, 1)
    gate = gate_ref[0]
    acc = gate[:, 0:1] * ybuf[0]
    for k in range(1, TOP_K):
        acc = acc + gate[:, k:k + 1] * ybuf[k]
    xo = x_ref[0] + g2 * acc
    if final:
        xo = xo * lax.rsqrt(jnp.mean(xo * xo, axis=-1, keepdims=True) + EPS) * fw_ref[...]
    xo_ref[0] = xo


def _combine(pos, x, gate, mod, fw, y, n_batch, blk_off, final):
    bsz, l, d = x.shape
    n_blk = l // TM - blk_off
    ls = n_blk * TM
    return pl.pallas_call(
        functools.partial(_combine_kernel, n_batch=n_batch, d=d, blk_off=blk_off, final=final),
        grid=(bsz, n_blk),
        in_specs=[pl.BlockSpec((1, TOP_K, TM), lambda b, j: (b, 0, j), memory_space=pltpu.SMEM),
                  pl.BlockSpec((1, TM, d), lambda b, j: (b, j + blk_off, 0)),
                  pl.BlockSpec((1, TM, TOP_K), lambda b, j: (b, j, 0)),
                  _const_spec(mod.shape), _const_spec(fw.shape),
                  pl.BlockSpec(memory_space=pl.ANY)],
        out_specs=pl.BlockSpec((1, TM, d), lambda b, j: (b, j, 0)),
        out_shape=jax.ShapeDtypeStruct((bsz, ls, d), F32),
        scratch_shapes=[pltpu.VMEM((TOP_K, TM, d), F32), pltpu.SemaphoreType.DMA(())],
        compiler_params=_cp(("arbitrary", "arbitrary")),
        name="combine",
    )(pos, x, gate, mod, fw, y)


def _route_plan(idx, n_exp):
    shape = idx.shape
    e = idx.reshape(-1)
    onehot = (e[:, None] == jnp.arange(n_exp, dtype=jnp.int32)[None, :]).astype(jnp.int32)
    cs = jnp.cumsum(onehot, axis=0)
    rank = jnp.sum(onehot * cs, axis=1) - 1
    counts = cs[-1]
    padded = ((counts + EXPERT_TILE - 1) // EXPERT_TILE) * EXPERT_TILE
    ends = jnp.cumsum(padded)
    starts = ends - padded
    pos = (starts[e] + rank).reshape(shape).astype(jnp.int32)
    n_rows = e.shape[0] + n_exp * EXPERT_TILE
    n_tiles = n_rows // EXPERT_TILE
    n_used = (ends[-1] // EXPERT_TILE).astype(jnp.int32)
    tile_start = jnp.minimum(jnp.arange(n_tiles, dtype=jnp.int32), n_used - 1) * EXPERT_TILE
    tile_expert = jnp.minimum(jnp.searchsorted(ends, tile_start, side="right"), n_exp - 1).astype(jnp.int32)
    return pos, tile_expert, n_used.reshape(1), n_rows


def _moe(x, mod, nw, rw, rb, wup, bup, wdn, bdn, fw, n_batch, blk_off, final):
    n_exp = rw.shape[1]
    d = x.shape[-1]
    h, idx, gate = _route(x, mod, nw, rw.T, rb.reshape(n_exp, 1), n_batch, blk_off)
    pos, tile_expert, n_used, n_rows = _route_plan(idx, n_exp)
    hs = _dispatch(pos, h, n_rows)
    y = _experts(tile_expert, n_used, hs, wup.astype(BF16), bup.reshape(n_exp, 1, -1),
                 wdn.astype(BF16), bdn.reshape(n_exp, 1, d))
    return _combine(pos, x, jnp.transpose(gate, (0, 2, 1)), mod, fw, y, n_batch, blk_off, final)


KEY_BITS = 19


def _sorted_plan(idx, n_exp):
    bsz, top_k, ls = idx.shape
    n_tok = bsz * ls
    e = jnp.transpose(idx, (1, 0, 2)).reshape(-1)
    n_pair = e.shape[0]
    assert n_pair < (1 << KEY_BITS)
    keys = jnp.sort(e * (1 << KEY_BITS) + jnp.arange(n_pair, dtype=jnp.int32))
    pair_sorted = keys & ((1 << KEY_BITS) - 1)
    bounds = jnp.searchsorted(keys, jnp.arange(n_exp + 1, dtype=jnp.int32) * (1 << KEY_BITS)).astype(jnp.int32)
    counts = bounds[1:] - bounds[:-1]
    padded = ((counts + EXPERT_TILE - 1) // EXPERT_TILE) * EXPERT_TILE
    ends = jnp.cumsum(padded)
    starts = ends - padded
    n_rows = n_pair + n_exp * EXPERT_TILE
    n_tiles = n_rows // EXPERT_TILE
    n_used = (ends[-1] // EXPERT_TILE).astype(jnp.int32)
    tile_start = jnp.minimum(jnp.arange(n_tiles, dtype=jnp.int32), n_used - 1) * EXPERT_TILE
    tile_expert = jnp.minimum(jnp.searchsorted(ends, tile_start, side="right"), n_exp - 1).astype(jnp.int32)
    row = jnp.arange(n_rows, dtype=jnp.int32)
    row_e = tile_expert[row // EXPERT_TILE]
    off = row - starts[row_e]
    valid = jnp.logical_and(off < counts[row_e], row // EXPERT_TILE < n_used)
    pair = pair_sorted[jnp.clip(bounds[:-1][row_e] + off, 0, n_pair - 1)]
    dst = jnp.where(valid, pair, n_pair + row % EXPERT_TILE).astype(jnp.int32)
    src = jnp.where(valid, pair % n_tok, 0).astype(jnp.int32)
    shape = (n_tiles, 1, EXPERT_TILE)
    return src.reshape(shape), dst.reshape(shape), tile_expert, n_used.reshape(1)


def _fused_expert_kernel(te_ref, nu_ref, src0_ref, srcn_ref, dstp_ref, dstc_ref, h_hbm,
                         wup_ref, bup_ref, wdn_ref, bdn_ref, ys_hbm,
                         hbuf0, hbuf1, ybuf0, ybuf1, sem_g, sem_s):
    del te_ref, nu_ref
    i = pl.program_id(0)
    last = pl.num_programs(0) - 1
    n_pair = ys_hbm.shape[0] - EXPERT_TILE
    ff = wdn_ref.shape[1]
    fc = min(FF_CHUNK, ff)

    def gather_row(idx_ref, r, hbuf, sem):
        return pltpu.make_async_copy(h_hbm.at[pl.ds(idx_ref[0, 0, r], 1), :], hbuf.at[pl.ds(r, 1), :], sem)

    def scatter_row(row, r, ybuf, sem):
        return pltpu.make_async_copy(ybuf.at[pl.ds(r, 1), :], ys_hbm.at[pl.ds(row, 1), :], sem)

    def wait_all(buf, sem):
        pltpu.make_async_copy(buf, buf, sem).wait()

    def step(parity, hbuf_cur, hbuf_nxt, ybuf_cur, ybuf_prv):
        sg_cur, sg_nxt = sem_g.at[parity], sem_g.at[1 - parity]
        ss_cur, ss_prv = sem_s.at[parity], sem_s.at[1 - parity]
        wait_all(hbuf_cur, sg_cur)
        if parity == 1:
            wait_all(ybuf_cur, ss_cur)
        else:
            @pl.when(i > 0)
            def _():
                wait_all(ybuf_cur, ss_cur)

        dmas = []
        for r in range(EXPERT_TILE):
            dmas.append(("g", r))
            dmas.append(("s", r))
        n_piece = 3 * (ff // fc)
        per = -(-len(dmas) // n_piece)

        def issue(piece):
            for kind, r in dmas[piece * per:(piece + 1) * per]:
                if kind == "g":
                    gather_row(srcn_ref, r, hbuf_nxt, sg_nxt).start()
                else:
                    row = dstp_ref[0, 0, r]
                    if parity == 0:
                        row = jnp.where(i == 0, n_pair + r, row)
                    scatter_row(row, r, ybuf_prv, ss_prv).start()

        h = hbuf_cur[...].astype(BF16)
        y = jnp.zeros(ybuf_cur.shape, F32) + bdn_ref[0]
        piece = 0
        for c in range(0, ff, fc):
            glu = jnp.dot(h, wup_ref[0, :, c:c + fc], preferred_element_type=F32) + bup_ref[0, :, c:c + fc]
            issue(piece); piece += 1
            lin = (jnp.dot(h, wup_ref[0, :, ff + c:ff + c + fc], preferred_element_type=F32)
                   + bup_ref[0, :, ff + c:ff + c + fc])
            issue(piece); piece += 1
            glu = jnp.minimum(glu, SWIGLU_LIMIT)
            lin = jnp.clip(lin, -SWIGLU_LIMIT, SWIGLU_LIMIT)
            act = glu * jax.nn.sigmoid(SWIGLU_ALPHA * glu) * (lin + 1.0)
            y = y + jnp.dot(act.astype(BF16), wdn_ref[0, c:c + fc, :], preferred_element_type=F32)
            issue(piece); piece += 1
        ybuf_cur[...] = y

        @pl.when(i == last)
        def _():
            def tail(r, carry):
                scatter_row(dstc_ref[0, 0, r], r, ybuf_cur, ss_cur).start()
                return carry
            lax.fori_loop(0, EXPERT_TILE, tail, 0)
            wait_all(ybuf_cur, ss_cur)
            wait_all(ybuf_prv, ss_prv)
            wait_all(hbuf_nxt, sg_nxt)

    @pl.when(i == 0)
    def _():
        ybuf1[...] = jnp.zeros_like(ybuf1)

        def head(r, carry):
            gather_row(src0_ref, r, hbuf0, sem_g.at[0]).start()
            return carry
        lax.fori_loop(0, EXPERT_TILE, head, 0)

    @pl.when(i % 2 == 0)
    def _():
        step(0, hbuf0, hbuf1, ybuf0, ybuf1)

    @pl.when(i % 2 == 1)
    def _():
        step(1, hbuf1, hbuf0, ybuf1, ybuf0)


def _experts_fused(src, dst, tile_expert, n_used, h, wup, bup, wdn, bdn):
    n_tiles = src.shape[0]
    n_tok, d = h.shape
    ff = wdn.shape[1]
    n_slot_rows = (n_tiles * EXPERT_TILE - wup.shape[0] * EXPERT_TILE) + EXPERT_TILE
    idx_spec = lambda f: pl.BlockSpec((1, 1, EXPERT_TILE), f, memory_space=pltpu.SMEM)
    grid_spec = pltpu.PrefetchScalarGridSpec(
        num_scalar_prefetch=2,
        grid=(n_tiles,),
        in_specs=[idx_spec(lambda i, te, nu: (0, 0, 0)),
                  idx_spec(lambda i, te, nu: (jnp.minimum(i + 1, n_tiles - 1), 0, 0)),
                  idx_spec(lambda i, te, nu: (jnp.maximum(i - 1, 0), 0, 0)),
                  idx_spec(lambda i, te, nu: (i, 0, 0)),
                  pl.BlockSpec(memory_space=pl.ANY),
                  pl.BlockSpec((1, d, 2 * ff), lambda i, te, nu: (te[i], 0, 0)),
                  pl.BlockSpec((1, 1, 2 * ff), lambda i, te, nu: (te[i], 0, 0)),
                  pl.BlockSpec((1, ff, d), lambda i, te, nu: (te[i], 0, 0)),
                  pl.BlockSpec((1, 1, d), lambda i, te, nu: (te[i], 0, 0))],
        out_specs=pl.BlockSpec(memory_space=pl.ANY),
        scratch_shapes=[pltpu.VMEM((EXPERT_TILE, d), F32)] * 4 + [pltpu.SemaphoreType.DMA((2,))] * 2,
    )
    return pl.pallas_call(
        _fused_expert_kernel,
        grid_spec=grid_spec,
        out_shape=jax.ShapeDtypeStruct((n_slot_rows, d), F32),
        compiler_params=_cp(("arbitrary",)),
        name="experts_fused",
    )(tile_expert, n_used, src, src, dst, dst, h, wup, bup, wdn, bdn)


def _sum_kernel(x_ref, gate_ref, mod_ref, fw_ref, *rest, n_batch, d, blk_off, final):
    y_refs, xo_ref = rest[:TOP_K], rest[TOP_K]
    b, j = pl.program_id(0), pl.program_id(1) + blk_off
    (g2,) = _mod_rows(mod_ref, b, j, n_batch, d, 5, 1)
    gate = gate_ref[0]
    acc = gate[:, 0:1] * y_refs[0][...]
    for k in range(1, TOP_K):
        acc = acc + gate[:, k:k + 1] * y_refs[k][...]
    xo = x_ref[0] + g2 * acc
    if final:
        xo = xo * lax.rsqrt(jnp.mean(xo * xo, axis=-1, keepdims=True) + EPS) * fw_ref[...]
    xo_ref[0] = xo


def _combine_slots(x, gate, mod, fw, ys, n_batch, blk_off, final):
    bsz, l, d = x.shape
    n_blk = l // TM - blk_off
    ls = n_blk * TM
    per_choice = bsz * n_blk

    def slot(k):
        return pl.BlockSpec((TM, d), lambda b, j: (k * per_choice + b * n_blk + j, 0))

    return pl.pallas_call(
        functools.partial(_sum_kernel, n_batch=n_batch, d=d, blk_off=blk_off, final=final),
        grid=(bsz, n_blk),
        in_specs=[pl.BlockSpec((1, TM, d), lambda b, j: (b, j + blk_off, 0)),
                  pl.BlockSpec((1, TM, TOP_K), lambda b, j: (b, j, 0)),
                  _const_spec(mod.shape), _const_spec(fw.shape)] + [slot(k) for k in range(TOP_K)],
        out_specs=pl.BlockSpec((1, TM, d), lambda b, j: (b, j, 0)),
        out_shape=jax.ShapeDtypeStruct((bsz, ls, d), F32),
        compiler_params=_cp(("parallel", "arbitrary")),
        name="combine_slots",
    )(x, gate, mod, fw, *([ys] * TOP_K))


def _moe_fused(x, mod, nw, rw, rb, wup, bup, wdn, bdn, fw, n_batch, blk_off, final):
    n_exp = rw.shape[1]
    d = x.shape[-1]
    h, idx, gate = _route(x, mod, nw, rw.T, rb.reshape(n_exp, 1), n_batch, blk_off)
    src, dst, tile_expert, n_used = _sorted_plan(idx, n_exp)
    ys = _experts_fused(src, dst, tile_expert, n_used, h.reshape(-1, d), wup.astype(BF16), bup.reshape(n_exp, 1, -1),
                        wdn.astype(BF16), bdn.reshape(n_exp, 1, d))
    return _combine_slots(x, jnp.transpose(gate, (0, 2, 1)), mod, fw, ys, n_batch, blk_off, final)


def _pad_heads(w, n_heads, width):
    lead = w.shape[:-1]
    dh = w.shape[-1] // n_heads
    w = w.reshape(lead + (n_heads, dh))
    w = jnp.pad(w, [(0, 0)] * len(lead) + [(0, 0), (0, width - dh)])
    return w.reshape(lead + (n_heads * width,))


def _layer0_weights(w_in, w2_f, b2_f, w2_b, b2_b):
    d = w_in.shape[0]
    sizes = (A_HEADS * A_DK, A_HEADS * A_DK, A_HEADS * A_DV, A_RANK, A_RANK, A_HEADS * A_DV,
             B_HEADS * B_DK, B_HEADS * B_DK, B_HEADS * B_DK, B_HEADS * B_DV, B_HEADS * B_DV)
    aq, ak, av, ar_f, ar_b, aog, bq, bz_f, bz_b, bi, bog = jnp.split(w_in, np.cumsum(sizes)[:-1].tolist(), axis=1)
    ar = jnp.concatenate([ar_f, ar_b, jnp.zeros((d, HEAD_W - 2 * A_RANK), w_in.dtype)], axis=1)
    w = jnp.concatenate([_pad_heads(aq, A_HEADS, HEAD_W), bq, av, bi, aog, bog,
                         _pad_heads(ak, A_HEADS, HEAD_W), bz_f, bz_b, ar], axis=1).astype(BF16)
    hw = A_HEADS * HEAD_W
    w2 = jnp.zeros((2, HEAD_W, hw), F32)
    w2 = w2.at[0, 0:A_RANK].set(_pad_heads(w2_f, A_HEADS, HEAD_W))
    w2 = w2.at[1, A_RANK:2 * A_RANK].set(_pad_heads(w2_b, A_HEADS, HEAD_W))
    b2 = jnp.stack([_pad_heads(b2_f, A_HEADS, HEAD_W), _pad_heads(b2_b, A_HEADS, HEAD_W)]).reshape(2, 1, hw)
    return w, w2.astype(BF16), b2


QKV_W = 2 * C_HEADS * C_DK + C_HEADS * C_DV
OG_W = C_HEADS * C_DV
DQ_W = D_HEADS * D_HD
DKV_W = D_KV_HEADS * HEAD_W


def _proj1_kernel(x_ref, mod_ref, nw_ref, w_ref, cos_ref, sin_ref,
                  qkv_ref, gb_ref, og_ref, dq_ref, dk_ref, dv_ref, *, n_batch, d):
    b, j = pl.program_id(0), pl.program_id(1)
    shift, scale = _mod_rows(mod_ref, b, j, n_batch, d, 0, 2)
    h = _norm_mod(x_ref[0], nw_ref[...], shift, scale).astype(BF16)

    def mm(c0, c1):
        return jnp.dot(h, w_ref[:, c0:c1], preferred_element_type=F32)

    c = 0
    qkv_ref[0] = mm(c, c + QKV_W); c += QKV_W
    gb_ref[0] = mm(c, c + HEAD_W); c += HEAD_W
    og_ref[0] = mm(c, c + OG_W); c += OG_W
    cos, sin = cos_ref[...], sin_ref[...]
    for s in range(DQ_W // HEAD_W):
        xs = mm(c + s * HEAD_W, c + (s + 1) * HEAD_W)
        xp = mm(c + DQ_W + s * HEAD_W, c + DQ_W + (s + 1) * HEAD_W)
        dq_ref[0, :, s * HEAD_W:(s + 1) * HEAD_W] = (xs * cos + xp * sin) * (D_HD ** -0.5)
    c += 2 * DQ_W
    for s in range(DKV_W // HEAD_W):
        xs = mm(c + s * HEAD_W, c + (s + 1) * HEAD_W)
        xp = mm(c + DKV_W + s * HEAD_W, c + DKV_W + (s + 1) * HEAD_W)
        dk_ref[0, :, s * HEAD_W:(s + 1) * HEAD_W] = xs * cos + xp * sin
    c += 2 * DKV_W
    dv_ref[0] = mm(c, c + DKV_W)


def _proj1(x, mod, nw, w, cos, sin, n_batch):
    bsz, l, d = x.shape
    widths = (QKV_W, HEAD_W, OG_W, DQ_W, DKV_W, DKV_W)
    return pl.pallas_call(
        functools.partial(_proj1_kernel, n_batch=n_batch, d=d),
        grid=(bsz, l // TM),
        in_specs=[pl.BlockSpec((1, TM, d), lambda b, j: (b, j, 0)),
                  _const_spec(mod.shape), _const_spec(nw.shape), _const_spec(w.shape),
                  pl.BlockSpec((TM, HEAD_W), lambda b, j: (j, 0)),
                  pl.BlockSpec((TM, HEAD_W), lambda b, j: (j, 0))],
        out_specs=[pl.BlockSpec((1, TM, wd), lambda b, j: (b, j, 0)) for wd in widths],
        out_shape=[jax.ShapeDtypeStruct((bsz, l, wd), F32) for wd in widths],
        compiler_params=_cp(("parallel", "arbitrary")),
        name="proj1",
    )(x, mod, nw, w, cos, sin)


HALO = 8


def _conv1_kernel(cur_ref, prev_ref, next_ref, cw_ref, gbr_ref, alog_ref, dtb_ref,
                  q_ref, k_ref, v_ref, gb_ref, xe_ref):
    j, n_blk = pl.program_id(1), pl.num_programs(1)
    prev_ok = j >= 2
    next_ok = jnp.logical_and(j >= 1, j < n_blk - 1)
    xe_ref[0:HALO, :] = jnp.where(prev_ok, prev_ref[0], 0.0)
    xe_ref[HALO:HALO + TM, :] = cur_ref[0]
    xe_ref[HALO + TM:2 * HALO + TM, :] = jnp.where(next_ok, next_ref[0], 0.0)
    pad = CONV_W // 2
    hw = C_HEADS * HEAD_W
    for g, o_ref in enumerate((q_ref, k_ref, v_ref)):
        cols = slice(g * hw, (g + 1) * hw)
        acc = cw_ref[0:1, cols] * xe_ref[HALO - pad:HALO - pad + TM, cols]
        for i in range(1, CONV_W):
            acc = acc + cw_ref[i:i + 1, cols] * xe_ref[HALO - pad + i:HALO - pad + i + TM, cols]
        y = _silu(acc)
        if g < 2:
            parts = []
            for hh in range(C_HEADS):
                yh = y[:, hh * HEAD_W:(hh + 1) * HEAD_W]
                yh = yh * lax.rsqrt(jnp.sum(yh * yh, axis=-1, keepdims=True) + EPS)
                parts.append(yh * (C_DK ** -0.5) if g == 0 else yh)
            y = jnp.concatenate(parts, axis=-1)
        o_ref[0] = y
    raw = gbr_ref[0]
    z = raw + dtb_ref[...]
    softplus = jnp.maximum(z, 0.0) + jnp.log1p(jnp.exp(-jnp.abs(z)))
    lane = lax.broadcasted_iota(jnp.int32, raw.shape, 1)
    gb_ref[0] = jnp.where(lane < 2 * C_HEADS, jax.nn.sigmoid(raw), -jnp.exp(alog_ref[...]) * softplus)


def _conv1(qkv, conv_w, gb_raw, alog, dtb):
    bsz, l, wd = qkv.shape
    n_blk = l // TM
    per = TM // HALO
    last = l // HALO - 1
    hw = C_HEADS * HEAD_W
    ob = pl.BlockSpec((1, TM, hw), lambda b, j: (b, j, 0))
    gbs = pl.BlockSpec((1, TM, HEAD_W), lambda b, j: (b, j, 0))
    return pl.pallas_call(
        _conv1_kernel,
        grid=(bsz, n_blk),
        in_specs=[pl.BlockSpec((1, TM, wd), lambda b, j: (b, j, 0)),
                  pl.BlockSpec((1, HALO, wd), lambda b, j: (b, jnp.maximum(j * per - 1, 0), 0)),
                  pl.BlockSpec((1, HALO, wd), lambda b, j: (b, jnp.minimum((j + 1) * per, last), 0)),
                  _const_spec(conv_w.shape), gbs, _const_spec(alog.shape), _const_spec(dtb.shape)],
        out_specs=[ob, ob, ob, gbs],
        out_shape=[jax.ShapeDtypeStruct((bsz, l, hw), F32)] * 3 + [jax.ShapeDtypeStruct((bsz, l, HEAD_W), F32)],
        scratch_shapes=[pltpu.VMEM((TM + 2 * HALO, wd), F32)],
        compiler_params=_cp(("parallel", "arbitrary")),
        name="conv1",
    )(qkv, qkv, qkv, conv_w, gb_raw, alog, dtb)


_MERGE = (16, 32, 64)
_BASE = 8


def _delta_masks():
    t = np.arange(CHUNK)[:, None]
    s = np.arange(CHUNK)[None, :]
    out = [s <= t, s < t, s >= t, s > t, (t // _BASE) == (s // _BASE)]
    for m2 in _MERGE:
        out.append(((t // m2) == (s // m2)) & ((t // (m2 // 2)) != (s // (m2 // 2))))
    return np.stack(out).astype(np.float32)


def _bdot(a, b):
    return jnp.dot(a.astype(BF16), b.astype(BF16), preferred_element_type=F32)


DELTA_HEADS = 4


def _delta_prepare(chains, dm):
    n = range(len(chains))
    qs, ks, vs, gs, betas, revs = zip(*chains)
    incl = [dm[2] if r else dm[0] for r in revs]
    strict = [dm[3] if r else dm[1] for r in revs]
    cum = [_cumsum_rows(jnp.broadcast_to(gs[i], qs[i].shape), revs[i]) for i in n]
    decay = [incl[i] * jnp.exp(jnp.where(incl[i] > 0, cum[i][:, 0:CHUNK] - cum[i].T[0:CHUNK, :], 0.0)) for i in n]
    kbeta = [ks[i] * betas[i] for i in n]
    kbf = [ks[i].astype(BF16) for i in n]
    a = [strict[i] * _dot_nt(kbeta[i].astype(BF16), kbf[i]) * decay[i] for i in n]
    eye = dm[0] * dm[2]
    n0 = [-(a[i] * dm[4]) for i in n]
    n2 = [_bdot(n0[i], n0[i]) for i in n]
    n4 = [_bdot(n2[i], n2[i]) for i in n]
    t = [eye + n0[i] for i in n]
    t = [t[i] + _bdot(t[i], n2[i]) for i in n]
    t = [t[i] + _bdot(t[i], n4[i]) for i in n]
    for li in range(len(_MERGE)):
        inner = [_bdot(a[i] * dm[5 + li], t[i]) for i in n]
        t = [t[i] - _bdot(t[i], inner[i]) for i in n]
    uw = [_bdot(t[i], jnp.concatenate([vs[i] * betas[i], kbeta[i] * jnp.exp(cum[i])], axis=1)) for i in n]
    scores = [(_dot_nt(qs[i].astype(BF16), kbf[i]) * decay[i]).astype(BF16) for i in n]
    edge = [cum[i][0:1] if revs[i] else cum[i][CHUNK - 1:CHUNK] for i in n]
    q_in = [(qs[i] * jnp.exp(cum[i])).astype(BF16) for i in n]
    k_out = [(ks[i] * jnp.exp(edge[i] - cum[i])).astype(BF16) for i in n]
    carry = [jnp.exp(edge[i]) for i in n]
    return [(uw[i][:, 0:HEAD_W], uw[i][:, HEAD_W:].astype(BF16), scores[i], q_in[i], k_out[i], carry[i]) for i in n]


def _delta_step(prep, states):
    n = range(len(prep))
    sb = [states[i].astype(BF16) for i in n]
    v_new = [prep[i][0] - jnp.dot(prep[i][1], sb[i], preferred_element_type=F32) for i in n]
    vb = [v_new[i].astype(BF16) for i in n]
    o = [jnp.dot(prep[i][3], sb[i], preferred_element_type=F32)
         + jnp.dot(prep[i][2], vb[i], preferred_element_type=F32) for i in n]
    new = [states[i] * prep[i][5] + _dot_tn(prep[i][4], vb[i]) for i in n]
    return o, new


def _lane_col(x, lane_idx):
    lane = lax.broadcasted_iota(jnp.int32, x.shape, 1)
    return jnp.sum(jnp.where(lane == lane_idx, x, 0.0), axis=-1, keepdims=True)


def _delta_kernel(dm_ref, qf_ref, kf_ref, vf_ref, gbf_ref, qb_ref, kb_ref, vb_ref, gbb_ref,
                  of_ref, ob_ref, st_ref):
    @pl.when(pl.program_id(2) == 0)
    def _():
        st_ref[...] = jnp.zeros_like(st_ref)

    dm = [dm_ref[i] for i in range(dm_ref.shape[0])]
    n_ch = TM // CHUNK
    gbf, gbb = gbf_ref[0], gbb_ref[0]
    chains = []
    cols = []
    for hh in range(DELTA_HEADS):
        h = pl.program_id(1) * DELTA_HEADS + hh
        cols.append((_lane_col(gbf, h), _lane_col(gbf, 2 * C_HEADS + h),
                     _lane_col(gbb, C_HEADS + h), _lane_col(gbb, 3 * C_HEADS + h)))
    for c in range(n_ch):
        rf = slice(c * CHUNK, (c + 1) * CHUNK)
        rb = slice((n_ch - 1 - c) * CHUNK, (n_ch - c) * CHUNK)
        for hh in range(DELTA_HEADS):
            lanes = slice(hh * HEAD_W, (hh + 1) * HEAD_W)
            beta_f, g_f, beta_b, g_b = cols[hh]
            chains.append((qf_ref[0, rf, lanes], kf_ref[0, rf, lanes], vf_ref[0, rf, lanes], g_f[rf], beta_f[rf], False))
            chains.append((qb_ref[0, rb, lanes], kb_ref[0, rb, lanes], vb_ref[0, rb, lanes], g_b[rb], beta_b[rb], True))
    prep = _delta_prepare(chains, dm)
    per = 2 * DELTA_HEADS
    states = [st_ref[i] for i in range(per)]
    for c in range(n_ch):
        rf = slice(c * CHUNK, (c + 1) * CHUNK)
        rb = slice((n_ch - 1 - c) * CHUNK, (n_ch - c) * CHUNK)
        outs, states = _delta_step(prep[c * per:(c + 1) * per], states)
        for hh in range(DELTA_HEADS):
            lanes = slice(hh * HEAD_W, (hh + 1) * HEAD_W)
            of_ref[0, rf, lanes] = outs[2 * hh]
            ob_ref[0, rb, lanes] = outs[2 * hh + 1]
    for i in range(per):
        st_ref[i] = states[i]


def _delta(dmasks, q, k, v, gb):
    bsz, l, wide = q.shape
    n_blk = l // TM
    bw = DELTA_HEADS * HEAD_W
    fwd = pl.BlockSpec((1, TM, bw), lambda b, h, i: (b, i, h))
    bwd = pl.BlockSpec((1, TM, bw), lambda b, h, i: (b, _bwd_block(i, n_blk), h))
    gf = pl.BlockSpec((1, TM, HEAD_W), lambda b, h, i: (b, i, 0))
    gbw = pl.BlockSpec((1, TM, HEAD_W), lambda b, h, i: (b, _bwd_block(i, n_blk), 0))
    return pl.pallas_call(
        _delta_kernel,
        grid=(bsz, wide // bw, n_blk),
        in_specs=[_const_spec(dmasks.shape), fwd, fwd, fwd, gf, bwd, bwd, bwd, gbw],
        out_specs=[fwd, bwd],
        out_shape=[jax.ShapeDtypeStruct((bsz, l, wide), F32)] * 2,
        scratch_shapes=[pltpu.VMEM((2 * DELTA_HEADS, HEAD_W, HEAD_W), F32)],
        compiler_params=_cp(("parallel", "parallel", "arbitrary")),
        name="delta",
    )(dmasks, q, k, v, gb, q, k, v, gb)


def _attn_kernel(q_ref, kp_ref, kc_ref, kn_ref, vp_ref, vc_ref, vn_ref, kx_ref, vx_ref, sink_ref, o_ref, *, t_len):
    i = pl.program_id(1)
    q = q_ref[0]
    kl = jnp.concatenate([kp_ref[0], kc_ref[0], kn_ref[0]], axis=0).astype(BF16)
    vl = jnp.concatenate([vp_ref[0], vc_ref[0], vn_ref[0]], axis=0).astype(BF16)
    kx, vx = kx_ref[0].astype(BF16), vx_ref[0].astype(BF16)
    qpos = lax.broadcasted_iota(jnp.int32, (WINDOW, 3 * WINDOW), 0)
    kpos = lax.broadcasted_iota(jnp.int32, (WINDOW, 3 * WINDOW), 1) - WINDOW
    k_abs = i * WINDOW + kpos
    valid = (jnp.abs(kpos - qpos) <= WINDOW) & (k_abs >= 0) & (k_abs < t_len)
    low = lax.broadcasted_iota(jnp.int32, (WINDOW, HEAD_W), 1) < D_HD
    group = D_HEADS // D_KV_HEADS
    heads = range(D_HEADS)
    cols = [slice((h // group) * HEAD_W, (h // group + 1) * HEAD_W) for h in heads]
    qm = [jnp.where(low if h % 2 == 0 else jnp.logical_not(low), q[:, (h // 2) * HEAD_W:(h // 2 + 1) * HEAD_W], 0.0)
          .astype(BF16) for h in heads]
    s_l = [jnp.where(valid, _dot_nt(qm[h], kl[:, cols[h]]), -jnp.inf) for h in heads]
    s_x = [_dot_nt(qm[h], kx[:, cols[h]]) for h in heads]
    sink = [sink_ref[:, h:h + 1] for h in heads]
    m = [jnp.maximum(jnp.maximum(jnp.max(s_l[h], axis=-1, keepdims=True), jnp.max(s_x[h], axis=-1, keepdims=True)),
                     sink[h]) for h in heads]
    p_l = [jnp.exp(s_l[h] - m[h]) for h in heads]
    p_x = [jnp.exp(s_x[h] - m[h]) for h in heads]
    den = [jnp.sum(p_l[h], axis=-1, keepdims=True) + jnp.sum(p_x[h], axis=-1, keepdims=True) + jnp.exp(sink[h] - m[h])
           for h in heads]
    o = [(jnp.dot(p_l[h].astype(BF16), vl[:, cols[h]], preferred_element_type=F32)
          + jnp.dot(p_x[h].astype(BF16), vx[:, cols[h]], preferred_element_type=F32)) / den[h] for h in heads]
    for p in range(D_HEADS // 2):
        o_ref[0, :, p * HEAD_W:(p + 1) * HEAD_W] = jnp.where(low, o[2 * p], o[2 * p + 1])


def _attn(dq, dk, dv, sinks, n_ctx):
    bsz, l, _ = dq.shape
    t_len = l - n_ctx
    nq = t_len // WINDOW
    off = n_ctx // WINDOW

    def kv(delta):
        return pl.BlockSpec((1, WINDOW, DKV_W), lambda b, i: (b, off + jnp.clip(i + delta, 0, nq - 1), 0))

    ctx = pl.BlockSpec((1, n_ctx, DKV_W), lambda b, i: (b, 0, 0))
    return pl.pallas_call(
        functools.partial(_attn_kernel, t_len=t_len),
        grid=(bsz, nq),
        in_specs=[pl.BlockSpec((1, WINDOW, DQ_W), lambda b, i: (b, off + i, 0)),
                  kv(-1), kv(0), kv(1), kv(-1), kv(0), kv(1), ctx, ctx, _const_spec(sinks.shape)],
        out_specs=pl.BlockSpec((1, WINDOW, DQ_W), lambda b, i: (b, i, 0)),
        out_shape=jax.ShapeDtypeStruct((bsz, t_len, DQ_W), F32),
        compiler_params=_cp(("parallel", "arbitrary")),
        name="attn",
    )(dq, dk, dk, dk, dv, dv, dv, dk, dv, sinks)


def _read1_kernel(x_ref, of_ref, ob_ref, og_ref, od_ref, nw_ref, wout_ref, mod_ref, xo_ref, *, n_batch, d):
    b, j = pl.program_id(0), pl.program_id(1) + 1
    (g1,) = _mod_rows(mod_ref, b, j, n_batch, d, 2, 1)
    o = of_ref[0] + ob_ref[0]
    parts = [_head_rms(o[:, hh * HEAD_W:(hh + 1) * HEAD_W], nw_ref[...]) for hh in range(C_HEADS)]
    y = jnp.concatenate(parts, axis=-1) * _silu(og_ref[0])
    y = jnp.concatenate([y, od_ref[0]], axis=-1)
    xo_ref[0] = x_ref[0] + g1 * jnp.dot(y.astype(BF16), wout_ref[...], preferred_element_type=F32)


def _read1(x, of, ob, og, od, nw, wout, mod, n_batch):
    bsz, l, d = x.shape
    lat = lambda wd: pl.BlockSpec((1, TM, wd), lambda b, j: (b, j + 1, 0))
    return pl.pallas_call(
        functools.partial(_read1_kernel, n_batch=n_batch, d=d),
        grid=(bsz, l // TM - 1),
        in_specs=[lat(d), lat(OG_W), lat(OG_W), lat(OG_W),
                  pl.BlockSpec((1, TM, DQ_W), lambda b, j: (b, j, 0)),
                  _const_spec(nw.shape), _const_spec(wout.shape), _const_spec(mod.shape)],
        out_specs=lat(d),
        out_shape=jax.ShapeDtypeStruct(x.shape, F32),
        input_output_aliases={0: 0},
        compiler_params=_cp(("parallel", "arbitrary")),
        name="read1",
    )(x, of, ob, og, od, nw, wout, mod)


def _swap_halves(w, n_heads):
    lead = w.shape[:-1]
    dh = w.shape[-1] // n_heads
    w = w.reshape(lead + (n_heads, 2, dh // 2))
    return w[..., ::-1, :].reshape(lead + (n_heads * dh,))


def _dup_heads(w, n_heads):
    lead = w.shape[:-1]
    dh = w.shape[-1] // n_heads
    w = w.reshape(lead + (n_heads, 1, dh))
    return jnp.concatenate([w, w], axis=-2).reshape(lead + (2 * n_heads * dh,))


def _layer1_weights(w_in):
    d = w_in.shape[0]
    sizes = (C_HEADS * C_DK, C_HEADS * C_DK, C_HEADS * C_DV, C_HEADS, C_HEADS, C_HEADS, C_HEADS,
             C_HEADS * C_DV, D_HEADS * D_HD, D_KV_HEADS * D_HD, D_KV_HEADS * D_HD)
    cq, ck, cv, bt_f, bt_b, a_f, a_b, og, dq, dk, dv = jnp.split(w_in, np.cumsum(sizes)[:-1].tolist(), axis=1)
    gates = jnp.concatenate([bt_f, bt_b, a_f, a_b, jnp.zeros((d, HEAD_W - 4 * C_HEADS), w_in.dtype)], axis=1)
    return jnp.concatenate([cq, ck, cv, gates, og, dq, _swap_halves(dq, D_HEADS),
                            _dup_heads(dk, D_KV_HEADS), _dup_heads(_swap_halves(dk, D_KV_HEADS), D_KV_HEADS),
                            _dup_heads(dv, D_KV_HEADS)], axis=1).astype(BF16)


def _rope_tables(n_ctx, t_len):
    rows = t_len // GRID_W
    row = jnp.repeat(jnp.arange(rows, dtype=F32), GRID_W)
    col = jnp.tile(jnp.arange(GRID_W, dtype=F32), rows)
    n_freq = D_HD // 4
    inv = ROPE_BASE ** (-jnp.arange(n_freq, dtype=F32) / n_freq)
    ang = jnp.concatenate([row[:, None] * inv, col[:, None] * inv], axis=-1)
    cos, sin = jnp.cos(ang), jnp.sin(ang)
    cos = jnp.concatenate([jnp.ones((n_ctx, D_HD // 2), F32), cos], axis=0)
    sin = jnp.concatenate([jnp.zeros((n_ctx, D_HD // 2), F32), sin], axis=0)
    return jnp.concatenate([cos] * 4, axis=1), jnp.concatenate([-sin, sin] * 2, axis=1)


def _mixer1(xx, mod, nw, w_in, w_out, conv_w, a_log_f, dt_bias_f, a_log_b, dt_bias_b, dn_norm_w, sinks, n_batch, n_ctx):
    l = xx.shape[1]
    cos, sin = _rope_tables(n_ctx, l - n_ctx)
    qkv, gb_raw, og, dq, dk, dv = _proj1(xx, mod, nw, _layer1_weights(w_in), cos, sin, n_batch)
    zero4 = jnp.zeros((C_HEADS,), F32)
    fill = jnp.zeros((HEAD_W - 4 * C_HEADS,), F32)
    alog = jnp.concatenate([zero4, zero4, a_log_f, a_log_b, fill]).reshape(1, HEAD_W)
    dtb = jnp.concatenate([zero4, zero4, dt_bias_f, dt_bias_b, fill]).reshape(1, HEAD_W)
    cq, ck, cv, gb = _conv1(qkv, conv_w, gb_raw, alog, dtb)
    of, ob = _delta(jnp.asarray(_delta_masks()), cq, ck, cv, gb)
    sink_row = jnp.concatenate([sinks, jnp.zeros((HEAD_W - D_HEADS,), F32)]).reshape(1, HEAD_W)
    od = _attn(dq, dk, dv, sink_row, n_ctx)
    return _read1(xx, of, ob, og, od, dn_norm_w.reshape(1, -1), w_out.astype(BF16), mod, n_batch)


def kernel(x, c, ctx, c_ctx, l0_ada_w, l0_ada_b, l0_norm_mix_w, l0_w_in, l0_w_out, l0_gla_w2_f, l0_gla_b_f, l0_gla_w2_b, l0_gla_b_b, l0_gla_norm_w, l0_hgrn_norm_w, hgrn_lb_logits, l0_norm_ffn_w, l0_router_w, l0_router_b, l0_w_up, l0_b_up, l0_w_down, l0_b_down, l1_ada_w, l1_ada_b, l1_norm_mix_w, l1_w_in, l1_w_out, l1_conv_w, l1_a_log_f, l1_dt_bias_f, l1_a_log_b, l1_dt_bias_b, l1_dn_norm_w, l1_sinks, l1_norm_ffn_w, l1_router_w, l1_router_b, l1_w_up, l1_b_up, l1_w_down, l1_b_down, final_norm_w):
    bsz, t, d = x.shape
    n_ctx = ctx.shape[1]
    assert n_ctx == TM and t % TM == 0
    xx = jnp.concatenate([ctx, x], axis=1)
    mod_rows = -(-(bsz + 1) // 8) * 8
    c_all = jnp.zeros((mod_rows, d), F32).at[:bsz].set(c).at[bsz].set(c_ctx)
    ones = jnp.ones((1, d), F32)

    mod0 = _ada_table(c_all, l0_ada_w, l0_ada_b)
    w0, w2, b2 = _layer0_weights(l0_w_in, l0_gla_w2_f, l0_gla_b_f, l0_gla_w2_b, l0_gla_b_b)
    q, kf, kb, v, lff, lfb, og = _proj0(xx, mod0, l0_norm_mix_w.reshape(1, d), w0, w2, b2, hgrn_lb_logits, bsz)
    of, ob = _scan0(jnp.asarray(_level_masks()), q, kf, kb, v, lff, lfb)
    xx = _read0(xx, of, ob, og, l0_gla_norm_w.reshape(1, -1), l0_hgrn_norm_w.reshape(1, -1), l0_w_out.astype(BF16), mod0, bsz)
    xx = _moe_fused(xx, mod0, l0_norm_ffn_w.reshape(1, d), l0_router_w, l0_router_b, l0_w_up, l0_b_up, l0_w_down, l0_b_down,
              ones, bsz, 0, False)

    mod1 = _ada_table(c_all, l1_ada_w, l1_ada_b)
    xx = _mixer1(xx, mod1, l1_norm_mix_w.reshape(1, d), l1_w_in, l1_w_out, l1_conv_w, l1_a_log_f, l1_dt_bias_f,
                 l1_a_log_b, l1_dt_bias_b, l1_dn_norm_w, l1_sinks, bsz, n_ctx)
    return _moe_fused(xx, mod1, l1_norm_ffn_w.reshape(1, d), l1_router_w, l1_router_b, l1_w_up, l1_b_up, l1_w_down, l1_b_down,
                final_norm_w.reshape(1, d), bsz, 1, True)
```

```python
import functools

import numpy as np
import jax
import jax.numpy as jnp
from jax import lax
from jax.experimental import pallas as pl
from jax.experimental.pallas import tpu as pltpu

F32 = jnp.float32
BF16 = jnp.bfloat16
EPS = 1e-6

CHUNK = 64
A_HEADS, A_DK, A_DV, A_RANK = 4, 64, 128, 16
GATE_TAU = 16.0
B_HEADS, B_DK, B_DV = 4, 128, 128
C_HEADS, C_DK, C_DV = 4, 128, 128
CONV_W = 5
D_HEADS, D_KV_HEADS, D_HD = 8, 2, 64
WINDOW = 128
GRID_W = 64
ROPE_BASE = 10000.0
TOP_K = 4
SWIGLU_LIMIT = 7.0
SWIGLU_ALPHA = 1.702

TM = 256
HEAD_W = 128
EXPERT_TILE = 512
FF_CHUNK = 512
VMEM_LIMIT = 48 * 1024 * 1024


def _cp(sem, vmem=VMEM_LIMIT):
    return pltpu.CompilerParams(dimension_semantics=sem, vmem_limit_bytes=vmem)


def _const_spec(shape):
    nd = len(shape)
    return pl.BlockSpec(shape, lambda *_: (0,) * nd)


def _silu(x):
    return x * jax.nn.sigmoid(x)


def _log_sigmoid(z):
    return jnp.minimum(z, 0.0) - jnp.log1p(jnp.exp(-jnp.abs(z)))


def _norm_mod(x, nw, shift, scale):
    y = x * lax.rsqrt(jnp.mean(x * x, axis=-1, keepdims=True) + EPS) * nw
    return y * (1.0 + scale) + shift


def _mod_rows(mod_ref, batch, blk, n_batch, d, first, count):
    row = jnp.where(blk == 0, n_batch, batch)
    return [mod_ref[pl.ds(row, 1), (first + i) * d:(first + i + 1) * d] for i in range(count)]


def _ada_kernel(c_ref, w_ref, b_ref, o_ref):
    s = _silu(c_ref[...])
    o_ref[...] = jnp.dot(s, w_ref[...], precision=lax.Precision.HIGHEST,
                         preferred_element_type=F32) + b_ref[...]


def _ada_table(c_all, w, b):
    rows, d = c_all.shape
    n = w.shape[1]
    bn = d
    return pl.pallas_call(
        _ada_kernel,
        grid=(n // bn,),
        in_specs=[pl.BlockSpec((rows, d), lambda j: (0, 0)),
                  pl.BlockSpec((d, bn), lambda j: (0, j)),
                  pl.BlockSpec((1, bn), lambda j: (0, j))],
        out_specs=pl.BlockSpec((rows, bn), lambda j: (0, j)),
        out_shape=jax.ShapeDtypeStruct((rows, n), F32),
        compiler_params=_cp(("arbitrary",)),
        name="ada_table",
    )(c_all, w, b.reshape(1, n))


def _proj0_kernel(x_ref, mod_ref, nw_ref, w_ref, w2_ref, b2_ref, lbl_ref,
                  q_ref, kf_ref, kb_ref, v_ref, lff_ref, lfb_ref, og_ref, *, n_batch, d):
    b, j = pl.program_id(0), pl.program_id(1)
    shift, scale = _mod_rows(mod_ref, b, j, n_batch, d, 0, 2)
    h = _norm_mod(x_ref[0], nw_ref[...], shift, scale).astype(BF16)

    def mm(c0, c1):
        return jnp.dot(h, w_ref[:, c0:c1], preferred_element_type=F32)

    hw = A_HEADS * HEAD_W
    q_ref[0, :, 0:hw] = mm(0, hw) * (A_DK ** -0.5)
    q_ref[0, :, hw:2 * hw] = mm(hw, 2 * hw)
    v_ref[0] = mm(2 * hw, 4 * hw)
    og_ref[0] = mm(4 * hw, 6 * hw)
    kg = mm(6 * hw, 7 * hw)
    kf_ref[0, :, 0:hw] = kg
    kb_ref[0, :, 0:hw] = kg
    lg = lbl_ref[...]
    e = jnp.exp(lg - jnp.max(lg, axis=0, keepdims=True))
    lb = e[0:1] / jnp.sum(e, axis=0, keepdims=True)
    log_lb, log_1m = jnp.log(lb), jnp.log1p(-lb)
    ar = mm(9 * hw, 9 * hw + HEAD_W).astype(BF16)
    for di, (k_ref, lf_ref) in enumerate(((kf_ref, lff_ref), (kb_ref, lfb_ref))):
        z = mm((7 + di) * hw, (8 + di) * hw)
        s1 = log_1m + _log_sigmoid(z)
        lf_ref[0, :, hw:2 * hw] = jnp.maximum(log_lb, s1) + jnp.log1p(jnp.exp(-jnp.abs(log_lb - s1)))
        k_ref[0, :, hw:2 * hw] = (1.0 - lb) * jax.nn.sigmoid(-z)
        za = jnp.dot(ar, w2_ref[di], preferred_element_type=F32) + b2_ref[di]
        lf_ref[0, :, 0:hw] = _log_sigmoid(za) * (1.0 / GATE_TAU)


def _proj0(x, mod, nw, w, w2, b2, lbl, n_batch):
    bsz, l, d = x.shape
    n_blk = l // TM
    wide = 2 * A_HEADS * HEAD_W
    blk = pl.BlockSpec((1, TM, wide), lambda b, j: (b, j, 0))
    return pl.pallas_call(
        functools.partial(_proj0_kernel, n_batch=n_batch, d=d),
        grid=(bsz, n_blk),
        in_specs=[pl.BlockSpec((1, TM, d), lambda b, j: (b, j, 0)),
                  _const_spec(mod.shape), _const_spec(nw.shape), _const_spec(w.shape),
                  _const_spec(w2.shape), _const_spec(b2.shape), _const_spec(lbl.shape)],
        out_specs=[blk] * 7,
        out_shape=[jax.ShapeDtypeStruct((bsz, l, wide), F32)] * 7,
        compiler_params=_cp(("parallel", "arbitrary")),
        name="proj0",
    )(x, mod, nw, w, w2, b2, lbl)


_LEVELS = (32, 16, 8, 4, 2, 1)


def _level_masks():
    t = np.arange(CHUNK)[:, None]
    s = np.arange(CHUNK)[None, :]
    out = np.zeros((2, len(_LEVELS) + 1, CHUNK, CHUNK), np.float32)
    for li, m in enumerate(_LEVELS):
        same = (t // (2 * m)) == (s // (2 * m))
        fwd = same & (t % (2 * m) >= m) & (s % (2 * m) < m)
        out[0, li] = fwd
        out[1, li] = fwd.T
    out[:, -1] = np.eye(CHUNK)
    return out


def _cumsum_rows(x, reverse):
    n = x.shape[0]
    r = lax.broadcasted_iota(jnp.int32, x.shape, 0)
    sh = 1
    while sh < n:
        if reverse:
            x = x + jnp.where(r < n - sh, pltpu.roll(x, n - sh, 0), 0.0)
        else:
            x = x + jnp.where(r >= sh, pltpu.roll(x, sh, 0), 0.0)
        sh *= 2
    return x


def _level_ref(cum, m, reverse):
    n = cum.shape[0]
    tgt = m if reverse else m - 1
    if 2 * m >= 8:
        parts = [jnp.broadcast_to(cum[g + tgt:g + tgt + 1, :], (2 * m, cum.shape[1]))
                 for g in range(0, n, 2 * m)]
        return parts[0] if len(parts) == 1 else jnp.concatenate(parts, axis=0)
    pos = lax.broadcasted_iota(jnp.int32, cum.shape, 0) % (2 * m)
    out = cum
    for p in range(2 * m):
        if p == tgt:
            continue
        shift = (p - tgt) % n
        out = jnp.where(pos == p, pltpu.roll(cum, shift, 0), out)
    return out


def _dot_nt(a, b):
    return lax.dot_general(a, b, (((1,), (1,)), ((), ())), preferred_element_type=F32)


def _dot_tn(a, b):
    return lax.dot_general(a, b, (((0,), (0,)), ((), ())), preferred_element_type=F32)


def _gated_prepare(chains, masks):
    n = range(len(chains))
    qs, ks, vs, lfs, revs = zip(*chains)
    cum = [_cumsum_rows(lfs[i], revs[i]) for i in n]
    qb = [qs[i].astype(BF16) for i in n]
    kb = [ks[i].astype(BF16) for i in n]
    vb = [vs[i].astype(BF16) for i in n]
    scores = [masks[revs[i]][len(_LEVELS)] * _dot_nt(qb[i], kb[i]) for i in n]
    for li, m in enumerate(_LEVELS):
        w = [jnp.exp(-jnp.abs(cum[i] - _level_ref(cum[i], m, revs[i]))).astype(BF16) for i in n]
        part = [_dot_nt(qb[i] * w[i], kb[i] * w[i]) for i in n]
        scores = [scores[i] + masks[revs[i]][li] * part[i] for i in n]
    intra = [jnp.dot(scores[i].astype(BF16), vb[i], preferred_element_type=F32) for i in n]
    q_in = [(qs[i] * jnp.exp(cum[i])).astype(BF16) for i in n]
    edge = [cum[i][0:1] if revs[i] else cum[i][CHUNK - 1:CHUNK] for i in n]
    update = [_dot_tn(vb[i], (ks[i] * jnp.exp(edge[i] - cum[i])).astype(BF16)) for i in n]
    carry = [jnp.exp(edge[i]) for i in n]
    return [(intra[i], q_in[i], update[i], carry[i]) for i in n]


def _scan0_kernel(m_ref, qf_ref, kf_ref, vf_ref, lff_ref, qb_ref, kb_ref, vb_ref, lfb_ref,
                  of_ref, ob_ref, stf_ref, stb_ref):
    @pl.when(pl.program_id(2) == 0)
    def _():
        stf_ref[...] = jnp.zeros_like(stf_ref)
        stb_ref[...] = jnp.zeros_like(stb_ref)

    masks = [[m_ref[r, i] for i in range(len(_LEVELS) + 1)] for r in range(2)]
    n_ch = TM // CHUNK
    fwd_rows = [slice(c * CHUNK, (c + 1) * CHUNK) for c in range(n_ch)]
    bwd_rows = [slice((n_ch - 1 - c) * CHUNK, (n_ch - c) * CHUNK) for c in range(n_ch)]
    chains = []
    for c in range(n_ch):
        rf, rb = fwd_rows[c], bwd_rows[c]
        chains.append((qf_ref[0, rf, :], kf_ref[0, rf, :], vf_ref[0, rf, :], lff_ref[0, rf, :], 0))
        chains.append((qb_ref[0, rb, :], kb_ref[0, rb, :], vb_ref[0, rb, :], lfb_ref[0, rb, :], 1))
    prep = _gated_prepare(chains, masks)
    states = [stf_ref[...], stb_ref[...]]
    for c in range(n_ch):
        for r, (o_ref, rows) in enumerate(((of_ref, fwd_rows[c]), (ob_ref, bwd_rows[c]))):
            intra, q_in, update, carry = prep[2 * c + r]
            o_ref[0, rows, :] = intra + _dot_nt(q_in, states[r].astype(BF16))
            states[r] = states[r] * carry + update
    stf_ref[...] = states[0]
    stb_ref[...] = states[1]


def _bwd_block(i, n_blk):
    return jnp.where(i == 0, 0, n_blk - i)


def _scan0(masks, q, kf, kb, v, lff, lfb):
    bsz, l, wide = q.shape
    n_blk = l // TM
    n_heads = wide // HEAD_W
    fwd = pl.BlockSpec((1, TM, HEAD_W), lambda b, h, i: (b, i, h))
    bwd = pl.BlockSpec((1, TM, HEAD_W), lambda b, h, i: (b, _bwd_block(i, n_blk), h))
    return pl.pallas_call(
        _scan0_kernel,
        grid=(bsz, n_heads, n_blk),
        in_specs=[_const_spec(masks.shape), fwd, fwd, fwd, fwd, bwd, bwd, bwd, bwd],
        out_specs=[fwd, bwd],
        out_shape=[jax.ShapeDtypeStruct((bsz, l, wide), F32)] * 2,
        scratch_shapes=[pltpu.VMEM((HEAD_W, HEAD_W), F32), pltpu.VMEM((HEAD_W, HEAD_W), F32)],
        compiler_params=_cp(("parallel", "parallel", "arbitrary")),
        name="scan0",
    )(masks, q, kf, v, lff, q, kb, v, lfb)


def _head_rms(o, w):
    return o * lax.rsqrt(jnp.mean(o * o, axis=-1, keepdims=True) + EPS) * w


def _read0_kernel(x_ref, of_ref, ob_ref, og_ref, nwa_ref, nwb_ref, wout_ref, mod_ref, xo_ref, *, n_batch, d):
    b, j = pl.program_id(0), pl.program_id(1)
    (g1,) = _mod_rows(mod_ref, b, j, n_batch, d, 2, 1)
    o = of_ref[0] + ob_ref[0]
    parts = []
    for hh in range(A_HEADS + B_HEADS):
        nw = nwa_ref[...] if hh < A_HEADS else nwb_ref[...]
        parts.append(_head_rms(o[:, hh * HEAD_W:(hh + 1) * HEAD_W], nw))
    y = jnp.concatenate(parts, axis=-1) * _silu(og_ref[0])
    yo = jnp.dot(y.astype(BF16), wout_ref[...], preferred_element_type=F32)
    xo_ref[0] = x_ref[0] + g1 * yo


def _read0(x, of, ob, og, nwa, nwb, wout, mod, n_batch):
    bsz, l, d = x.shape
    wide = of.shape[-1]
    xb = pl.BlockSpec((1, TM, d), lambda b, j: (b, j, 0))
    wb = pl.BlockSpec((1, TM, wide), lambda b, j: (b, j, 0))
    return pl.pallas_call(
        functools.partial(_read0_kernel, n_batch=n_batch, d=d),
        grid=(bsz, l // TM),
        in_specs=[xb, wb, wb, wb, _const_spec(nwa.shape), _const_spec(nwb.shape),
                  _const_spec(wout.shape), _const_spec(mod.shape)],
        out_specs=xb,
        out_shape=jax.ShapeDtypeStruct(x.shape, F32),
        compiler_params=_cp(("parallel", "arbitrary")),
        name="read0",
    )(x, of, ob, og, nwa, nwb, wout, mod)


def _route_kernel(x_ref, mod_ref, nw_ref, rwt_ref, rb_ref, h_ref, idx_ref, gate_ref, *, n_batch, d, blk_off):
    b, j = pl.program_id(0), pl.program_id(1) + blk_off
    shift, scale = _mod_rows(mod_ref, b, j, n_batch, d, 3, 2)
    h = _norm_mod(x_ref[0], nw_ref[...], shift, scale)
    h_ref[0] = h
    logits = lax.dot_general(rwt_ref[...], h, (((1,), (1,)), ((), ())), precision=lax.Precision.HIGHEST,
                             preferred_element_type=F32) + rb_ref[...]
    n_exp = logits.shape[0]
    rows = lax.broadcasted_iota(jnp.int32, logits.shape, 0)
    vals, idxs = [], []
    for _ in range(TOP_K):
        m = jnp.max(logits, axis=0, keepdims=True)
        i = jnp.min(jnp.where(logits == m, rows, n_exp), axis=0, keepdims=True)
        vals.append(m)
        idxs.append(i)
        logits = jnp.where(rows == i, -jnp.inf, logits)
    ex = [jnp.exp(v - vals[0]) for v in vals]
    tot = ex[0] + ex[1] + ex[2] + ex[3]
    idx_ref[0] = jnp.concatenate(idxs, axis=0)
    gate_ref[0] = jnp.concatenate([e / tot for e in ex], axis=0)


def _route(x, mod, nw, rwt, rb, n_batch, blk_off):
    bsz, l, d = x.shape
    n_blk = l // TM - blk_off
    ls = n_blk * TM
    n_exp = rwt.shape[0]
    return pl.pallas_call(
        functools.partial(_route_kernel, n_batch=n_batch, d=d, blk_off=blk_off),
        grid=(bsz, n_blk),
        in_specs=[pl.BlockSpec((1, TM, d), lambda b, j: (b, j + blk_off, 0)),
                  _const_spec(mod.shape), _const_spec(nw.shape), _const_spec(rwt.shape), _const_spec(rb.shape)],
        out_specs=[pl.BlockSpec((1, TM, d), lambda b, j: (b, j, 0)),
                   pl.BlockSpec((1, TOP_K, TM), lambda b, j: (b, 0, j)),
                   pl.BlockSpec((1, TOP_K, TM), lambda b, j: (b, 0, j))],
        out_shape=[jax.ShapeDtypeStruct((bsz, ls, d), F32),
                   jax.ShapeDtypeStruct((bsz, TOP_K, ls), jnp.int32),
                   jax.ShapeDtypeStruct((bsz, TOP_K, ls), F32)],
        compiler_params=_cp(("parallel", "arbitrary")),
        name="route",
    )(x, mod, nw, rwt, rb)


def _row_copy(src, dst, sem):
    return pltpu.make_async_copy(src, dst, sem)


def _dispatch_kernel(pos_ref, h_ref, hs_in_ref, hs_ref, sem):
    del hs_in_ref

    def issue(r, carry):
        for k in range(TOP_K):
            _row_copy(h_ref.at[0, pl.ds(r, 1), :], hs_ref.at[pl.ds(pos_ref[0, k, r], 1), :], sem).start()
        return carry

    lax.fori_loop(0, TM, issue, 0, unroll=8)
    for k in range(TOP_K):
        _row_copy(h_ref.at[0], hs_ref.at[pl.ds(0, TM), :], sem).wait()


def _dispatch(pos, h, n_rows):
    bsz, ls, d = h.shape
    zeros = jnp.zeros((n_rows, d), F32)
    return pl.pallas_call(
        _dispatch_kernel,
        grid=(bsz, ls // TM),
        in_specs=[pl.BlockSpec((1, TOP_K, TM), lambda b, j: (b, 0, j), memory_space=pltpu.SMEM),
                  pl.BlockSpec((1, TM, d), lambda b, j: (b, j, 0)),
                  pl.BlockSpec(memory_space=pl.ANY)],
        out_specs=pl.BlockSpec(memory_space=pl.ANY),
        out_shape=jax.ShapeDtypeStruct((n_rows, d), F32),
        scratch_shapes=[pltpu.SemaphoreType.DMA(())],
        input_output_aliases={2: 0},
        compiler_params=_cp(("arbitrary", "arbitrary")),
        name="dispatch",
    )(pos, h, zeros)


def _expert_kernel(te_ref, nu_ref, hs_ref, wup_ref, bup_ref, wdn_ref, bdn_ref, y_ref):
    del te_ref
    i = pl.program_id(0)

    @pl.when(i < nu_ref[0])
    def _():
        h = hs_ref[...].astype(BF16)
        ff = wdn_ref.shape[1]
        y = jnp.zeros(y_ref.shape, F32) + bdn_ref[0]
        fc = min(FF_CHUNK, ff)
        for c in range(0, ff, fc):
            glu = jnp.dot(h, wup_ref[0, :, c:c + fc], preferred_element_type=F32) + bup_ref[0, :, c:c + fc]
            lin = (jnp.dot(h, wup_ref[0, :, ff + c:ff + c + fc], preferred_element_type=F32)
                   + bup_ref[0, :, ff + c:ff + c + fc])
            glu = jnp.minimum(glu, SWIGLU_LIMIT)
            lin = jnp.clip(lin, -SWIGLU_LIMIT, SWIGLU_LIMIT)
            act = glu * jax.nn.sigmoid(SWIGLU_ALPHA * glu) * (lin + 1.0)
            y = y + jnp.dot(act.astype(BF16), wdn_ref[0, c:c + fc, :], preferred_element_type=F32)
        y_ref[...] = y

    @pl.when(i >= nu_ref[0])
    def _():
        y_ref[...] = jnp.zeros_like(y_ref)


def _experts(tile_expert, n_used, hs, wup, bup, wdn, bdn):
    n_rows, d = hs.shape
    n_tiles = n_rows // EXPERT_TILE
    ff = wdn.shape[1]
    grid_spec = pltpu.PrefetchScalarGridSpec(
        num_scalar_prefetch=2,
        grid=(n_tiles,),
        in_specs=[pl.BlockSpec((EXPERT_TILE, d), lambda i, te, nu: (i, 0)),
                  pl.BlockSpec((1, d, 2 * ff), lambda i, te, nu: (te[i], 0, 0)),
                  pl.BlockSpec((1, 1, 2 * ff), lambda i, te, nu: (te[i], 0, 0)),
                  pl.BlockSpec((1, ff, d), lambda i, te, nu: (te[i], 0, 0)),
                  pl.BlockSpec((1, 1, d), lambda i, te, nu: (te[i], 0, 0))],
        out_specs=pl.BlockSpec((EXPERT_TILE, d), lambda i, te, nu: (i, 0)),
    )
    return pl.pallas_call(
        _expert_kernel,
        grid_spec=grid_spec,
        out_shape=jax.ShapeDtypeStruct((n_rows, d), F32),
        compiler_params=_cp(("arbitrary",)),
        name="experts",
    )(tile_expert, n_used, hs, wup, bup, wdn, bdn)


def _combine_kernel(pos_ref, x_ref, gate_ref, mod_ref, fw_ref, y_hbm, xo_ref, ybuf, sem, *, n_batch, d, blk_off, final):
    b, j = pl.program_id(0), pl.program_id(1) + blk_off

    def issue(r, carry):
        for k in range(TOP_K):
            _row_copy(y_hbm.at[pl.ds(pos_ref[0, k, r], 1), :], ybuf.at[k, pl.ds(r, 1), :], sem).start()
        return carry

    lax.fori_loop(0, TM, issue, 0, unroll=8)
    for k in range(TOP_K):
        _row_copy(y_hbm.at[pl.ds(0, TM), :], ybuf.at[k], sem).wait()
    (g2,) = _mod_rows(mod_ref, b, j, n_batch, d, 5, 1)
    gate = gate_ref[0]
    acc = gate[:, 0:1] * ybuf[0]
    for k in range(1, TOP_K):
        acc = acc + gate[:, k:k + 1] * ybuf[k]
    xo = x_ref[0] + g2 * acc
    if final:
        xo = xo * lax.rsqrt(jnp.mean(xo * xo, axis=-1, keepdims=True) + EPS) * fw_ref[...]
    xo_ref[0] = xo


def _combine(pos, x, gate, mod, fw, y, n_batch, blk_off, final):
    bsz, l, d = x.shape
    n_blk = l // TM - blk_off
    ls = n_blk * TM
    return pl.pallas_call(
        functools.partial(_combine_kernel, n_batch=n_batch, d=d, blk_off=blk_off, final=final),
        grid=(bsz, n_blk),
        in_specs=[pl.BlockSpec((1, TOP_K, TM), lambda b, j: (b, 0, j), memory_space=pltpu.SMEM),
                  pl.BlockSpec((1, TM, d), lambda b, j: (b, j + blk_off, 0)),
                  pl.BlockSpec((1, TM, TOP_K), lambda b, j: (b, j, 0)),
                  _const_spec(mod.shape), _const_spec(fw.shape),
                  pl.BlockSpec(memory_space=pl.ANY)],
        out_specs=pl.BlockSpec((1, TM, d), lambda b, j: (b, j, 0)),
        out_shape=jax.ShapeDtypeStruct((bsz, ls, d), F32),
        scratch_shapes=[pltpu.VMEM((TOP_K, TM, d), F32), pltpu.SemaphoreType.DMA(())],
        compiler_params=_cp(("arbitrary", "arbitrary")),
        name="combine",
    )(pos, x, gate, mod, fw, y)


def _route_plan(idx, n_exp):
    shape = idx.shape
    e = idx.reshape(-1)
    onehot = (e[:, None] == jnp.arange(n_exp, dtype=jnp.int32)[None, :])
    grouped = onehot.reshape(-1, TM, n_exp).astype(BF16)
    tri = jnp.tril(jnp.ones((TM, TM), BF16))
    local = jnp.einsum('ts,gse->gte', tri, grouped, preferred_element_type=F32)
    group_total = local[:, -1, :]
    group_off = jnp.cumsum(group_total, axis=0) - group_total
    cs = (local + group_off[:, None, :]).reshape(-1, n_exp)
    rank = jnp.sum(jnp.where(onehot, cs, 0.0), axis=1).astype(jnp.int32) - 1
    counts = (group_off[-1] + group_total[-1]).astype(jnp.int32)
    padded = ((counts + EXPERT_TILE - 1) // EXPERT_TILE) * EXPERT_TILE
    ends = jnp.cumsum(padded)
    starts = ends - padded
    pos = (starts[e] + rank).reshape(shape).astype(jnp.int32)
    n_rows = e.shape[0] + n_exp * EXPERT_TILE
    n_tiles = n_rows // EXPERT_TILE
    n_used = (ends[-1] // EXPERT_TILE).astype(jnp.int32)
    tile_start = jnp.minimum(jnp.arange(n_tiles, dtype=jnp.int32), n_used - 1) * EXPERT_TILE
    tile_expert = jnp.minimum(jnp.searchsorted(ends, tile_start, side="right"), n_exp - 1).astype(jnp.int32)
    return pos, tile_expert, n_used.reshape(1), n_rows


def _moe(x, mod, nw, rw, rb, wup, bup, wdn, bdn, fw, n_batch, blk_off, final):
    n_exp = rw.shape[1]
    d = x.shape[-1]
    h, idx, gate = _route(x, mod, nw, rw.T, rb.reshape(n_exp, 1), n_batch, blk_off)
    pos, tile_expert, n_used, n_rows = _route_plan(idx, n_exp)
    hs = _dispatch(pos, h, n_rows)
    y = _experts(tile_expert, n_used, hs, wup.astype(BF16), bup.reshape(n_exp, 1, -1),
                 wdn.astype(BF16), bdn.reshape(n_exp, 1, d))
    return _combine(pos, x, jnp.transpose(gate, (0, 2, 1)), mod, fw, y, n_batch, blk_off, final)


def _pad_heads(w, n_heads, width):
    lead = w.shape[:-1]
    dh = w.shape[-1] // n_heads
    w = w.reshape(lead + (n_heads, dh))
    w = jnp.pad(w, [(0, 0)] * len(lead) + [(0, 0), (0, width - dh)])
    return w.reshape(lead + (n_heads * width,))


def _layer0_weights(w_in, w2_f, b2_f, w2_b, b2_b):
    d = w_in.shape[0]
    sizes = (A_HEADS * A_DK, A_HEADS * A_DK, A_HEADS * A_DV, A_RANK, A_RANK, A_HEADS * A_DV,
             B_HEADS * B_DK, B_HEADS * B_DK, B_HEADS * B_DK, B_HEADS * B_DV, B_HEADS * B_DV)
    aq, ak, av, ar_f, ar_b, aog, bq, bz_f, bz_b, bi, bog = jnp.split(w_in, np.cumsum(sizes)[:-1].tolist(), axis=1)
    ar = jnp.concatenate([ar_f, ar_b, jnp.zeros((d, HEAD_W - 2 * A_RANK), w_in.dtype)], axis=1)
    w = jnp.concatenate([_pad_heads(aq, A_HEADS, HEAD_W), bq, av, bi, aog, bog,
                         _pad_heads(ak, A_HEADS, HEAD_W), bz_f, bz_b, ar], axis=1).astype(BF16)
    hw = A_HEADS * HEAD_W
    w2 = jnp.zeros((2, HEAD_W, hw), F32)
    w2 = w2.at[0, 0:A_RANK].set(_pad_heads(w2_f, A_HEADS, HEAD_W))
    w2 = w2.at[1, A_RANK:2 * A_RANK].set(_pad_heads(w2_b, A_HEADS, HEAD_W))
    b2 = jnp.stack([_pad_heads(b2_f, A_HEADS, HEAD_W), _pad_heads(b2_b, A_HEADS, HEAD_W)]).reshape(2, 1, hw)
    return w, w2.astype(BF16), b2


QKV_W = 2 * C_HEADS * C_DK + C_HEADS * C_DV
OG_W = C_HEADS * C_DV
DQ_W = D_HEADS * D_HD
DKV_W = D_KV_HEADS * HEAD_W


def _proj1_kernel(x_ref, mod_ref, nw_ref, w_ref, cos_ref, sin_ref,
                  qkv_ref, gb_ref, og_ref, dq_ref, dk_ref, dv_ref, *, n_batch, d):
    b, j = pl.program_id(0), pl.program_id(1)
    shift, scale = _mod_rows(mod_ref, b, j, n_batch, d, 0, 2)
    h = _norm_mod(x_ref[0], nw_ref[...], shift, scale).astype(BF16)

    def mm(c0, c1):
        return jnp.dot(h, w_ref[:, c0:c1], preferred_element_type=F32)

    c = 0
    qkv_ref[0] = mm(c, c + QKV_W); c += QKV_W
    gb_ref[0] = mm(c, c + HEAD_W); c += HEAD_W
    og_ref[0] = mm(c, c + OG_W); c += OG_W
    cos, sin = cos_ref[...], sin_ref[...]
    for s in range(DQ_W // HEAD_W):
        xs = mm(c + s * HEAD_W, c + (s + 1) * HEAD_W)
        xp = mm(c + DQ_W + s * HEAD_W, c + DQ_W + (s + 1) * HEAD_W)
        dq_ref[0, :, s * HEAD_W:(s + 1) * HEAD_W] = (xs * cos + xp * sin) * (D_HD ** -0.5)
    c += 2 * DQ_W
    for s in range(DKV_W // HEAD_W):
        xs = mm(c + s * HEAD_W, c + (s + 1) * HEAD_W)
        xp = mm(c + DKV_W + s * HEAD_W, c + DKV_W + (s + 1) * HEAD_W)
        dk_ref[0, :, s * HEAD_W:(s + 1) * HEAD_W] = xs * cos + xp * sin
    c += 2 * DKV_W
    dv_ref[0] = mm(c, c + DKV_W)


def _proj1(x, mod, nw, w, cos, sin, n_batch):
    bsz, l, d = x.shape
    widths = (QKV_W, HEAD_W, OG_W, DQ_W, DKV_W, DKV_W)
    return pl.pallas_call(
        functools.partial(_proj1_kernel, n_batch=n_batch, d=d),
        grid=(bsz, l // TM),
        in_specs=[pl.BlockSpec((1, TM, d), lambda b, j: (b, j, 0)),
                  _const_spec(mod.shape), _const_spec(nw.shape), _const_spec(w.shape),
                  pl.BlockSpec((TM, HEAD_W), lambda b, j: (j, 0)),
                  pl.BlockSpec((TM, HEAD_W), lambda b, j: (j, 0))],
        out_specs=[pl.BlockSpec((1, TM, wd), lambda b, j: (b, j, 0)) for wd in widths],
        out_shape=[jax.ShapeDtypeStruct((bsz, l, wd), F32) for wd in widths],
        compiler_params=_cp(("parallel", "arbitrary")),
        name="proj1",
    )(x, mod, nw, w, cos, sin)


HALO = 8


def _conv1_kernel(cur_ref, prev_ref, next_ref, cw_ref, gbr_ref, alog_ref, dtb_ref,
                  q_ref, k_ref, v_ref, gb_ref, xe_ref):
    j, n_blk = pl.program_id(1), pl.num_programs(1)
    prev_ok = j >= 2
    next_ok = jnp.logical_and(j >= 1, j < n_blk - 1)
    xe_ref[0:HALO, :] = jnp.where(prev_ok, prev_ref[0], 0.0)
    xe_ref[HALO:HALO + TM, :] = cur_ref[0]
    xe_ref[HALO + TM:2 * HALO + TM, :] = jnp.where(next_ok, next_ref[0], 0.0)
    pad = CONV_W // 2
    hw = C_HEADS * HEAD_W
    for g, o_ref in enumerate((q_ref, k_ref, v_ref)):
        cols = slice(g * hw, (g + 1) * hw)
        acc = cw_ref[0:1, cols] * xe_ref[HALO - pad:HALO - pad + TM, cols]
        for i in range(1, CONV_W):
            acc = acc + cw_ref[i:i + 1, cols] * xe_ref[HALO - pad + i:HALO - pad + i + TM, cols]
        y = _silu(acc)
        if g < 2:
            parts = []
            for hh in range(C_HEADS):
                yh = y[:, hh * HEAD_W:(hh + 1) * HEAD_W]
                yh = yh * lax.rsqrt(jnp.sum(yh * yh, axis=-1, keepdims=True) + EPS)
                parts.append(yh * (C_DK ** -0.5) if g == 0 else yh)
            y = jnp.concatenate(parts, axis=-1)
        o_ref[0] = y
    raw = gbr_ref[0]
    z = raw + dtb_ref[...]
    softplus = jnp.maximum(z, 0.0) + jnp.log1p(jnp.exp(-jnp.abs(z)))
    lane = lax.broadcasted_iota(jnp.int32, raw.shape, 1)
    gb_ref[0] = jnp.where(lane < 2 * C_HEADS, jax.nn.sigmoid(raw), -jnp.exp(alog_ref[...]) * softplus)


def _conv1(qkv, conv_w, gb_raw, alog, dtb):
    bsz, l, wd = qkv.shape
    n_blk = l // TM
    per = TM // HALO
    last = l // HALO - 1
    hw = C_HEADS * HEAD_W
    ob = pl.BlockSpec((1, TM, hw), lambda b, j: (b, j, 0))
    gbs = pl.BlockSpec((1, TM, HEAD_W), lambda b, j: (b, j, 0))
    return pl.pallas_call(
        _conv1_kernel,
        grid=(bsz, n_blk),
        in_specs=[pl.BlockSpec((1, TM, wd), lambda b, j: (b, j, 0)),
                  pl.BlockSpec((1, HALO, wd), lambda b, j: (b, jnp.maximum(j * per - 1, 0), 0)),
                  pl.BlockSpec((1, HALO, wd), lambda b, j: (b, jnp.minimum((j + 1) * per, last), 0)),
                  _const_spec(conv_w.shape), gbs, _const_spec(alog.shape), _const_spec(dtb.shape)],
        out_specs=[ob, ob, ob, gbs],
        out_shape=[jax.ShapeDtypeStruct((bsz, l, hw), F32)] * 3 + [jax.ShapeDtypeStruct((bsz, l, HEAD_W), F32)],
        scratch_shapes=[pltpu.VMEM((TM + 2 * HALO, wd), F32)],
        compiler_params=_cp(("parallel", "arbitrary")),
        name="conv1",
    )(qkv, qkv, qkv, conv_w, gb_raw, alog, dtb)


_MERGE = (16, 32, 64)
_BASE = 8


def _delta_masks():
    t = np.arange(CHUNK)[:, None]
    s = np.arange(CHUNK)[None, :]
    out = [s <= t, s < t, s >= t, s > t, (t // _BASE) == (s // _BASE)]
    for m2 in _MERGE:
        out.append(((t // m2) == (s // m2)) & ((t // (m2 // 2)) != (s // (m2 // 2))))
    return np.stack(out).astype(np.float32)


def _bdot(a, b):
    return jnp.dot(a.astype(BF16), b.astype(BF16), preferred_element_type=F32)


DELTA_HEADS = 4


def _delta_prepare(chains, dm):
    n = range(len(chains))
    qs, ks, vs, gs, betas, revs = zip(*chains)
    incl = [dm[2] if r else dm[0] for r in revs]
    strict = [dm[3] if r else dm[1] for r in revs]
    cum = [_cumsum_rows(jnp.broadcast_to(gs[i], qs[i].shape), revs[i]) for i in n]
    decay = [incl[i] * jnp.exp(jnp.where(incl[i] > 0, cum[i][:, 0:CHUNK] - cum[i].T[0:CHUNK, :], 0.0)) for i in n]
    kbeta = [ks[i] * betas[i] for i in n]
    kbf = [ks[i].astype(BF16) for i in n]
    a = [strict[i] * _dot_nt(kbeta[i].astype(BF16), kbf[i]) * decay[i] for i in n]
    eye = dm[0] * dm[2]
    n0 = [-(a[i] * dm[4]) for i in n]
    n2 = [_bdot(n0[i], n0[i]) for i in n]
    n4 = [_bdot(n2[i], n2[i]) for i in n]
    t = [eye + n0[i] for i in n]
    t = [t[i] + _bdot(t[i], n2[i]) for i in n]
    t = [t[i] + _bdot(t[i], n4[i]) for i in n]
    for li in range(len(_MERGE)):
        inner = [_bdot(a[i] * dm[5 + li], t[i]) for i in n]
        t = [t[i] - _bdot(t[i], inner[i]) for i in n]
    uw = [_bdot(t[i], jnp.concatenate([vs[i] * betas[i], kbeta[i] * jnp.exp(cum[i])], axis=1)) for i in n]
    scores = [(_dot_nt(qs[i].astype(BF16), kbf[i]) * decay[i]).astype(BF16) for i in n]
    edge = [cum[i][0:1] if revs[i] else cum[i][CHUNK - 1:CHUNK] for i in n]
    q_in = [(qs[i] * jnp.exp(cum[i])).astype(BF16) for i in n]
    k_out = [(ks[i] * jnp.exp(edge[i] - cum[i])).astype(BF16) for i in n]
    carry = [jnp.exp(edge[i]) for i in n]
    return [(uw[i][:, 0:HEAD_W], uw[i][:, HEAD_W:].astype(BF16), scores[i], q_in[i], k_out[i], carry[i]) for i in n]


def _delta_step(prep, states):
    n = range(len(prep))
    sb = [states[i].astype(BF16) for i in n]
    v_new = [prep[i][0] - jnp.dot(prep[i][1], sb[i], preferred_element_type=F32) for i in n]
    vb = [v_new[i].astype(BF16) for i in n]
    o = [jnp.dot(prep[i][3], sb[i], preferred_element_type=F32)
         + jnp.dot(prep[i][2], vb[i], preferred_element_type=F32) for i in n]
    new = [states[i] * prep[i][5] + _dot_tn(prep[i][4], vb[i]) for i in n]
    return o, new


def _lane_col(x, lane_idx):
    lane = lax.broadcasted_iota(jnp.int32, x.shape, 1)
    return jnp.sum(jnp.where(lane == lane_idx, x, 0.0), axis=-1, keepdims=True)


def _delta_kernel(dm_ref, qf_ref, kf_ref, vf_ref, gbf_ref, qb_ref, kb_ref, vb_ref, gbb_ref,
                  of_ref, ob_ref, st_ref):
    @pl.when(pl.program_id(2) == 0)
    def _():
        st_ref[...] = jnp.zeros_like(st_ref)

    dm = [dm_ref[i] for i in range(dm_ref.shape[0])]
    n_ch = TM // CHUNK
    gbf, gbb = gbf_ref[0], gbb_ref[0]
    chains = []
    cols = []
    for hh in range(DELTA_HEADS):
        h = pl.program_id(1) * DELTA_HEADS + hh
        cols.append((_lane_col(gbf, h), _lane_col(gbf, 2 * C_HEADS + h),
                     _lane_col(gbb, C_HEADS + h), _lane_col(gbb, 3 * C_HEADS + h)))
    for c in range(n_ch):
        rf = slice(c * CHUNK, (c + 1) * CHUNK)
        rb = slice((n_ch - 1 - c) * CHUNK, (n_ch - c) * CHUNK)
        for hh in range(DELTA_HEADS):
            lanes = slice(hh * HEAD_W, (hh + 1) * HEAD_W)
            beta_f, g_f, beta_b, g_b = cols[hh]
            chains.append((qf_ref[0, rf, lanes], kf_ref[0, rf, lanes], vf_ref[0, rf, lanes], g_f[rf], beta_f[rf], False))
            chains.append((qb_ref[0, rb, lanes], kb_ref[0, rb, lanes], vb_ref[0, rb, lanes], g_b[rb], beta_b[rb], True))
    prep = _delta_prepare(chains, dm)
    per = 2 * DELTA_HEADS
    states = [st_ref[i] for i in range(per)]
    for c in range(n_ch):
        rf = slice(c * CHUNK, (c + 1) * CHUNK)
        rb = slice((n_ch - 1 - c) * CHUNK, (n_ch - c) * CHUNK)
        outs, states = _delta_step(prep[c * per:(c + 1) * per], states)
        for hh in range(DELTA_HEADS):
            lanes = slice(hh * HEAD_W, (hh + 1) * HEAD_W)
            of_ref[0, rf, lanes] = outs[2 * hh]
            ob_ref[0, rb, lanes] = outs[2 * hh + 1]
    for i in range(per):
        st_ref[i] = states[i]


def _delta(dmasks, q, k, v, gb):
    bsz, l, wide = q.shape
    n_blk = l // TM
    bw = DELTA_HEADS * HEAD_W
    fwd = pl.BlockSpec((1, TM, bw), lambda b, h, i: (b, i, h))
    bwd = pl.BlockSpec((1, TM, bw), lambda b, h, i: (b, _bwd_block(i, n_blk), h))
    gf = pl.BlockSpec((1, TM, HEAD_W), lambda b, h, i: (b, i, 0))
    gbw = pl.BlockSpec((1, TM, HEAD_W), lambda b, h, i: (b, _bwd_block(i, n_blk), 0))
    return pl.pallas_call(
        _delta_kernel,
        grid=(bsz, wide // bw, n_blk),
        in_specs=[_const_spec(dmasks.shape), fwd, fwd, fwd, gf, bwd, bwd, bwd, gbw],
        out_specs=[fwd, bwd],
        out_shape=[jax.ShapeDtypeStruct((bsz, l, wide), F32)] * 2,
        scratch_shapes=[pltpu.VMEM((2 * DELTA_HEADS, HEAD_W, HEAD_W), F32)],
        compiler_params=_cp(("parallel", "parallel", "arbitrary")),
        name="delta",
    )(dmasks, q, k, v, gb, q, k, v, gb)


def _attn_kernel(q_ref, kp_ref, kc_ref, kn_ref, vp_ref, vc_ref, vn_ref, kx_ref, vx_ref, sink_ref, o_ref, *, t_len):
    i = pl.program_id(1)
    q = q_ref[0]
    kl = jnp.concatenate([kp_ref[0], kc_ref[0], kn_ref[0]], axis=0).astype(BF16)
    vl = jnp.concatenate([vp_ref[0], vc_ref[0], vn_ref[0]], axis=0).astype(BF16)
    kx, vx = kx_ref[0].astype(BF16), vx_ref[0].astype(BF16)
    qpos = lax.broadcasted_iota(jnp.int32, (WINDOW, 3 * WINDOW), 0)
    kpos = lax.broadcasted_iota(jnp.int32, (WINDOW, 3 * WINDOW), 1) - WINDOW
    k_abs = i * WINDOW + kpos
    valid = (jnp.abs(kpos - qpos) <= WINDOW) & (k_abs >= 0) & (k_abs < t_len)
    low = lax.broadcasted_iota(jnp.int32, (WINDOW, HEAD_W), 1) < D_HD
    group = D_HEADS // D_KV_HEADS
    heads = range(D_HEADS)
    cols = [slice((h // group) * HEAD_W, (h // group + 1) * HEAD_W) for h in heads]
    qm = [jnp.where(low if h % 2 == 0 else jnp.logical_not(low), q[:, (h // 2) * HEAD_W:(h // 2 + 1) * HEAD_W], 0.0)
          .astype(BF16) for h in heads]
    s_l = [jnp.where(valid, _dot_nt(qm[h], kl[:, cols[h]]), -jnp.inf) for h in heads]
    s_x = [_dot_nt(qm[h], kx[:, cols[h]]) for h in heads]
    sink = [sink_ref[:, h:h + 1] for h in heads]
    m = [jnp.maximum(jnp.maximum(jnp.max(s_l[h], axis=-1, keepdims=True), jnp.max(s_x[h], axis=-1, keepdims=True)),
                     sink[h]) for h in heads]
    p_l = [jnp.exp(s_l[h] - m[h]) for h in heads]
    p_x = [jnp.exp(s_x[h] - m[h]) for h in heads]
    den = [jnp.sum(p_l[h], axis=-1, keepdims=True) + jnp.sum(p_x[h], axis=-1, keepdims=True) + jnp.exp(sink[h] - m[h])
           for h in heads]
    o = [(jnp.dot(p_l[h].astype(BF16), vl[:, cols[h]], preferred_element_type=F32)
          + jnp.dot(p_x[h].astype(BF16), vx[:, cols[h]], preferred_element_type=F32)) / den[h] for h in heads]
    for p in range(D_HEADS // 2):
        o_ref[0, :, p * HEAD_W:(p + 1) * HEAD_W] = jnp.where(low, o[2 * p], o[2 * p + 1])


def _attn(dq, dk, dv, sinks, n_ctx):
    bsz, l, _ = dq.shape
    t_len = l - n_ctx
    nq = t_len // WINDOW
    off = n_ctx // WINDOW

    def kv(delta):
        return pl.BlockSpec((1, WINDOW, DKV_W), lambda b, i: (b, off + jnp.clip(i + delta, 0, nq - 1), 0))

    ctx = pl.BlockSpec((1, n_ctx, DKV_W), lambda b, i: (b, 0, 0))
    return pl.pallas_call(
        functools.partial(_attn_kernel, t_len=t_len),
        grid=(bsz, nq),
        in_specs=[pl.BlockSpec((1, WINDOW, DQ_W), lambda b, i: (b, off + i, 0)),
                  kv(-1), kv(0), kv(1), kv(-1), kv(0), kv(1), ctx, ctx, _const_spec(sinks.shape)],
        out_specs=pl.BlockSpec((1, WINDOW, DQ_W), lambda b, i: (b, i, 0)),
        out_shape=jax.ShapeDtypeStruct((bsz, t_len, DQ_W), F32),
        compiler_params=_cp(("parallel", "arbitrary")),
        name="attn",
    )(dq, dk, dk, dk, dv, dv, dv, dk, dv, sinks)


def _read1_kernel(x_ref, of_ref, ob_ref, og_ref, od_ref, nw_ref, wout_ref, mod_ref, xo_ref, *, n_batch, d):
    b, j = pl.program_id(0), pl.program_id(1) + 1
    (g1,) = _mod_rows(mod_ref, b, j, n_batch, d, 2, 1)
    o = of_ref[0] + ob_ref[0]
    parts = [_head_rms(o[:, hh * HEAD_W:(hh + 1) * HEAD_W], nw_ref[...]) for hh in range(C_HEADS)]
    y = jnp.concatenate(parts, axis=-1) * _silu(og_ref[0])
    y = jnp.concatenate([y, od_ref[0]], axis=-1)
    xo_ref[0] = x_ref[0] + g1 * jnp.dot(y.astype(BF16), wout_ref[...], preferred_element_type=F32)


def _read1(x, of, ob, og, od, nw, wout, mod, n_batch):
    bsz, l, d = x.shape
    lat = lambda wd: pl.BlockSpec((1, TM, wd), lambda b, j: (b, j + 1, 0))
    return pl.pallas_call(
        functools.partial(_read1_kernel, n_batch=n_batch, d=d),
        grid=(bsz, l // TM - 1),
        in_specs=[lat(d), lat(OG_W), lat(OG_W), lat(OG_W),
                  pl.BlockSpec((1, TM, DQ_W), lambda b, j: (b, j, 0)),
                  _const_spec(nw.shape), _const_spec(wout.shape), _const_spec(mod.shape)],
        out_specs=lat(d),
        out_shape=jax.ShapeDtypeStruct(x.shape, F32),
        input_output_aliases={0: 0},
        compiler_params=_cp(("parallel", "arbitrary")),
        name="read1",
    )(x, of, ob, og, od, nw, wout, mod)


def _swap_halves(w, n_heads):
    lead = w.shape[:-1]
    dh = w.shape[-1] // n_heads
    w = w.reshape(lead + (n_heads, 2, dh // 2))
    return w[..., ::-1, :].reshape(lead + (n_heads * dh,))


def _dup_heads(w, n_heads):
    lead = w.shape[:-1]
    dh = w.shape[-1] // n_heads
    w = w.reshape(lead + (n_heads, 1, dh))
    return jnp.concatenate([w, w], axis=-2).reshape(lead + (2 * n_heads * dh,))


def _layer1_weights(w_in):
    d = w_in.shape[0]
    sizes = (C_HEADS * C_DK, C_HEADS * C_DK, C_HEADS * C_DV, C_HEADS, C_HEADS, C_HEADS, C_HEADS,
             C_HEADS * C_DV, D_HEADS * D_HD, D_KV_HEADS * D_HD, D_KV_HEADS * D_HD)
    cq, ck, cv, bt_f, bt_b, a_f, a_b, og, dq, dk, dv = jnp.split(w_in, np.cumsum(sizes)[:-1].tolist(), axis=1)
    gates = jnp.concatenate([bt_f, bt_b, a_f, a_b, jnp.zeros((d, HEAD_W - 4 * C_HEADS), w_in.dtype)], axis=1)
    return jnp.concatenate([cq, ck, cv, gates, og, dq, _swap_halves(dq, D_HEADS),
                            _dup_heads(dk, D_KV_HEADS), _dup_heads(_swap_halves(dk, D_KV_HEADS), D_KV_HEADS),
                            _dup_heads(dv, D_KV_HEADS)], axis=1).astype(BF16)


def _rope_tables(n_ctx, t_len):
    rows = t_len // GRID_W
    row = jnp.repeat(jnp.arange(rows, dtype=F32), GRID_W)
    col = jnp.tile(jnp.arange(GRID_W, dtype=F32), rows)
    n_freq = D_HD // 4
    inv = ROPE_BASE ** (-jnp.arange(n_freq, dtype=F32) / n_freq)
    ang = jnp.concatenate([row[:, None] * inv, col[:, None] * inv], axis=-1)
    cos, sin = jnp.cos(ang), jnp.sin(ang)
    cos = jnp.concatenate([jnp.ones((n_ctx, D_HD // 2), F32), cos], axis=0)
    sin = jnp.concatenate([jnp.zeros((n_ctx, D_HD // 2), F32), sin], axis=0)
    return jnp.concatenate([cos] * 4, axis=1), jnp.concatenate([-sin, sin] * 2, axis=1)


def _mixer1(xx, mod, nw, w_in, w_out, conv_w, a_log_f, dt_bias_f, a_log_b, dt_bias_b, dn_norm_w, sinks, n_batch, n_ctx):
    l = xx.shape[1]
    cos, sin = _rope_tables(n_ctx, l - n_ctx)
    qkv, gb_raw, og, dq, dk, dv = _proj1(xx, mod, nw, _layer1_weights(w_in), cos, sin, n_batch)
    zero4 = jnp.zeros((C_HEADS,), F32)
    fill = jnp.zeros((HEAD_W - 4 * C_HEADS,), F32)
    alog = jnp.concatenate([zero4, zero4, a_log_f, a_log_b, fill]).reshape(1, HEAD_W)
    dtb = jnp.concatenate([zero4, zero4, dt_bias_f, dt_bias_b, fill]).reshape(1, HEAD_W)
    cq, ck, cv, gb = _conv1(qkv, conv_w, gb_raw, alog, dtb)
    of, ob = _delta(jnp.asarray(_delta_masks()), cq, ck, cv, gb)
    sink_row = jnp.concatenate([sinks, jnp.zeros((HEAD_W - D_HEADS,), F32)]).reshape(1, HEAD_W)
    od = _attn(dq, dk, dv, sink_row, n_ctx)
    return _read1(xx, of, ob, og, od, dn_norm_w.reshape(1, -1), w_out.astype(BF16), mod, n_batch)


def kernel(x, c, ctx, c_ctx, l0_ada_w, l0_ada_b, l0_norm_mix_w, l0_w_in, l0_w_out, l0_gla_w2_f, l0_gla_b_f, l0_gla_w2_b, l0_gla_b_b, l0_gla_norm_w, l0_hgrn_norm_w, hgrn_lb_logits, l0_norm_ffn_w, l0_router_w, l0_router_b, l0_w_up, l0_b_up, l0_w_down, l0_b_down, l1_ada_w, l1_ada_b, l1_norm_mix_w, l1_w_in, l1_w_out, l1_conv_w, l1_a_log_f, l1_dt_bias_f, l1_a_log_b, l1_dt_bias_b, l1_dn_norm_w, l1_sinks, l1_norm_ffn_w, l1_router_w, l1_router_b, l1_w_up, l1_b_up, l1_w_down, l1_b_down, final_norm_w):
    bsz, t, d = x.shape
    n_ctx = ctx.shape[1]
    assert n_ctx == TM and t % TM == 0
    xx = jnp.concatenate([ctx, x], axis=1)
    mod_rows = -(-(bsz + 1) // 8) * 8
    c_all = jnp.zeros((mod_rows, d), F32).at[:bsz].set(c).at[bsz].set(c_ctx)
    ones = jnp.ones((1, d), F32)

    mod0 = _ada_table(c_all, l0_ada_w, l0_ada_b)
    w0, w2, b2 = _layer0_weights(l0_w_in, l0_gla_w2_f, l0_gla_b_f, l0_gla_w2_b, l0_gla_b_b)
    q, kf, kb, v, lff, lfb, og = _proj0(xx, mod0, l0_norm_mix_w.reshape(1, d), w0, w2, b2, hgrn_lb_logits, bsz)
    of, ob = _scan0(jnp.asarray(_level_masks()), q, kf, kb, v, lff, lfb)
    xx = _read0(xx, of, ob, og, l0_gla_norm_w.reshape(1, -1), l0_hgrn_norm_w.reshape(1, -1), l0_w_out.astype(BF16), mod0, bsz)
    xx = _moe(xx, mod0, l0_norm_ffn_w.reshape(1, d), l0_router_w, l0_router_b, l0_w_up, l0_b_up, l0_w_down, l0_b_down,
              ones, bsz, 0, False)

    mod1 = _ada_table(c_all, l1_ada_w, l1_ada_b)
    xx = _mixer1(xx, mod1, l1_norm_mix_w.reshape(1, d), l1_w_in, l1_w_out, l1_conv_w, l1_a_log_f, l1_dt_bias_f,
                 l1_a_log_b, l1_dt_bias_b, l1_dn_norm_w, l1_sinks, bsz, n_ctx)
    return _moe(xx, mod1, l1_norm_ffn_w.reshape(1, d), l1_router_w, l1_router_b, l1_w_up, l1_b_up, l1_w_down, l1_b_down,
                final_norm_w.reshape(1, d), bsz, 1, True)
```

```python
import functools

import numpy as np
import jax
import jax.numpy as jnp
from jax import lax
from jax.experimental import pallas as pl
from jax.experimental.pallas import tpu as pltpu

F32 = jnp.float32
BF16 = jnp.bfloat16
EPS = 1e-6

CHUNK = 64
A_HEADS, A_DK, A_DV, A_RANK = 4, 64, 128, 16
GATE_TAU = 16.0
B_HEADS, B_DK, B_DV = 4, 128, 128
C_HEADS, C_DK, C_DV = 4, 128, 128
CONV_W = 5
D_HEADS, D_KV_HEADS, D_HD = 8, 2, 64
WINDOW = 128
GRID_W = 64
ROPE_BASE = 10000.0
TOP_K = 4
SWIGLU_LIMIT = 7.0
SWIGLU_ALPHA = 1.702

TM = 256
HEAD_W = 128
EXPERT_TILE = 512
FF_CHUNK = 512
VMEM_LIMIT = 48 * 1024 * 1024


def _cp(sem, vmem=VMEM_LIMIT):
    return pltpu.CompilerParams(dimension_semantics=sem, vmem_limit_bytes=vmem)


def _const_spec(shape):
    nd = len(shape)
    return pl.BlockSpec(shape, lambda *_: (0,) * nd)


def _silu(x):
    return x * jax.nn.sigmoid(x)


def _log_sigmoid(z):
    return jnp.minimum(z, 0.0) - jnp.log1p(jnp.exp(-jnp.abs(z)))


def _norm_mod(x, nw, shift, scale):
    y = x * lax.rsqrt(jnp.mean(x * x, axis=-1, keepdims=True) + EPS) * nw
    return y * (1.0 + scale) + shift


def _mod_rows(mod_ref, batch, blk, n_batch, d, first, count):
    row = jnp.where(blk == 0, n_batch, batch)
    return [mod_ref[pl.ds(row, 1), (first + i) * d:(first + i + 1) * d] for i in range(count)]


def _ada_kernel(c_ref, w_ref, b_ref, o_ref):
    s = _silu(c_ref[...])
    o_ref[...] = jnp.dot(s, w_ref[...], precision=lax.Precision.HIGHEST,
                         preferred_element_type=F32) + b_ref[...]


def _ada_table(c_all, w, b):
    rows, d = c_all.shape
    n = w.shape[1]
    bn = d
    return pl.pallas_call(
        _ada_kernel,
        grid=(n // bn,),
        in_specs=[pl.BlockSpec((rows, d), lambda j: (0, 0)),
                  pl.BlockSpec((d, bn), lambda j: (0, j)),
                  pl.BlockSpec((1, bn), lambda j: (0, j))],
        out_specs=pl.BlockSpec((rows, bn), lambda j: (0, j)),
        out_shape=jax.ShapeDtypeStruct((rows, n), F32),
        compiler_params=_cp(("arbitrary",)),
        name="ada_table",
    )(c_all, w, b.reshape(1, n))


def _proj0_kernel(x_ref, mod_ref, nw_ref, w_ref, w2_ref, b2_ref, lbl_ref,
                  q_ref, kf_ref, kb_ref, v_ref, lff_ref, lfb_ref, og_ref, *, n_batch, d):
    b, j = pl.program_id(0), pl.program_id(1)
    shift, scale = _mod_rows(mod_ref, b, j, n_batch, d, 0, 2)
    h = _norm_mod(x_ref[0], nw_ref[...], shift, scale).astype(BF16)

    def mm(c0, c1):
        return jnp.dot(h, w_ref[:, c0:c1], preferred_element_type=F32)

    hw = A_HEADS * HEAD_W
    q_ref[0, :, 0:hw] = mm(0, hw) * (A_DK ** -0.5)
    q_ref[0, :, hw:2 * hw] = mm(hw, 2 * hw)
    v_ref[0] = mm(2 * hw, 4 * hw)
    og_ref[0] = mm(4 * hw, 6 * hw)
    kg = mm(6 * hw, 7 * hw)
    kf_ref[0, :, 0:hw] = kg
    kb_ref[0, :, 0:hw] = kg
    lg = lbl_ref[...]
    e = jnp.exp(lg - jnp.max(lg, axis=0, keepdims=True))
    lb = e[0:1] / jnp.sum(e, axis=0, keepdims=True)
    log_lb, log_1m = jnp.log(lb), jnp.log1p(-lb)
    ar = mm(9 * hw, 9 * hw + HEAD_W).astype(BF16)
    for di, (k_ref, lf_ref) in enumerate(((kf_ref, lff_ref), (kb_ref, lfb_ref))):
        z = mm((7 + di) * hw, (8 + di) * hw)
        s1 = log_1m + _log_sigmoid(z)
        lf_ref[0, :, hw:2 * hw] = jnp.maximum(log_lb, s1) + jnp.log1p(jnp.exp(-jnp.abs(log_lb - s1)))
        k_ref[0, :, hw:2 * hw] = (1.0 - lb) * jax.nn.sigmoid(-z)
        za = jnp.dot(ar, w2_ref[di], preferred_element_type=F32) + b2_ref[di]
        lf_ref[0, :, 0:hw] = _log_sigmoid(za) * (1.0 / GATE_TAU)


def _proj0(x, mod, nw, w, w2, b2, lbl, n_batch):
    bsz, l, d = x.shape
    n_blk = l // TM
    wide = 2 * A_HEADS * HEAD_W
    blk = pl.BlockSpec((1, TM, wide), lambda b, j: (b, j, 0))
    return pl.pallas_call(
        functools.partial(_proj0_kernel, n_batch=n_batch, d=d),
        grid=(bsz, n_blk),
        in_specs=[pl.BlockSpec((1, TM, d), lambda b, j: (b, j, 0)),
                  _const_spec(mod.shape), _const_spec(nw.shape), _const_spec(w.shape),
                  _const_spec(w2.shape), _const_spec(b2.shape), _const_spec(lbl.shape)],
        out_specs=[blk] * 7,
        out_shape=[jax.ShapeDtypeStruct((bsz, l, wide), F32)] * 7,
        compiler_params=_cp(("parallel", "arbitrary")),
        name="proj0",
    )(x, mod, nw, w, w2, b2, lbl)


_LEVELS = (32, 16, 8, 4, 2, 1)


def _level_masks():
    t = np.arange(CHUNK)[:, None]
    s = np.arange(CHUNK)[None, :]
    out = np.zeros((2, len(_LEVELS) + 1, CHUNK, CHUNK), np.float32)
    for li, m in enumerate(_LEVELS):
        same = (t // (2 * m)) == (s // (2 * m))
        fwd = same & (t % (2 * m) >= m) & (s % (2 * m) < m)
        out[0, li] = fwd
        out[1, li] = fwd.T
    out[:, -1] = np.eye(CHUNK)
    return out


def _cumsum_rows(x, reverse):
    n = x.shape[0]
    r = lax.broadcasted_iota(jnp.int32, x.shape, 0)
    sh = 1
    while sh < n:
        if reverse:
            x = x + jnp.where(r < n - sh, pltpu.roll(x, n - sh, 0), 0.0)
        else:
            x = x + jnp.where(r >= sh, pltpu.roll(x, sh, 0), 0.0)
        sh *= 2
    return x


def _level_ref(cum, m, reverse):
    n = cum.shape[0]
    tgt = m if reverse else m - 1
    if 2 * m >= 8:
        parts = [jnp.broadcast_to(cum[g + tgt:g + tgt + 1, :], (2 * m, cum.shape[1]))
                 for g in range(0, n, 2 * m)]
        return parts[0] if len(parts) == 1 else jnp.concatenate(parts, axis=0)
    pos = lax.broadcasted_iota(jnp.int32, cum.shape, 0) % (2 * m)
    out = cum
    for p in range(2 * m):
        if p == tgt:
            continue
        shift = (p - tgt) % n
        out = jnp.where(pos == p, pltpu.roll(cum, shift, 0), out)
    return out


def _dot_nt(a, b):
    return lax.dot_general(a, b, (((1,), (1,)), ((), ())), preferred_element_type=F32)


def _dot_tn(a, b):
    return lax.dot_general(a, b, (((0,), (0,)), ((), ())), preferred_element_type=F32)


def _gated_prepare(chains, masks):
    n = range(len(chains))
    qs, ks, vs, lfs, revs = zip(*chains)
    cum = [_cumsum_rows(lfs[i], revs[i]) for i in n]
    qb = [qs[i].astype(BF16) for i in n]
    kb = [ks[i].astype(BF16) for i in n]
    vb = [vs[i].astype(BF16) for i in n]
    scores = [masks[revs[i]][len(_LEVELS)] * _dot_nt(qb[i], kb[i]) for i in n]
    for li, m in enumerate(_LEVELS):
        w = [jnp.exp(-jnp.abs(cum[i] - _level_ref(cum[i], m, revs[i]))).astype(BF16) for i in n]
        part = [_dot_nt(qb[i] * w[i], kb[i] * w[i]) for i in n]
        scores = [scores[i] + masks[revs[i]][li] * part[i] for i in n]
    intra = [jnp.dot(scores[i].astype(BF16), vb[i], preferred_element_type=F32) for i in n]
    q_in = [(qs[i] * jnp.exp(cum[i])).astype(BF16) for i in n]
    edge = [cum[i][0:1] if revs[i] else cum[i][CHUNK - 1:CHUNK] for i in n]
    update = [_dot_tn(vb[i], (ks[i] * jnp.exp(edge[i] - cum[i])).astype(BF16)) for i in n]
    carry = [jnp.exp(edge[i]) for i in n]
    return [(intra[i], q_in[i], update[i], carry[i]) for i in n]


SCAN_HEADS = 4


def _scan0_kernel(m_ref, qf_ref, kf_ref, vf_ref, lff_ref, qb_ref, kb_ref, vb_ref, lfb_ref,
                  of_ref, ob_ref, st_ref):
    @pl.when(pl.program_id(2) == 0)
    def _():
        st_ref[...] = jnp.zeros_like(st_ref)

    masks = [[m_ref[r, i] for i in range(len(_LEVELS) + 1)] for r in range(2)]
    n_ch = TM // CHUNK
    fwd_rows = [slice(c * CHUNK, (c + 1) * CHUNK) for c in range(n_ch)]
    bwd_rows = [slice((n_ch - 1 - c) * CHUNK, (n_ch - c) * CHUNK) for c in range(n_ch)]
    lanes = [slice(hh * HEAD_W, (hh + 1) * HEAD_W) for hh in range(SCAN_HEADS)]
    chains = []
    for c in range(n_ch):
        rf, rb = fwd_rows[c], bwd_rows[c]
        for ln in lanes:
            chains.append((qf_ref[0, rf, ln], kf_ref[0, rf, ln], vf_ref[0, rf, ln], lff_ref[0, rf, ln], 0))
            chains.append((qb_ref[0, rb, ln], kb_ref[0, rb, ln], vb_ref[0, rb, ln], lfb_ref[0, rb, ln], 1))
    prep = _gated_prepare(chains, masks)
    per = 2 * SCAN_HEADS
    states = [st_ref[i] for i in range(per)]
    for c in range(n_ch):
        for hh, ln in enumerate(lanes):
            for r, (o_ref, rows) in enumerate(((of_ref, fwd_rows[c]), (ob_ref, bwd_rows[c]))):
                s = 2 * hh + r
                intra, q_in, update, carry = prep[c * per + s]
                o_ref[0, rows, ln] = intra + _dot_nt(q_in, states[s].astype(BF16))
                states[s] = states[s] * carry + update
    for i in range(per):
        st_ref[i] = states[i]


def _bwd_block(i, n_blk):
    return jnp.where(i == 0, 0, n_blk - i)


def _scan0(masks, q, kf, kb, v, lff, lfb):
    bsz, l, wide = q.shape
    n_blk = l // TM
    bw = SCAN_HEADS * HEAD_W
    fwd = pl.BlockSpec((1, TM, bw), lambda b, h, i: (b, i, h))
    bwd = pl.BlockSpec((1, TM, bw), lambda b, h, i: (b, _bwd_block(i, n_blk), h))
    return pl.pallas_call(
        _scan0_kernel,
        grid=(bsz, wide // bw, n_blk),
        in_specs=[_const_spec(masks.shape), fwd, fwd, fwd, fwd, bwd, bwd, bwd, bwd],
        out_specs=[fwd, bwd],
        out_shape=[jax.ShapeDtypeStruct((bsz, l, wide), F32)] * 2,
        scratch_shapes=[pltpu.VMEM((2 * SCAN_HEADS, HEAD_W, HEAD_W), F32)],
        compiler_params=_cp(("parallel", "parallel", "arbitrary")),
        name="scan0",
    )(masks, q, kf, v, lff, q, kb, v, lfb)


def _head_rms(o, w):
    return o * lax.rsqrt(jnp.mean(o * o, axis=-1, keepdims=True) + EPS) * w


def _read0_kernel(x_ref, of_ref, ob_ref, og_ref, nwa_ref, nwb_ref, wout_ref, mod_ref, xo_ref, *, n_batch, d):
    b, j = pl.program_id(0), pl.program_id(1)
    (g1,) = _mod_rows(mod_ref, b, j, n_batch, d, 2, 1)
    o = of_ref[0] + ob_ref[0]
    parts = []
    for hh in range(A_HEADS + B_HEADS):
        nw = nwa_ref[...] if hh < A_HEADS else nwb_ref[...]
        parts.append(_head_rms(o[:, hh * HEAD_W:(hh + 1) * HEAD_W], nw))
    y = jnp.concatenate(parts, axis=-1) * _silu(og_ref[0])
    yo = jnp.dot(y.astype(BF16), wout_ref[...], preferred_element_type=F32)
    xo_ref[0] = x_ref[0] + g1 * yo


def _read0(x, of, ob, og, nwa, nwb, wout, mod, n_batch):
    bsz, l, d = x.shape
    wide = of.shape[-1]
    xb = pl.BlockSpec((1, TM, d), lambda b, j: (b, j, 0))
    wb = pl.BlockSpec((1, TM, wide), lambda b, j: (b, j, 0))
    return pl.pallas_call(
        functools.partial(_read0_kernel, n_batch=n_batch, d=d),
        grid=(bsz, l // TM),
        in_specs=[xb, wb, wb, wb, _const_spec(nwa.shape), _const_spec(nwb.shape),
                  _const_spec(wout.shape), _const_spec(mod.shape)],
        out_specs=xb,
        out_shape=jax.ShapeDtypeStruct(x.shape, F32),
        compiler_params=_cp(("parallel", "arbitrary")),
        name="read0",
    )(x, of, ob, og, nwa, nwb, wout, mod)


def _route_kernel(x_ref, mod_ref, nw_ref, rwt_ref, rb_ref, h_ref, idx_ref, gate_ref, *, n_batch, d, blk_off):
    b, j = pl.program_id(0), pl.program_id(1) + blk_off
    shift, scale = _mod_rows(mod_ref, b, j, n_batch, d, 3, 2)
    h = _norm_mod(x_ref[0], nw_ref[...], shift, scale)
    h_ref[0] = h
    logits = lax.dot_general(rwt_ref[...], h, (((1,), (1,)), ((), ())), precision=lax.Precision.HIGHEST,
                             preferred_element_type=F32) + rb_ref[...]
    n_exp = logits.shape[0]
    rows = lax.broadcasted_iota(jnp.int32, logits.shape, 0)
    vals, idxs = [], []
    for _ in range(TOP_K):
        m = jnp.max(logits, axis=0, keepdims=True)
        i = jnp.min(jnp.where(logits == m, rows, n_exp), axis=0, keepdims=True)
        vals.append(m)
        idxs.append(i)
        logits = jnp.where(rows == i, -jnp.inf, logits)
    ex = [jnp.exp(v - vals[0]) for v in vals]
    tot = ex[0] + ex[1] + ex[2] + ex[3]
    idx_ref[0] = jnp.concatenate(idxs, axis=0)
    gate_ref[0] = jnp.concatenate([e / tot for e in ex], axis=0)


def _route(x, mod, nw, rwt, rb, n_batch, blk_off):
    bsz, l, d = x.shape
    n_blk = l // TM - blk_off
    ls = n_blk * TM
    n_exp = rwt.shape[0]
    return pl.pallas_call(
        functools.partial(_route_kernel, n_batch=n_batch, d=d, blk_off=blk_off),
        grid=(bsz, n_blk),
        in_specs=[pl.BlockSpec((1, TM, d), lambda b, j: (b, j + blk_off, 0)),
                  _const_spec(mod.shape), _const_spec(nw.shape), _const_spec(rwt.shape), _const_spec(rb.shape)],
        out_specs=[pl.BlockSpec((1, TM, d), lambda b, j: (b, j, 0)),
                   pl.BlockSpec((1, TOP_K, TM), lambda b, j: (b, 0, j)),
                   pl.BlockSpec((1, TOP_K, TM), lambda b, j: (b, 0, j))],
        out_shape=[jax.ShapeDtypeStruct((bsz, ls, d), F32),
                   jax.ShapeDtypeStruct((bsz, TOP_K, ls), jnp.int32),
                   jax.ShapeDtypeStruct((bsz, TOP_K, ls), F32)],
        compiler_params=_cp(("parallel", "arbitrary")),
        name="route",
    )(x, mod, nw, rwt, rb)


def _row_copy(src, dst, sem):
    return pltpu.make_async_copy(src, dst, sem)


def _dispatch_kernel(ends_ref, padded_ref, pos_ref, h_ref, hs_ref, zero_ref, sem):
    @pl.when(jnp.logical_and(pl.program_id(0) == 0, pl.program_id(1) == 0))
    def _():
        zero_ref[...] = jnp.zeros_like(zero_ref)
        for e in range(ends_ref.shape[0]):
            @pl.when(padded_ref[e] > 0)
            def _():
                start = pl.multiple_of(ends_ref[e] - EXPERT_TILE, EXPERT_TILE)
                fill = _row_copy(zero_ref, hs_ref.at[pl.ds(start, EXPERT_TILE), :], sem)
                fill.start()
                fill.wait()

    def issue(r, carry):
        for k in range(TOP_K):
            _row_copy(h_ref.at[0, pl.ds(r, 1), :], hs_ref.at[pl.ds(pos_ref[0, k, r], 1), :], sem).start()
        return carry

    lax.fori_loop(0, TM, issue, 0, unroll=8)
    for k in range(TOP_K):
        _row_copy(h_ref.at[0], hs_ref.at[pl.ds(0, TM), :], sem).wait()


def _dispatch(ends, padded, pos, h, n_rows):
    bsz, ls, d = h.shape
    grid_spec = pltpu.PrefetchScalarGridSpec(
        num_scalar_prefetch=2,
        grid=(bsz, ls // TM),
        in_specs=[pl.BlockSpec((1, TOP_K, TM), lambda b, j, en, pd: (b, 0, j), memory_space=pltpu.SMEM),
                  pl.BlockSpec((1, TM, d), lambda b, j, en, pd: (b, j, 0))],
        out_specs=pl.BlockSpec(memory_space=pl.ANY),
        scratch_shapes=[pltpu.VMEM((EXPERT_TILE, d), F32), pltpu.SemaphoreType.DMA(())],
    )
    return pl.pallas_call(
        _dispatch_kernel,
        grid_spec=grid_spec,
        out_shape=jax.ShapeDtypeStruct((n_rows, d), F32),
        compiler_params=_cp(("arbitrary", "arbitrary")),
        name="dispatch",
    )(ends, padded, pos, h)


def _expert_kernel(te_ref, nu_ref, hs_ref, wup_ref, bup_ref, wdn_ref, bdn_ref, y_ref):
    del te_ref
    i = pl.program_id(0)

    @pl.when(i < nu_ref[0])
    def _():
        h = hs_ref[...].astype(BF16)
        ff = wdn_ref.shape[1]
        y = jnp.zeros(y_ref.shape, F32) + bdn_ref[0]
        fc = min(FF_CHUNK, ff)
        for c in range(0, ff, fc):
            glu = jnp.dot(h, wup_ref[0, :, c:c + fc], preferred_element_type=F32) + bup_ref[0, :, c:c + fc]
            lin = (jnp.dot(h, wup_ref[0, :, ff + c:ff + c + fc], preferred_element_type=F32)
                   + bup_ref[0, :, ff + c:ff + c + fc])
            glu = jnp.minimum(glu, SWIGLU_LIMIT)
            lin = jnp.clip(lin, -SWIGLU_LIMIT, SWIGLU_LIMIT)
            act = glu * jax.nn.sigmoid(SWIGLU_ALPHA * glu) * (lin + 1.0)
            y = y + jnp.dot(act.astype(BF16), wdn_ref[0, c:c + fc, :], preferred_element_type=F32)
        y_ref[...] = y

    @pl.when(i >= nu_ref[0])
    def _():
        y_ref[...] = jnp.zeros_like(y_ref)


def _experts(tile_expert, n_used, hs, wup, bup, wdn, bdn):
    n_rows, d = hs.shape
    n_tiles = n_rows // EXPERT_TILE
    ff = wdn.shape[1]
    grid_spec = pltpu.PrefetchScalarGridSpec(
        num_scalar_prefetch=2,
        grid=(n_tiles,),
        in_specs=[pl.BlockSpec((EXPERT_TILE, d), lambda i, te, nu: (i, 0)),
                  pl.BlockSpec((1, d, 2 * ff), lambda i, te, nu: (te[i], 0, 0)),
                  pl.BlockSpec((1, 1, 2 * ff), lambda i, te, nu: (te[i], 0, 0)),
                  pl.BlockSpec((1, ff, d), lambda i, te, nu: (te[i], 0, 0)),
                  pl.BlockSpec((1, 1, d), lambda i, te, nu: (te[i], 0, 0))],
        out_specs=pl.BlockSpec((EXPERT_TILE, d), lambda i, te, nu: (i, 0)),
    )
    return pl.pallas_call(
        _expert_kernel,
        grid_spec=grid_spec,
        out_shape=jax.ShapeDtypeStruct((n_rows, d), F32),
        compiler_params=_cp(("arbitrary",)),
        name="experts",
    )(tile_expert, n_used, hs, wup, bup, wdn, bdn)


def _combine_kernel(pos_ref, x_ref, gate_ref, mod_ref, fw_ref, y_hbm, xo_ref, ybuf, sem, *, n_batch, d, blk_off, final):
    b, j = pl.program_id(0), pl.program_id(1) + blk_off

    def issue(r, carry):
        for k in range(TOP_K):
            _row_copy(y_hbm.at[pl.ds(pos_ref[0, k, r], 1), :], ybuf.at[k, pl.ds(r, 1), :], sem).start()
        return carry

    lax.fori_loop(0, TM, issue, 0, unroll=8)
    for k in range(TOP_K):
        _row_copy(y_hbm.at[pl.ds(0, TM), :], ybuf.at[k], sem).wait()
    (g2,) = _mod_rows(mod_ref, b, j, n_batch, d, 5, 1)
    gate = gate_ref[0]
    acc = gate[:, 0:1] * ybuf[0]
    for k in range(1, TOP_K):
        acc = acc + gate[:, k:k + 1] * ybuf[k]
    xo = x_ref[0] + g2 * acc
    if final:
        xo = xo * lax.rsqrt(jnp.mean(xo * xo, axis=-1, keepdims=True) + EPS) * fw_ref[...]
    xo_ref[0] = xo


def _combine(pos, x, gate, mod, fw, y, n_batch, blk_off, final):
    bsz, l, d = x.shape
    n_blk = l // TM - blk_off
    ls = n_blk * TM
    return pl.pallas_call(
        functools.partial(_combine_kernel, n_batch=n_batch, d=d, blk_off=blk_off, final=final),
        grid=(bsz, n_blk),
        in_specs=[pl.BlockSpec((1, TOP_K, TM), lambda b, j: (b, 0, j), memory_space=pltpu.SMEM),
                  pl.BlockSpec((1, TM, d), lambda b, j: (b, j + blk_off, 0)),
                  pl.BlockSpec((1, TM, TOP_K), lambda b, j: (b, j, 0)),
                  _const_spec(mod.shape), _const_spec(fw.shape),
                  pl.BlockSpec(memory_space=pl.ANY)],
        out_specs=pl.BlockSpec((1, TM, d), lambda b, j: (b, j, 0)),
        out_shape=jax.ShapeDtypeStruct((bsz, ls, d), F32),
        scratch_shapes=[pltpu.VMEM((TOP_K, TM, d), F32), pltpu.SemaphoreType.DMA(())],
        compiler_params=_cp(("arbitrary", "arbitrary")),
        name="combine",
    )(pos, x, gate, mod, fw, y)


def _route_plan(idx, n_exp):
    shape = idx.shape
    e = idx.reshape(-1)
    onehot = (e[:, None] == jnp.arange(n_exp, dtype=jnp.int32)[None, :])
    grouped = onehot.reshape(-1, TM, n_exp).astype(BF16)
    tri = jnp.tril(jnp.ones((TM, TM), BF16))
    local = jnp.einsum('ts,gse->gte', tri, grouped, preferred_element_type=F32)
    group_total = local[:, -1, :]
    group_off = jnp.cumsum(group_total, axis=0) - group_total
    cs = (local + group_off[:, None, :]).reshape(-1, n_exp)
    rank = jnp.sum(jnp.where(onehot, cs, 0.0), axis=1).astype(jnp.int32) - 1
    counts = (group_off[-1] + group_total[-1]).astype(jnp.int32)
    padded = ((counts + EXPERT_TILE - 1) // EXPERT_TILE) * EXPERT_TILE
    ends = jnp.cumsum(padded)
    starts = ends - padded
    pos = (starts[e] + rank).reshape(shape).astype(jnp.int32)
    n_rows = e.shape[0] + n_exp * EXPERT_TILE
    n_tiles = n_rows // EXPERT_TILE
    n_used = (ends[-1] // EXPERT_TILE).astype(jnp.int32)
    tile_start = jnp.minimum(jnp.arange(n_tiles, dtype=jnp.int32), n_used - 1) * EXPERT_TILE
    tile_expert = jnp.sum((tile_start[:, None] >= ends[None, :]).astype(jnp.int32), axis=1)
    tile_expert = jnp.minimum(tile_expert, n_exp - 1)
    return pos, tile_expert, n_used.reshape(1), n_rows, ends.astype(jnp.int32), padded.astype(jnp.int32)


def _moe(x, mod, nw, rw, rb, wup, bup, wdn, bdn, fw, n_batch, blk_off, final):
    n_exp = rw.shape[1]
    d = x.shape[-1]
    h, idx, gate = _route(x, mod, nw, rw.T, rb.reshape(n_exp, 1), n_batch, blk_off)
    pos, tile_expert, n_used, n_rows, ends, padded = _route_plan(idx, n_exp)
    hs = _dispatch(ends, padded, pos, h, n_rows)
    y = _experts(tile_expert, n_used, hs, wup.astype(BF16), bup.reshape(n_exp, 1, -1),
                 wdn.astype(BF16), bdn.reshape(n_exp, 1, d))
    return _combine(pos, x, jnp.transpose(gate, (0, 2, 1)), mod, fw, y, n_batch, blk_off, final)


def _pad_heads(w, n_heads, width):
    lead = w.shape[:-1]
    dh = w.shape[-1] // n_heads
    w = w.reshape(lead + (n_heads, dh))
    w = jnp.pad(w, [(0, 0)] * len(lead) + [(0, 0), (0, width - dh)])
    return w.reshape(lead + (n_heads * width,))


def _layer0_weights(w_in, w2_f, b2_f, w2_b, b2_b):
    d = w_in.shape[0]
    sizes = (A_HEADS * A_DK, A_HEADS * A_DK, A_HEADS * A_DV, A_RANK, A_RANK, A_HEADS * A_DV,
             B_HEADS * B_DK, B_HEADS * B_DK, B_HEADS * B_DK, B_HEADS * B_DV, B_HEADS * B_DV)
    aq, ak, av, ar_f, ar_b, aog, bq, bz_f, bz_b, bi, bog = jnp.split(w_in, np.cumsum(sizes)[:-1].tolist(), axis=1)
    ar = jnp.concatenate([ar_f, ar_b, jnp.zeros((d, HEAD_W - 2 * A_RANK), w_in.dtype)], axis=1)
    w = jnp.concatenate([_pad_heads(aq, A_HEADS, HEAD_W), bq, av, bi, aog, bog,
                         _pad_heads(ak, A_HEADS, HEAD_W), bz_f, bz_b, ar], axis=1).astype(BF16)
    hw = A_HEADS * HEAD_W
    w2 = jnp.zeros((2, HEAD_W, hw), F32)
    w2 = w2.at[0, 0:A_RANK].set(_pad_heads(w2_f, A_HEADS, HEAD_W))
    w2 = w2.at[1, A_RANK:2 * A_RANK].set(_pad_heads(w2_b, A_HEADS, HEAD_W))
    b2 = jnp.stack([_pad_heads(b2_f, A_HEADS, HEAD_W), _pad_heads(b2_b, A_HEADS, HEAD_W)]).reshape(2, 1, hw)
    return w, w2.astype(BF16), b2


QKV_W = 2 * C_HEADS * C_DK + C_HEADS * C_DV
OG_W = C_HEADS * C_DV
DQ_W = D_HEADS * D_HD
DKV_W = D_KV_HEADS * HEAD_W


def _proj1_kernel(x_ref, mod_ref, nw_ref, w_ref, cos_ref, sin_ref,
                  qkv_ref, gb_ref, og_ref, dq_ref, dk_ref, dv_ref, *, n_batch, d):
    b, j = pl.program_id(0), pl.program_id(1)
    shift, scale = _mod_rows(mod_ref, b, j, n_batch, d, 0, 2)
    h = _norm_mod(x_ref[0], nw_ref[...], shift, scale).astype(BF16)

    def mm(c0, c1):
        return jnp.dot(h, w_ref[:, c0:c1], preferred_element_type=F32)

    c = 0
    qkv_ref[0] = mm(c, c + QKV_W); c += QKV_W
    gb_ref[0] = mm(c, c + HEAD_W); c += HEAD_W
    og_ref[0] = mm(c, c + OG_W); c += OG_W
    cos, sin = cos_ref[...], sin_ref[...]
    for s in range(DQ_W // HEAD_W):
        xs = mm(c + s * HEAD_W, c + (s + 1) * HEAD_W)
        xp = mm(c + DQ_W + s * HEAD_W, c + DQ_W + (s + 1) * HEAD_W)
        dq_ref[0, :, s * HEAD_W:(s + 1) * HEAD_W] = (xs * cos + xp * sin) * (D_HD ** -0.5)
    c += 2 * DQ_W
    for s in range(DKV_W // HEAD_W):
        xs = mm(c + s * HEAD_W, c + (s + 1) * HEAD_W)
        xp = mm(c + DKV_W + s * HEAD_W, c + DKV_W + (s + 1) * HEAD_W)
        dk_ref[0, :, s * HEAD_W:(s + 1) * HEAD_W] = xs * cos + xp * sin
    c += 2 * DKV_W
    dv_ref[0] = mm(c, c + DKV_W)


def _proj1(x, mod, nw, w, cos, sin, n_batch):
    bsz, l, d = x.shape
    widths = (QKV_W, HEAD_W, OG_W, DQ_W, DKV_W, DKV_W)
    return pl.pallas_call(
        functools.partial(_proj1_kernel, n_batch=n_batch, d=d),
        grid=(bsz, l // TM),
        in_specs=[pl.BlockSpec((1, TM, d), lambda b, j: (b, j, 0)),
                  _const_spec(mod.shape), _const_spec(nw.shape), _const_spec(w.shape),
                  pl.BlockSpec((TM, HEAD_W), lambda b, j: (j, 0)),
                  pl.BlockSpec((TM, HEAD_W), lambda b, j: (j, 0))],
        out_specs=[pl.BlockSpec((1, TM, wd), lambda b, j: (b, j, 0)) for wd in widths],
        out_shape=[jax.ShapeDtypeStruct((bsz, l, wd), F32) for wd in widths],
        compiler_params=_cp(("parallel", "arbitrary")),
        name="proj1",
    )(x, mod, nw, w, cos, sin)


HALO = 8


def _conv1_kernel(cur_ref, prev_ref, next_ref, cw_ref, gbr_ref, alog_ref, dtb_ref,
                  q_ref, k_ref, v_ref, gb_ref, xe_ref):
    j, n_blk = pl.program_id(1), pl.num_programs(1)
    prev_ok = j >= 2
    next_ok = jnp.logical_and(j >= 1, j < n_blk - 1)
    xe_ref[0:HALO, :] = jnp.where(prev_ok, prev_ref[0], 0.0)
    xe_ref[HALO:HALO + TM, :] = cur_ref[0]
    xe_ref[HALO + TM:2 * HALO + TM, :] = jnp.where(next_ok, next_ref[0], 0.0)
    pad = CONV_W // 2
    hw = C_HEADS * HEAD_W
    for g, o_ref in enumerate((q_ref, k_ref, v_ref)):
        cols = slice(g * hw, (g + 1) * hw)
        acc = cw_ref[0:1, cols] * xe_ref[HALO - pad:HALO - pad + TM, cols]
        for i in range(1, CONV_W):
            acc = acc + cw_ref[i:i + 1, cols] * xe_ref[HALO - pad + i:HALO - pad + i + TM, cols]
        y = _silu(acc)
        if g < 2:
            parts = []
            for hh in range(C_HEADS):
                yh = y[:, hh * HEAD_W:(hh + 1) * HEAD_W]
                yh = yh * lax.rsqrt(jnp.sum(yh * yh, axis=-1, keepdims=True) + EPS)
                parts.append(yh * (C_DK ** -0.5) if g == 0 else yh)
            y = jnp.concatenate(parts, axis=-1)
        o_ref[0] = y
    raw = gbr_ref[0]
    z = raw + dtb_ref[...]
    softplus = jnp.maximum(z, 0.0) + jnp.log1p(jnp.exp(-jnp.abs(z)))
    lane = lax.broadcasted_iota(jnp.int32, raw.shape, 1)
    gb_ref[0] = jnp.where(lane < 2 * C_HEADS, jax.nn.sigmoid(raw), -jnp.exp(alog_ref[...]) * softplus)


def _conv1(qkv, conv_w, gb_raw, alog, dtb):
    bsz, l, wd = qkv.shape
    n_blk = l // TM
    per = TM // HALO
    last = l // HALO - 1
    hw = C_HEADS * HEAD_W
    ob = pl.BlockSpec((1, TM, hw), lambda b, j: (b, j, 0))
    gbs = pl.BlockSpec((1, TM, HEAD_W), lambda b, j: (b, j, 0))
    return pl.pallas_call(
        _conv1_kernel,
        grid=(bsz, n_blk),
        in_specs=[pl.BlockSpec((1, TM, wd), lambda b, j: (b, j, 0)),
                  pl.BlockSpec((1, HALO, wd), lambda b, j: (b, jnp.maximum(j * per - 1, 0), 0)),
                  pl.BlockSpec((1, HALO, wd), lambda b, j: (b, jnp.minimum((j + 1) * per, last), 0)),
                  _const_spec(conv_w.shape), gbs, _const_spec(alog.shape), _const_spec(dtb.shape)],
        out_specs=[ob, ob, ob, gbs],
        out_shape=[jax.ShapeDtypeStruct((bsz, l, hw), F32)] * 3 + [jax.ShapeDtypeStruct((bsz, l, HEAD_W), F32)],
        scratch_shapes=[pltpu.VMEM((TM + 2 * HALO, wd), F32)],
        compiler_params=_cp(("parallel", "arbitrary")),
        name="conv1",
    )(qkv, qkv, qkv, conv_w, gb_raw, alog, dtb)


_MERGE = (16, 32, 64)
_BASE = 8


def _delta_masks():
    t = np.arange(CHUNK)[:, None]
    s = np.arange(CHUNK)[None, :]
    out = [s <= t, s < t, s >= t, s > t, (t // _BASE) == (s // _BASE)]
    for m2 in _MERGE:
        out.append(((t // m2) == (s // m2)) & ((t // (m2 // 2)) != (s // (m2 // 2))))
    return np.stack(out).astype(np.float32)


def _bdot(a, b):
    return jnp.dot(a.astype(BF16), b.astype(BF16), preferred_element_type=F32)


DELTA_HEADS = 4


def _delta_prepare(chains, dm):
    n = range(len(chains))
    qs, ks, vs, gs, betas, revs = zip(*chains)
    incl = [dm[2] if r else dm[0] for r in revs]
    strict = [dm[3] if r else dm[1] for r in revs]
    cum = [_cumsum_rows(jnp.broadcast_to(gs[i], qs[i].shape), revs[i]) for i in n]
    decay = [incl[i] * jnp.exp(jnp.where(incl[i] > 0, cum[i][:, 0:CHUNK] - cum[i].T[0:CHUNK, :], 0.0)) for i in n]
    kbeta = [ks[i] * betas[i] for i in n]
    kbf = [ks[i].astype(BF16) for i in n]
    a = [strict[i] * _dot_nt(kbeta[i].astype(BF16), kbf[i]) * decay[i] for i in n]
    eye = dm[0] * dm[2]
    n0 = [-(a[i] * dm[4]) for i in n]
    n2 = [_bdot(n0[i], n0[i]) for i in n]
    n4 = [_bdot(n2[i], n2[i]) for i in n]
    t = [eye + n0[i] for i in n]
    t = [t[i] + _bdot(t[i], n2[i]) for i in n]
    t = [t[i] + _bdot(t[i], n4[i]) for i in n]
    for li in range(len(_MERGE)):
        inner = [_bdot(a[i] * dm[5 + li], t[i]) for i in n]
        t = [t[i] - _bdot(t[i], inner[i]) for i in n]
    uw = [_bdot(t[i], jnp.concatenate([vs[i] * betas[i], kbeta[i] * jnp.exp(cum[i])], axis=1)) for i in n]
    scores = [(_dot_nt(qs[i].astype(BF16), kbf[i]) * decay[i]).astype(BF16) for i in n]
    edge = [cum[i][0:1] if revs[i] else cum[i][CHUNK - 1:CHUNK] for i in n]
    q_in = [(qs[i] * jnp.exp(cum[i])).astype(BF16) for i in n]
    k_out = [(ks[i] * jnp.exp(edge[i] - cum[i])).astype(BF16) for i in n]
    carry = [jnp.exp(edge[i]) for i in n]
    return [(uw[i][:, 0:HEAD_W], uw[i][:, HEAD_W:].astype(BF16), scores[i], q_in[i], k_out[i], carry[i]) for i in n]


def _delta_step(prep, states):
    n = range(len(prep))
    sb = [states[i].astype(BF16) for i in n]
    v_new = [prep[i][0] - jnp.dot(prep[i][1], sb[i], preferred_element_type=F32) for i in n]
    vb = [v_new[i].astype(BF16) for i in n]
    o = [jnp.dot(prep[i][3], sb[i], preferred_element_type=F32)
         + jnp.dot(prep[i][2], vb[i], preferred_element_type=F32) for i in n]
    new = [states[i] * prep[i][5] + _dot_tn(prep[i][4], vb[i]) for i in n]
    return o, new


def _lane_col(x, lane_idx):
    lane = lax.broadcasted_iota(jnp.int32, x.shape, 1)
    return jnp.sum(jnp.where(lane == lane_idx, x, 0.0), axis=-1, keepdims=True)


def _delta_kernel(dm_ref, qf_ref, kf_ref, vf_ref, gbf_ref, qb_ref, kb_ref, vb_ref, gbb_ref,
                  of_ref, ob_ref, st_ref):
    @pl.when(pl.program_id(2) == 0)
    def _():
        st_ref[...] = jnp.zeros_like(st_ref)

    dm = [dm_ref[i] for i in range(dm_ref.shape[0])]
    n_ch = TM // CHUNK
    gbf, gbb = gbf_ref[0], gbb_ref[0]
    chains = []
    cols = []
    for hh in range(DELTA_HEADS):
        h = pl.program_id(1) * DELTA_HEADS + hh
        cols.append((_lane_col(gbf, h), _lane_col(gbf, 2 * C_HEADS + h),
                     _lane_col(gbb, C_HEADS + h), _lane_col(gbb, 3 * C_HEADS + h)))
    for c in range(n_ch):
        rf = slice(c * CHUNK, (c + 1) * CHUNK)
        rb = slice((n_ch - 1 - c) * CHUNK, (n_ch - c) * CHUNK)
        for hh in range(DELTA_HEADS):
            lanes = slice(hh * HEAD_W, (hh + 1) * HEAD_W)
            beta_f, g_f, beta_b, g_b = cols[hh]
            chains.append((qf_ref[0, rf, lanes], kf_ref[0, rf, lanes], vf_ref[0, rf, lanes], g_f[rf], beta_f[rf], False))
            chains.append((qb_ref[0, rb, lanes], kb_ref[0, rb, lanes], vb_ref[0, rb, lanes], g_b[rb], beta_b[rb], True))
    prep = _delta_prepare(chains, dm)
    per = 2 * DELTA_HEADS
    states = [st_ref[i] for i in range(per)]
    for c in range(n_ch):
        rf = slice(c * CHUNK, (c + 1) * CHUNK)
        rb = slice((n_ch - 1 - c) * CHUNK, (n_ch - c) * CHUNK)
        outs, states = _delta_step(prep[c * per:(c + 1) * per], states)
        for hh in range(DELTA_HEADS):
            lanes = slice(hh * HEAD_W, (hh + 1) * HEAD_W)
            of_ref[0, rf, lanes] = outs[2 * hh]
            ob_ref[0, rb, lanes] = outs[2 * hh + 1]
    for i in range(per):
        st_ref[i] = states[i]


def _delta(dmasks, q, k, v, gb):
    bsz, l, wide = q.shape
    n_blk = l // TM
    bw = DELTA_HEADS * HEAD_W
    fwd = pl.BlockSpec((1, TM, bw), lambda b, h, i: (b, i, h))
    bwd = pl.BlockSpec((1, TM, bw), lambda b, h, i: (b, _bwd_block(i, n_blk), h))
    gf = pl.BlockSpec((1, TM, HEAD_W), lambda b, h, i: (b, i, 0))
    gbw = pl.BlockSpec((1, TM, HEAD_W), lambda b, h, i: (b, _bwd_block(i, n_blk), 0))
    return pl.pallas_call(
        _delta_kernel,
        grid=(bsz, wide // bw, n_blk),
        in_specs=[_const_spec(dmasks.shape), fwd, fwd, fwd, gf, bwd, bwd, bwd, gbw],
        out_specs=[fwd, bwd],
        out_shape=[jax.ShapeDtypeStruct((bsz, l, wide), F32)] * 2,
        scratch_shapes=[pltpu.VMEM((2 * DELTA_HEADS, HEAD_W, HEAD_W), F32)],
        compiler_params=_cp(("parallel", "parallel", "arbitrary")),
        name="delta",
    )(dmasks, q, k, v, gb, q, k, v, gb)


def _attn_kernel(q_ref, kp_ref, kc_ref, kn_ref, vp_ref, vc_ref, vn_ref, kx_ref, vx_ref, sink_ref, o_ref, *, t_len):
    i = pl.program_id(1)
    q = q_ref[0]
    kl = jnp.concatenate([kp_ref[0], kc_ref[0], kn_ref[0]], axis=0).astype(BF16)
    vl = jnp.concatenate([vp_ref[0], vc_ref[0], vn_ref[0]], axis=0).astype(BF16)
    kx, vx = kx_ref[0].astype(BF16), vx_ref[0].astype(BF16)
    qpos = lax.broadcasted_iota(jnp.int32, (WINDOW, 3 * WINDOW), 0)
    kpos = lax.broadcasted_iota(jnp.int32, (WINDOW, 3 * WINDOW), 1) - WINDOW
    k_abs = i * WINDOW + kpos
    valid = (jnp.abs(kpos - qpos) <= WINDOW) & (k_abs >= 0) & (k_abs < t_len)
    low = lax.broadcasted_iota(jnp.int32, (WINDOW, HEAD_W), 1) < D_HD
    group = D_HEADS // D_KV_HEADS
    heads = range(D_HEADS)
    cols = [slice((h // group) * HEAD_W, (h // group + 1) * HEAD_W) for h in heads]
    qm = [jnp.where(low if h % 2 == 0 else jnp.logical_not(low), q[:, (h // 2) * HEAD_W:(h // 2 + 1) * HEAD_W], 0.0)
          .astype(BF16) for h in heads]
    s_l = [jnp.where(valid, _dot_nt(qm[h], kl[:, cols[h]]), -jnp.inf) for h in heads]
    s_x = [_dot_nt(qm[h], kx[:, cols[h]]) for h in heads]
    sink = [sink_ref[:, h:h + 1] for h in heads]
    m = [jnp.maximum(jnp.maximum(jnp.max(s_l[h], axis=-1, keepdims=True), jnp.max(s_x[h], axis=-1, keepdims=True)),
                     sink[h]) for h in heads]
    p_l = [jnp.exp(s_l[h] - m[h]) for h in heads]
    p_x = [jnp.exp(s_x[h] - m[h]) for h in heads]
    den = [jnp.sum(p_l[h], axis=-1, keepdims=True) + jnp.sum(p_x[h], axis=-1, keepdims=True) + jnp.exp(sink[h] - m[h])
           for h in heads]
    o = [(jnp.dot(p_l[h].astype(BF16), vl[:, cols[h]], preferred_element_type=F32)
          + jnp.dot(p_x[h].astype(BF16), vx[:, cols[h]], preferred_element_type=F32)) / den[h] for h in heads]
    for p in range(D_HEADS // 2):
        o_ref[0, :, p * HEAD_W:(p + 1) * HEAD_W] = jnp.where(low, o[2 * p], o[2 * p + 1])


def _attn(dq, dk, dv, sinks, n_ctx):
    bsz, l, _ = dq.shape
    t_len = l - n_ctx
    nq = t_len // WINDOW
    off = n_ctx // WINDOW

    def kv(delta):
        return pl.BlockSpec((1, WINDOW, DKV_W), lambda b, i: (b, off + jnp.clip(i + delta, 0, nq - 1), 0))

    ctx = pl.BlockSpec((1, n_ctx, DKV_W), lambda b, i: (b, 0, 0))
    return pl.pallas_call(
        functools.partial(_attn_kernel, t_len=t_len),
        grid=(bsz, nq),
        in_specs=[pl.BlockSpec((1, WINDOW, DQ_W), lambda b, i: (b, off + i, 0)),
                  kv(-1), kv(0), kv(1), kv(-1), kv(0), kv(1), ctx, ctx, _const_spec(sinks.shape)],
        out_specs=pl.BlockSpec((1, WINDOW, DQ_W), lambda b, i: (b, i, 0)),
        out_shape=jax.ShapeDtypeStruct((bsz, t_len, DQ_W), F32),
        compiler_params=_cp(("parallel", "arbitrary")),
        name="attn",
    )(dq, dk, dk, dk, dv, dv, dv, dk, dv, sinks)


def _read1_kernel(x_ref, of_ref, ob_ref, og_ref, od_ref, nw_ref, wout_ref, mod_ref, xo_ref, *, n_batch, d):
    b, j = pl.program_id(0), pl.program_id(1) + 1
    (g1,) = _mod_rows(mod_ref, b, j, n_batch, d, 2, 1)
    o = of_ref[0] + ob_ref[0]
    parts = [_head_rms(o[:, hh * HEAD_W:(hh + 1) * HEAD_W], nw_ref[...]) for hh in range(C_HEADS)]
    y = jnp.concatenate(parts, axis=-1) * _silu(og_ref[0])
    y = jnp.concatenate([y, od_ref[0]], axis=-1)
    xo_ref[0] = x_ref[0] + g1 * jnp.dot(y.astype(BF16), wout_ref[...], preferred_element_type=F32)


def _read1(x, of, ob, og, od, nw, wout, mod, n_batch):
    bsz, l, d = x.shape
    lat = lambda wd: pl.BlockSpec((1, TM, wd), lambda b, j: (b, j + 1, 0))
    return pl.pallas_call(
        functools.partial(_read1_kernel, n_batch=n_batch, d=d),
        grid=(bsz, l // TM - 1),
        in_specs=[lat(d), lat(OG_W), lat(OG_W), lat(OG_W),
                  pl.BlockSpec((1, TM, DQ_W), lambda b, j: (b, j, 0)),
                  _const_spec(nw.shape), _const_spec(wout.shape), _const_spec(mod.shape)],
        out_specs=lat(d),
        out_shape=jax.ShapeDtypeStruct(x.shape, F32),
        input_output_aliases={0: 0},
        compiler_params=_cp(("parallel", "arbitrary")),
        name="read1",
    )(x, of, ob, og, od, nw, wout, mod)


def _swap_halves(w, n_heads):
    lead = w.shape[:-1]
    dh = w.shape[-1] // n_heads
    w = w.reshape(lead + (n_heads, 2, dh // 2))
    return w[..., ::-1, :].reshape(lead + (n_heads * dh,))


def _dup_heads(w, n_heads):
    lead = w.shape[:-1]
    dh = w.shape[-1] // n_heads
    w = w.reshape(lead + (n_heads, 1, dh))
    return jnp.concatenate([w, w], axis=-2).reshape(lead + (2 * n_heads * dh,))


def _layer1_weights(w_in):
    d = w_in.shape[0]
    sizes = (C_HEADS * C_DK, C_HEADS * C_DK, C_HEADS * C_DV, C_HEADS, C_HEADS, C_HEADS, C_HEADS,
             C_HEADS * C_DV, D_HEADS * D_HD, D_KV_HEADS * D_HD, D_KV_HEADS * D_HD)
    cq, ck, cv, bt_f, bt_b, a_f, a_b, og, dq, dk, dv = jnp.split(w_in, np.cumsum(sizes)[:-1].tolist(), axis=1)
    gates = jnp.concatenate([bt_f, bt_b, a_f, a_b, jnp.zeros((d, HEAD_W - 4 * C_HEADS), w_in.dtype)], axis=1)
    return jnp.concatenate([cq, ck, cv, gates, og, dq, _swap_halves(dq, D_HEADS),
                            _dup_heads(dk, D_KV_HEADS), _dup_heads(_swap_halves(dk, D_KV_HEADS), D_KV_HEADS),
                            _dup_heads(dv, D_KV_HEADS)], axis=1).astype(BF16)


def _rope_tables(n_ctx, t_len):
    rows = t_len // GRID_W
    row = jnp.repeat(jnp.arange(rows, dtype=F32), GRID_W)
    col = jnp.tile(jnp.arange(GRID_W, dtype=F32), rows)
    n_freq = D_HD // 4
    inv = ROPE_BASE ** (-jnp.arange(n_freq, dtype=F32) / n_freq)
    ang = jnp.concatenate([row[:, None] * inv, col[:, None] * inv], axis=-1)
    cos, sin = jnp.cos(ang), jnp.sin(ang)
    cos = jnp.concatenate([jnp.ones((n_ctx, D_HD // 2), F32), cos], axis=0)
    sin = jnp.concatenate([jnp.zeros((n_ctx, D_HD // 2), F32), sin], axis=0)
    return jnp.concatenate([cos] * 4, axis=1), jnp.concatenate([-sin, sin] * 2, axis=1)


def _mixer1(xx, mod, nw, w_in, w_out, conv_w, a_log_f, dt_bias_f, a_log_b, dt_bias_b, dn_norm_w, sinks, n_batch, n_ctx):
    l = xx.shape[1]
    cos, sin = _rope_tables(n_ctx, l - n_ctx)
    qkv, gb_raw, og, dq, dk, dv = _proj1(xx, mod, nw, _layer1_weights(w_in), cos, sin, n_batch)
    zero4 = jnp.zeros((C_HEADS,), F32)
    fill = jnp.zeros((HEAD_W - 4 * C_HEADS,), F32)
    alog = jnp.concatenate([zero4, zero4, a_log_f, a_log_b, fill]).reshape(1, HEAD_W)
    dtb = jnp.concatenate([zero4, zero4, dt_bias_f, dt_bias_b, fill]).reshape(1, HEAD_W)
    cq, ck, cv, gb = _conv1(qkv, conv_w, gb_raw, alog, dtb)
    of, ob = _delta(jnp.asarray(_delta_masks()), cq, ck, cv, gb)
    sink_row = jnp.concatenate([sinks, jnp.zeros((HEAD_W - D_HEADS,), F32)]).reshape(1, HEAD_W)
    od = _attn(dq, dk, dv, sink_row, n_ctx)
    return _read1(xx, of, ob, og, od, dn_norm_w.reshape(1, -1), w_out.astype(BF16), mod, n_batch)


def kernel(x, c, ctx, c_ctx, l0_ada_w, l0_ada_b, l0_norm_mix_w, l0_w_in, l0_w_out, l0_gla_w2_f, l0_gla_b_f, l0_gla_w2_b, l0_gla_b_b, l0_gla_norm_w, l0_hgrn_norm_w, hgrn_lb_logits, l0_norm_ffn_w, l0_router_w, l0_router_b, l0_w_up, l0_b_up, l0_w_down, l0_b_down, l1_ada_w, l1_ada_b, l1_norm_mix_w, l1_w_in, l1_w_out, l1_conv_w, l1_a_log_f, l1_dt_bias_f, l1_a_log_b, l1_dt_bias_b, l1_dn_norm_w, l1_sinks, l1_norm_ffn_w, l1_router_w, l1_router_b, l1_w_up, l1_b_up, l1_w_down, l1_b_down, final_norm_w):
    bsz, t, d = x.shape
    n_ctx = ctx.shape[1]
    assert n_ctx == TM and t % TM == 0
    xx = jnp.concatenate([ctx, x], axis=1)
    mod_rows = -(-(bsz + 1) // 8) * 8
    c_all = jnp.zeros((mod_rows, d), F32).at[:bsz].set(c).at[bsz].set(c_ctx)
    ones = jnp.ones((1, d), F32)

    mod0 = _ada_table(c_all, l0_ada_w, l0_ada_b)
    w0, w2, b2 = _layer0_weights(l0_w_in, l0_gla_w2_f, l0_gla_b_f, l0_gla_w2_b, l0_gla_b_b)
    q, kf, kb, v, lff, lfb, og = _proj0(xx, mod0, l0_norm_mix_w.reshape(1, d), w0, w2, b2, hgrn_lb_logits, bsz)
    of, ob = _scan0(jnp.asarray(_level_masks()), q, kf, kb, v, lff, lfb)
    xx = _read0(xx, of, ob, og, l0_gla_norm_w.reshape(1, -1), l0_hgrn_norm_w.reshape(1, -1), l0_w_out.astype(BF16), mod0, bsz)
    xx = _moe(xx, mod0, l0_norm_ffn_w.reshape(1, d), l0_router_w, l0_router_b, l0_w_up, l0_b_up, l0_w_down, l0_b_down,
              ones, bsz, 0, False)

    mod1 = _ada_table(c_all, l1_ada_w, l1_ada_b)
    xx = _mixer1(xx, mod1, l1_norm_mix_w.reshape(1, d), l1_w_in, l1_w_out, l1_conv_w, l1_a_log_f, l1_dt_bias_f,
                 l1_a_log_b, l1_dt_bias_b, l1_dn_norm_w, l1_sinks, bsz, n_ctx)
    return _moe(xx, mod1, l1_norm_ffn_w.reshape(1, d), l1_router_w, l1_router_b, l1_w_up, l1_b_up, l1_w_down, l1_b_down,
                final_norm_w.reshape(1, d), bsz, 1, True)
```

```python
import functools

import numpy as np
import jax
import jax.numpy as jnp
from jax import lax
from jax.experimental import pallas as pl
from jax.experimental.pallas import tpu as pltpu

F32 = jnp.float32
BF16 = jnp.bfloat16
EPS = 1e-6

CHUNK = 64
A_HEADS, A_DK, A_DV, A_RANK = 4, 64, 128, 16
GATE_TAU = 16.0
B_HEADS, B_DK, B_DV = 4, 128, 128
C_HEADS, C_DK, C_DV = 4, 128, 128
CONV_W = 5
D_HEADS, D_KV_HEADS, D_HD = 8, 2, 64
WINDOW = 128
GRID_W = 64
ROPE_BASE = 10000.0
TOP_K = 4
SWIGLU_LIMIT = 7.0
SWIGLU_ALPHA = 1.702

TM = 256
HEAD_W = 128
EXPERT_TILE = 512
FF_CHUNK = 512
VMEM_LIMIT = 48 * 1024 * 1024


def _cp(sem, vmem=VMEM_LIMIT):
    return pltpu.CompilerParams(dimension_semantics=sem, vmem_limit_bytes=vmem)


def _const_spec(shape):
    nd = len(shape)
    return pl.BlockSpec(shape, lambda *_: (0,) * nd)


def _silu(x):
    return x * jax.nn.sigmoid(x)


def _log_sigmoid(z):
    return jnp.minimum(z, 0.0) - jnp.log1p(jnp.exp(-jnp.abs(z)))


def _norm_mod(x, nw, shift, scale):
    y = x * lax.rsqrt(jnp.mean(x * x, axis=-1, keepdims=True) + EPS) * nw
    return y * (1.0 + scale) + shift


def _mod_rows(mod_ref, batch, blk, n_batch, d, first, count):
    row = jnp.where(blk == 0, n_batch, batch)
    return [mod_ref[pl.ds(row, 1), (first + i) * d:(first + i + 1) * d] for i in range(count)]


def _ada_kernel(c_ref, w_ref, b_ref, o_ref):
    s = _silu(c_ref[...])
    o_ref[...] = jnp.dot(s, w_ref[...], precision=lax.Precision.HIGHEST,
                         preferred_element_type=F32) + b_ref[...]


def _ada_table(c_all, w, b):
    rows, d = c_all.shape
    n = w.shape[1]
    bn = d
    return pl.pallas_call(
        _ada_kernel,
        grid=(n // bn,),
        in_specs=[pl.BlockSpec((rows, d), lambda j: (0, 0)),
                  pl.BlockSpec((d, bn), lambda j: (0, j)),
                  pl.BlockSpec((1, bn), lambda j: (0, j))],
        out_specs=pl.BlockSpec((rows, bn), lambda j: (0, j)),
        out_shape=jax.ShapeDtypeStruct((rows, n), F32),
        compiler_params=_cp(("arbitrary",)),
        name="ada_table",
    )(c_all, w, b.reshape(1, n))


def _proj0_kernel(x_ref, mod_ref, nw_ref, w_ref, w2_ref, b2_ref, lbl_ref,
                  q_ref, kf_ref, kb_ref, v_ref, lff_ref, lfb_ref, og_ref, *, n_batch, d):
    b, j = pl.program_id(0), pl.program_id(1)
    shift, scale = _mod_rows(mod_ref, b, j, n_batch, d, 0, 2)
    h = _norm_mod(x_ref[0], nw_ref[...], shift, scale).astype(BF16)

    def mm(c0, c1):
        return jnp.dot(h, w_ref[:, c0:c1], preferred_element_type=F32)

    hw = A_HEADS * HEAD_W
    q_ref[0, :, 0:hw] = mm(0, hw) * (A_DK ** -0.5)
    q_ref[0, :, hw:2 * hw] = mm(hw, 2 * hw)
    v_ref[0] = mm(2 * hw, 4 * hw)
    og_ref[0] = mm(4 * hw, 6 * hw)
    kg = mm(6 * hw, 7 * hw)
    kf_ref[0, :, 0:hw] = kg
    kb_ref[0, :, 0:hw] = kg
    lg = lbl_ref[...]
    e = jnp.exp(lg - jnp.max(lg, axis=0, keepdims=True))
    lb = e[0:1] / jnp.sum(e, axis=0, keepdims=True)
    log_lb, log_1m = jnp.log(lb), jnp.log1p(-lb)
    ar = mm(9 * hw, 9 * hw + HEAD_W).astype(BF16)
    for di, (k_ref, lf_ref) in enumerate(((kf_ref, lff_ref), (kb_ref, lfb_ref))):
        z = mm((7 + di) * hw, (8 + di) * hw)
        s1 = log_1m + _log_sigmoid(z)
        lf_ref[0, :, hw:2 * hw] = jnp.maximum(log_lb, s1) + jnp.log1p(jnp.exp(-jnp.abs(log_lb - s1)))
        k_ref[0, :, hw:2 * hw] = (1.0 - lb) * jax.nn.sigmoid(-z)
        za = jnp.dot(ar, w2_ref[di], preferred_element_type=F32) + b2_ref[di]
        lf_ref[0, :, 0:hw] = _log_sigmoid(za) * (1.0 / GATE_TAU)


def _proj0(x, mod, nw, w, w2, b2, lbl, n_batch):
    bsz, l, d = x.shape
    n_blk = l // TM
    wide = 2 * A_HEADS * HEAD_W
    blk = pl.BlockSpec((1, TM, wide), lambda b, j: (b, j, 0))
    return pl.pallas_call(
        functools.partial(_proj0_kernel, n_batch=n_batch, d=d),
        grid=(bsz, n_blk),
        in_specs=[pl.BlockSpec((1, TM, d), lambda b, j: (b, j, 0)),
                  _const_spec(mod.shape), _const_spec(nw.shape), _const_spec(w.shape),
                  _const_spec(w2.shape), _const_spec(b2.shape), _const_spec(lbl.shape)],
        out_specs=[blk] * 7,
        out_shape=[jax.ShapeDtypeStruct((bsz, l, wide), F32)] * 7,
        compiler_params=_cp(("parallel", "arbitrary")),
        name="proj0",
    )(x, mod, nw, w, w2, b2, lbl)


_LEVELS = (32, 16, 8, 4, 2, 1)


def _level_masks():
    t = np.arange(CHUNK)[:, None]
    s = np.arange(CHUNK)[None, :]
    out = np.zeros((2, len(_LEVELS) + 1, CHUNK, CHUNK), np.float32)
    for li, m in enumerate(_LEVELS):
        same = (t // (2 * m)) == (s // (2 * m))
        fwd = same & (t % (2 * m) >= m) & (s % (2 * m) < m)
        out[0, li] = fwd
        out[1, li] = fwd.T
    out[:, -1] = np.eye(CHUNK)
    return out


def _cumsum_rows(x, reverse):
    n = x.shape[0]
    r = lax.broadcasted_iota(jnp.int32, x.shape, 0)
    sh = 1
    while sh < n:
        if reverse:
            x = x + jnp.where(r < n - sh, pltpu.roll(x, n - sh, 0), 0.0)
        else:
            x = x + jnp.where(r >= sh, pltpu.roll(x, sh, 0), 0.0)
        sh *= 2
    return x


def _level_ref(cum, m, reverse):
    n = cum.shape[0]
    tgt = m if reverse else m - 1
    if 2 * m >= 8:
        parts = [jnp.broadcast_to(cum[g + tgt:g + tgt + 1, :], (2 * m, cum.shape[1]))
                 for g in range(0, n, 2 * m)]
        return parts[0] if len(parts) == 1 else jnp.concatenate(parts, axis=0)
    pos = lax.broadcasted_iota(jnp.int32, cum.shape, 0) % (2 * m)
    out = cum
    for p in range(2 * m):
        if p == tgt:
            continue
        shift = (p - tgt) % n
        out = jnp.where(pos == p, pltpu.roll(cum, shift, 0), out)
    return out


def _dot_nt(a, b):
    return lax.dot_general(a, b, (((1,), (1,)), ((), ())), preferred_element_type=F32)


def _dot_tn(a, b):
    return lax.dot_general(a, b, (((0,), (0,)), ((), ())), preferred_element_type=F32)


def _tri3():
    t = np.arange(CHUNK)[:, None]
    s = np.arange(CHUNK)[None, :]
    return np.stack([np.tile(s <= t, (1, 3)), np.tile(s >= t, (1, 3))]).astype(np.float32)


def _cumsum_split(x, tri):
    hi = x.astype(BF16)
    r1 = x - hi.astype(F32)
    mid = r1.astype(BF16)
    lo = (r1 - mid.astype(F32)).astype(BF16)
    return jnp.dot(tri, jnp.concatenate([hi, mid, lo], axis=0), preferred_element_type=F32)


def _gated_prepare(chains, masks, tri3):
    n = range(len(chains))
    qs, ks, vs, lfs, revs = zip(*chains)
    cum = [_cumsum_split(lfs[i], tri3[revs[i]]) for i in n]
    qb = [qs[i].astype(BF16) for i in n]
    kb = [ks[i].astype(BF16) for i in n]
    vb = [vs[i].astype(BF16) for i in n]
    scores = [masks[revs[i]][len(_LEVELS)] * _dot_nt(qb[i], kb[i]) for i in n]
    for li, m in enumerate(_LEVELS):
        w = [jnp.exp(-jnp.abs(cum[i] - _level_ref(cum[i], m, revs[i]))).astype(BF16) for i in n]
        part = [_dot_nt(qb[i] * w[i], kb[i] * w[i]) for i in n]
        scores = [scores[i] + masks[revs[i]][li] * part[i] for i in n]
    intra = [jnp.dot(scores[i].astype(BF16), vb[i], preferred_element_type=F32) for i in n]
    q_in = [(qs[i] * jnp.exp(cum[i])).astype(BF16) for i in n]
    edge = [cum[i][0:1] if revs[i] else cum[i][CHUNK - 1:CHUNK] for i in n]
    update = [_dot_tn(vb[i], (ks[i] * jnp.exp(edge[i] - cum[i])).astype(BF16)) for i in n]
    carry = [jnp.exp(edge[i]) for i in n]
    return [(intra[i], q_in[i], update[i], carry[i]) for i in n]


SCAN_HEADS = 4


def _scan0_kernel(m_ref, tri_ref, qf_ref, kf_ref, vf_ref, lff_ref, qb_ref, kb_ref, vb_ref, lfb_ref,
                  of_ref, ob_ref, st_ref):
    @pl.when(pl.program_id(2) == 0)
    def _():
        st_ref[...] = jnp.zeros_like(st_ref)

    masks = [[m_ref[r, i] for i in range(len(_LEVELS) + 1)] for r in range(2)]
    tri3 = [tri_ref[0], tri_ref[1]]
    n_ch = TM // CHUNK
    fwd_rows = [slice(c * CHUNK, (c + 1) * CHUNK) for c in range(n_ch)]
    bwd_rows = [slice((n_ch - 1 - c) * CHUNK, (n_ch - c) * CHUNK) for c in range(n_ch)]
    lanes = [slice(hh * HEAD_W, (hh + 1) * HEAD_W) for hh in range(SCAN_HEADS)]
    chains = []
    for c in range(n_ch):
        rf, rb = fwd_rows[c], bwd_rows[c]
        for ln in lanes:
            chains.append((qf_ref[0, rf, ln], kf_ref[0, rf, ln], vf_ref[0, rf, ln], lff_ref[0, rf, ln], 0))
            chains.append((qb_ref[0, rb, ln], kb_ref[0, rb, ln], vb_ref[0, rb, ln], lfb_ref[0, rb, ln], 1))
    prep = _gated_prepare(chains, masks, tri3)
    per = 2 * SCAN_HEADS
    states = [st_ref[i] for i in range(per)]
    for c in range(n_ch):
        for hh, ln in enumerate(lanes):
            for r, (o_ref, rows) in enumerate(((of_ref, fwd_rows[c]), (ob_ref, bwd_rows[c]))):
                s = 2 * hh + r
                intra, q_in, update, carry = prep[c * per + s]
                o_ref[0, rows, ln] = intra + _dot_nt(q_in, states[s].astype(BF16))
                states[s] = states[s] * carry + update
    for i in range(per):
        st_ref[i] = states[i]


def _bwd_block(i, n_blk):
    return jnp.where(i == 0, 0, n_blk - i)


def _scan0(masks, tri3, q, kf, kb, v, lff, lfb):
    bsz, l, wide = q.shape
    n_blk = l // TM
    bw = SCAN_HEADS * HEAD_W
    fwd = pl.BlockSpec((1, TM, bw), lambda b, h, i: (b, i, h))
    bwd = pl.BlockSpec((1, TM, bw), lambda b, h, i: (b, _bwd_block(i, n_blk), h))
    return pl.pallas_call(
        _scan0_kernel,
        grid=(bsz, wide // bw, n_blk),
        in_specs=[_const_spec(masks.shape), _const_spec(tri3.shape), fwd, fwd, fwd, fwd, bwd, bwd, bwd, bwd],
        out_specs=[fwd, bwd],
        out_shape=[jax.ShapeDtypeStruct((bsz, l, wide), F32)] * 2,
        scratch_shapes=[pltpu.VMEM((2 * SCAN_HEADS, HEAD_W, HEAD_W), F32)],
        compiler_params=_cp(("parallel", "parallel", "arbitrary")),
        name="scan0",
    )(masks, tri3, q, kf, v, lff, q, kb, v, lfb)


def _head_rms(o, w):
    return o * lax.rsqrt(jnp.mean(o * o, axis=-1, keepdims=True) + EPS) * w


def _read0_kernel(x_ref, of_ref, ob_ref, og_ref, nwa_ref, nwb_ref, wout_ref, mod_ref, xo_ref, *, n_batch, d):
    b, j = pl.program_id(0), pl.program_id(1)
    (g1,) = _mod_rows(mod_ref, b, j, n_batch, d, 2, 1)
    o = of_ref[0] + ob_ref[0]
    parts = []
    for hh in range(A_HEADS + B_HEADS):
        nw = nwa_ref[...] if hh < A_HEADS else nwb_ref[...]
        parts.append(_head_rms(o[:, hh * HEAD_W:(hh + 1) * HEAD_W], nw))
    y = jnp.concatenate(parts, axis=-1) * _silu(og_ref[0])
    yo = jnp.dot(y.astype(BF16), wout_ref[...], preferred_element_type=F32)
    xo_ref[0] = x_ref[0] + g1 * yo


def _read0(x, of, ob, og, nwa, nwb, wout, mod, n_batch):
    bsz, l, d = x.shape
    wide = of.shape[-1]
    xb = pl.BlockSpec((1, TM, d), lambda b, j: (b, j, 0))
    wb = pl.BlockSpec((1, TM, wide), lambda b, j: (b, j, 0))
    return pl.pallas_call(
        functools.partial(_read0_kernel, n_batch=n_batch, d=d),
        grid=(bsz, l // TM),
        in_specs=[xb, wb, wb, wb, _const_spec(nwa.shape), _const_spec(nwb.shape),
                  _const_spec(wout.shape), _const_spec(mod.shape)],
        out_specs=xb,
        out_shape=jax.ShapeDtypeStruct(x.shape, F32),
        compiler_params=_cp(("parallel", "arbitrary")),
        name="read0",
    )(x, of, ob, og, nwa, nwb, wout, mod)


def _route_kernel(x_ref, mod_ref, nw_ref, rwt_ref, rb_ref, h_ref, idx_ref, gate_ref, *, n_batch, d, blk_off):
    b, j = pl.program_id(0), pl.program_id(1) + blk_off
    shift, scale = _mod_rows(mod_ref, b, j, n_batch, d, 3, 2)
    h = _norm_mod(x_ref[0], nw_ref[...], shift, scale)
    h_ref[0] = h
    logits = lax.dot_general(rwt_ref[...], h, (((1,), (1,)), ((), ())), precision=lax.Precision.HIGHEST,
                             preferred_element_type=F32) + rb_ref[...]
    n_exp = logits.shape[0]
    rows = lax.broadcasted_iota(jnp.int32, logits.shape, 0)
    vals, idxs = [], []
    for _ in range(TOP_K):
        m = jnp.max(logits, axis=0, keepdims=True)
        i = jnp.min(jnp.where(logits == m, rows, n_exp), axis=0, keepdims=True)
        vals.append(m)
        idxs.append(i)
        logits = jnp.where(rows == i, -jnp.inf, logits)
    ex = [jnp.exp(v - vals[0]) for v in vals]
    tot = ex[0] + ex[1] + ex[2] + ex[3]
    idx_ref[0] = jnp.concatenate(idxs, axis=0)
    gate_ref[0] = jnp.concatenate([e / tot for e in ex], axis=0)


def _route(x, mod, nw, rwt, rb, n_batch, blk_off):
    bsz, l, d = x.shape
    n_blk = l // TM - blk_off
    ls = n_blk * TM
    n_exp = rwt.shape[0]
    return pl.pallas_call(
        functools.partial(_route_kernel, n_batch=n_batch, d=d, blk_off=blk_off),
        grid=(bsz, n_blk),
        in_specs=[pl.BlockSpec((1, TM, d), lambda b, j: (b, j + blk_off, 0)),
                  _const_spec(mod.shape), _const_spec(nw.shape), _const_spec(rwt.shape), _const_spec(rb.shape)],
        out_specs=[pl.BlockSpec((1, TM, d), lambda b, j: (b, j, 0)),
                   pl.BlockSpec((1, TOP_K, TM), lambda b, j: (b, 0, j)),
                   pl.BlockSpec((1, TOP_K, TM), lambda b, j: (b, 0, j))],
        out_shape=[jax.ShapeDtypeStruct((bsz, ls, d), F32),
                   jax.ShapeDtypeStruct((bsz, TOP_K, ls), jnp.int32),
                   jax.ShapeDtypeStruct((bsz, TOP_K, ls), F32)],
        compiler_params=_cp(("parallel", "arbitrary")),
        name="route",
    )(x, mod, nw, rwt, rb)


def _row_copy(src, dst, sem):
    return pltpu.make_async_copy(src, dst, sem)


def _dispatch_kernel(ends_ref, padded_ref, pos_ref, h_ref, hs_ref, zero_ref, sem):
    @pl.when(jnp.logical_and(pl.program_id(0) == 0, pl.program_id(1) == 0))
    def _():
        zero_ref[...] = jnp.zeros_like(zero_ref)
        for e in range(ends_ref.shape[0]):
            @pl.when(padded_ref[e] > 0)
            def _():
                start = pl.multiple_of(ends_ref[e] - EXPERT_TILE, EXPERT_TILE)
                fill = _row_copy(zero_ref, hs_ref.at[pl.ds(start, EXPERT_TILE), :], sem)
                fill.start()
                fill.wait()

    def issue(r, carry):
        for k in range(TOP_K):
            _row_copy(h_ref.at[0, pl.ds(r, 1), :], hs_ref.at[pl.ds(pos_ref[0, k, r], 1), :], sem).start()
        return carry

    lax.fori_loop(0, TM, issue, 0, unroll=8)
    for k in range(TOP_K):
        _row_copy(h_ref.at[0], hs_ref.at[pl.ds(0, TM), :], sem).wait()


def _dispatch(ends, padded, pos, h, n_rows):
    bsz, ls, d = h.shape
    grid_spec = pltpu.PrefetchScalarGridSpec(
        num_scalar_prefetch=2,
        grid=(bsz, ls // TM),
        in_specs=[pl.BlockSpec((1, TOP_K, TM), lambda b, j, en, pd: (b, 0, j), memory_space=pltpu.SMEM),
                  pl.BlockSpec((1, TM, d), lambda b, j, en, pd: (b, j, 0))],
        out_specs=pl.BlockSpec(memory_space=pl.ANY),
        scratch_shapes=[pltpu.VMEM((EXPERT_TILE, d), F32), pltpu.SemaphoreType.DMA(())],
    )
    return pl.pallas_call(
        _dispatch_kernel,
        grid_spec=grid_spec,
        out_shape=jax.ShapeDtypeStruct((n_rows, d), F32),
        compiler_params=_cp(("arbitrary", "arbitrary")),
        name="dispatch",
    )(ends, padded, pos, h)


def _expert_kernel(te_ref, nu_ref, hs_ref, wup_ref, bup_ref, wdn_ref, bdn_ref, y_ref):
    del te_ref
    i = pl.program_id(0)

    @pl.when(i < nu_ref[0])
    def _():
        h = hs_ref[...].astype(BF16)
        ff = wdn_ref.shape[1]
        y = jnp.zeros(y_ref.shape, F32) + bdn_ref[0]
        fc = min(FF_CHUNK, ff)
        for c in range(0, ff, fc):
            glu = jnp.dot(h, wup_ref[0, :, c:c + fc], preferred_element_type=F32) + bup_ref[0, :, c:c + fc]
            lin = (jnp.dot(h, wup_ref[0, :, ff + c:ff + c + fc], preferred_element_type=F32)
                   + bup_ref[0, :, ff + c:ff + c + fc])
            glu = jnp.minimum(glu, SWIGLU_LIMIT)
            lin = jnp.clip(lin, -SWIGLU_LIMIT, SWIGLU_LIMIT)
            act = glu * jax.nn.sigmoid(SWIGLU_ALPHA * glu) * (lin + 1.0)
            y = y + jnp.dot(act.astype(BF16), wdn_ref[0, c:c + fc, :], preferred_element_type=F32)
        y_ref[...] = y

    @pl.when(i >= nu_ref[0])
    def _():
        y_ref[...] = jnp.zeros_like(y_ref)


def _experts(tile_expert, n_used, hs, wup, bup, wdn, bdn):
    n_rows, d = hs.shape
    n_tiles = n_rows // EXPERT_TILE
    ff = wdn.shape[1]
    grid_spec = pltpu.PrefetchScalarGridSpec(
        num_scalar_prefetch=2,
        grid=(n_tiles,),
        in_specs=[pl.BlockSpec((EXPERT_TILE, d), lambda i, te, nu: (i, 0)),
                  pl.BlockSpec((1, d, 2 * ff), lambda i, te, nu: (te[i], 0, 0)),
                  pl.BlockSpec((1, 1, 2 * ff), lambda i, te, nu: (te[i], 0, 0)),
                  pl.BlockSpec((1, ff, d), lambda i, te, nu: (te[i], 0, 0)),
                  pl.BlockSpec((1, 1, d), lambda i, te, nu: (te[i], 0, 0))],
        out_specs=pl.BlockSpec((EXPERT_TILE, d), lambda i, te, nu: (i, 0)),
    )
    return pl.pallas_call(
        _expert_kernel,
        grid_spec=grid_spec,
        out_shape=jax.ShapeDtypeStruct((n_rows, d), F32),
        compiler_params=_cp(("arbitrary",)),
        name="experts",
    )(tile_expert, n_used, hs, wup, bup, wdn, bdn)


def _combine_kernel(pos_ref, x_ref, gate_ref, mod_ref, fw_ref, y_hbm, xo_ref, ybuf, sem, *, n_batch, d, blk_off, final):
    b, j = pl.program_id(0), pl.program_id(1) + blk_off

    def issue(r, carry):
        for k in range(TOP_K):
            _row_copy(y_hbm.at[pl.ds(pos_ref[0, k, r], 1), :], ybuf.at[k, pl.ds(r, 1), :], sem).start()
        return carry

    lax.fori_loop(0, TM, issue, 0, unroll=8)
    for k in range(TOP_K):
        _row_copy(y_hbm.at[pl.ds(0, TM), :], ybuf.at[k], sem).wait()
    (g2,) = _mod_rows(mod_ref, b, j, n_batch, d, 5, 1)
    gate = gate_ref[0]
    acc = gate[:, 0:1] * ybuf[0]
    for k in range(1, TOP_K):
        acc = acc + gate[:, k:k + 1] * ybuf[k]
    xo = x_ref[0] + g2 * acc
    if final:
        xo = xo * lax.rsqrt(jnp.mean(xo * xo, axis=-1, keepdims=True) + EPS) * fw_ref[...]
    xo_ref[0] = xo


def _combine(pos, x, gate, mod, fw, y, n_batch, blk_off, final):
    bsz, l, d = x.shape
    n_blk = l // TM - blk_off
    ls = n_blk * TM
    return pl.pallas_call(
        functools.partial(_combine_kernel, n_batch=n_batch, d=d, blk_off=blk_off, final=final),
        grid=(bsz, n_blk),
        in_specs=[pl.BlockSpec((1, TOP_K, TM), lambda b, j: (b, 0, j), memory_space=pltpu.SMEM),
                  pl.BlockSpec((1, TM, d), lambda b, j: (b, j + blk_off, 0)),
                  pl.BlockSpec((1, TM, TOP_K), lambda b, j: (b, j, 0)),
                  _const_spec(mod.shape), _const_spec(fw.shape),
                  pl.BlockSpec(memory_space=pl.ANY)],
        out_specs=pl.BlockSpec((1, TM, d), lambda b, j: (b, j, 0)),
        out_shape=jax.ShapeDtypeStruct((bsz, ls, d), F32),
        scratch_shapes=[pltpu.VMEM((TOP_K, TM, d), F32), pltpu.SemaphoreType.DMA(())],
        compiler_params=_cp(("arbitrary", "arbitrary")),
        name="combine",
    )(pos, x, gate, mod, fw, y)


def _route_plan(idx, n_exp):
    shape = idx.shape
    e = idx.reshape(-1)
    onehot = (e[:, None] == jnp.arange(n_exp, dtype=jnp.int32)[None, :])
    grouped = onehot.reshape(-1, TM, n_exp).astype(BF16)
    tri = jnp.tril(jnp.ones((TM, TM), BF16))
    local = jnp.einsum('ts,gse->gte', tri, grouped, preferred_element_type=F32)
    group_total = local[:, -1, :]
    group_off = jnp.cumsum(group_total, axis=0) - group_total
    cs = (local + group_off[:, None, :]).reshape(-1, n_exp)
    rank = jnp.sum(jnp.where(onehot, cs, 0.0), axis=1).astype(jnp.int32) - 1
    counts = (group_off[-1] + group_total[-1]).astype(jnp.int32)
    padded = ((counts + EXPERT_TILE - 1) // EXPERT_TILE) * EXPERT_TILE
    ends = jnp.cumsum(padded)
    starts = ends - padded
    pos = (starts[e] + rank).reshape(shape).astype(jnp.int32)
    n_rows = e.shape[0] + n_exp * EXPERT_TILE
    n_tiles = n_rows // EXPERT_TILE
    n_used = (ends[-1] // EXPERT_TILE).astype(jnp.int32)
    tile_start = jnp.minimum(jnp.arange(n_tiles, dtype=jnp.int32), n_used - 1) * EXPERT_TILE
    tile_expert = jnp.sum((tile_start[:, None] >= ends[None, :]).astype(jnp.int32), axis=1)
    tile_expert = jnp.minimum(tile_expert, n_exp - 1)
    return pos, tile_expert, n_used.reshape(1), n_rows, ends.astype(jnp.int32), padded.astype(jnp.int32)


def _moe(x, mod, nw, rw, rb, wup, bup, wdn, bdn, fw, n_batch, blk_off, final):
    n_exp = rw.shape[1]
    d = x.shape[-1]
    h, idx, gate = _route(x, mod, nw, rw.T, rb.reshape(n_exp, 1), n_batch, blk_off)
    pos, tile_expert, n_used, n_rows, ends, padded = _route_plan(idx, n_exp)
    hs = _dispatch(ends, padded, pos, h, n_rows)
    y = _experts(tile_expert, n_used, hs, wup.astype(BF16), bup.reshape(n_exp, 1, -1),
                 wdn.astype(BF16), bdn.reshape(n_exp, 1, d))
    return _combine(pos, x, jnp.transpose(gate, (0, 2, 1)), mod, fw, y, n_batch, blk_off, final)


GRAN = 8
LOCAL_ROWS = 1280


def _block_plan(idx, n_exp):
    bsz, top_k, ls = idx.shape
    n_blk = ls // TM
    nb = bsz * n_blk
    pairs = top_k * TM
    assert pairs + n_exp * (GRAN - 1) <= LOCAL_ROWS
    e = jnp.transpose(idx.reshape(bsz, top_k, n_blk, TM), (0, 2, 1, 3)).reshape(nb, pairs)
    onehot = e[:, :, None] == jnp.arange(n_exp, dtype=jnp.int32)[None, None, :]
    grouped = onehot.reshape(nb, top_k, TM, n_exp).astype(BF16)
    tri = jnp.tril(jnp.ones((TM, TM), BF16))
    local = jnp.einsum('ts,ngse->ngte', tri, grouped, preferred_element_type=F32)
    g_tot = local[:, :, -1, :]
    g_off = jnp.cumsum(g_tot, axis=1) - g_tot
    cs = (local + g_off[:, :, None, :]).reshape(nb, pairs, n_exp)
    rank = jnp.sum(jnp.where(onehot, cs, 0.0), axis=2).astype(jnp.int32) - 1
    cnt = (g_off[:, -1, :] + g_tot[:, -1, :]).astype(jnp.int32)
    cpad = ((cnt + GRAN - 1) // GRAN) * GRAN
    l_end = jnp.cumsum(cpad, axis=1)
    l_start = l_end - cpad
    ldst = jnp.sum(jnp.where(onehot, l_start[:, None, :], 0), axis=2) + rank
    tot = jnp.sum(cpad, axis=0)
    padded = ((tot + EXPERT_TILE - 1) // EXPERT_TILE) * EXPERT_TILE
    ends = jnp.cumsum(padded)
    starts = ends - padded
    g_start = starts[None, :] + jnp.cumsum(cpad, axis=0) - cpad
    n_gran = LOCAL_ROWS // GRAN
    row0 = jnp.arange(n_gran, dtype=jnp.int32) * GRAN
    gran_e = jnp.sum((row0[None, :, None] >= l_end[:, None, :]).astype(jnp.int32), axis=2)
    gran_oh = jnp.minimum(gran_e, n_exp - 1)[:, :, None] == jnp.arange(n_exp, dtype=jnp.int32)[None, None, :]
    gdst = jnp.sum(jnp.where(gran_oh, (g_start - l_start)[:, None, :], 0), axis=2) + row0[None, :]
    used = (l_end[:, -1] // GRAN).astype(jnp.int32)
    gdst = jnp.where(row0[None, :] < l_end[:, -1:], gdst, 0).astype(jnp.int32)
    n_rows = -(-(nb * pairs + nb * n_exp * (GRAN - 1) + n_exp * EXPERT_TILE) // EXPERT_TILE) * EXPERT_TILE
    n_tiles = n_rows // EXPERT_TILE
    n_used = (ends[-1] // EXPERT_TILE).astype(jnp.int32)
    tile_start = jnp.minimum(jnp.arange(n_tiles, dtype=jnp.int32), n_used - 1) * EXPERT_TILE
    tile_expert = jnp.minimum(jnp.sum((tile_start[:, None] >= ends[None, :]).astype(jnp.int32), axis=1), n_exp - 1)
    ldst = ldst.astype(jnp.int32).reshape(bsz, n_blk, top_k, TM)
    return dict(ldst_rows=jnp.transpose(ldst, (0, 2, 1, 3)).reshape(bsz, top_k, ls), ldst_cols=jnp.transpose(ldst, (0, 1, 3, 2)).reshape(bsz, ls, top_k),
                gdst=gdst.reshape(nb, 1, n_gran), used=used, ends=ends.astype(jnp.int32), padded=padded.astype(jnp.int32),
                tile_expert=tile_expert, n_used=n_used.reshape(1), n_rows=n_rows)


def _gran_copy(src, dst, sem):
    return pltpu.make_async_copy(src, dst, sem)


def _dispatch_blocks_kernel(ends_ref, padded_ref, used_ref, gdst_ref, ldst_ref, h_ref, hs_ref, buf_ref, zero_ref, sem):
    n_blk = pl.num_programs(1)
    blk = pl.program_id(0) * n_blk + pl.program_id(1)

    @pl.when(blk == 0)
    def _():
        zero_ref[...] = jnp.zeros_like(zero_ref)
        for e in range(ends_ref.shape[0]):
            @pl.when(padded_ref[e] > 0)
            def _():
                start = pl.multiple_of(ends_ref[e] - EXPERT_TILE, EXPERT_TILE)
                fill = _gran_copy(zero_ref, hs_ref.at[pl.ds(start, EXPERT_TILE), :], sem)
                fill.start()
                fill.wait()

    rows = lax.broadcasted_iota(jnp.int32, (LOCAL_ROWS, TM), 0)
    hit = rows == ldst_ref[0, 0:1, :]
    for k in range(1, TOP_K):
        hit = jnp.logical_or(hit, rows == ldst_ref[0, k:k + 1, :])
    perm = jnp.where(hit, 1.0, 0.0).astype(BF16)
    buf_ref[...] = jnp.dot(perm, h_ref[0].astype(BF16), preferred_element_type=F32)

    def copy(j):
        src = pl.multiple_of(j * GRAN, GRAN)
        dst = pl.multiple_of(gdst_ref[0, 0, j], GRAN)
        return _gran_copy(buf_ref.at[pl.ds(src, GRAN), :], hs_ref.at[pl.ds(dst, GRAN), :], sem)

    def issue(j, carry):
        copy(j).start()
        return carry

    def drain(j, carry):
        copy(j).wait()
        return carry

    lax.fori_loop(0, used_ref[blk], issue, 0)
    lax.fori_loop(0, used_ref[blk], drain, 0)


def _dispatch_blocks(plan, h):
    bsz, ls, d = h.shape
    n_blk = ls // TM
    n_gran = LOCAL_ROWS // GRAN
    grid_spec = pltpu.PrefetchScalarGridSpec(
        num_scalar_prefetch=3,
        grid=(bsz, n_blk),
        in_specs=[pl.BlockSpec((1, 1, n_gran), lambda b, j, *_: (b * n_blk + j, 0, 0), memory_space=pltpu.SMEM),
                  pl.BlockSpec((1, TOP_K, TM), lambda b, j, *_: (b, 0, j)),
                  pl.BlockSpec((1, TM, d), lambda b, j, *_: (b, j, 0))],
        out_specs=pl.BlockSpec(memory_space=pl.ANY),
        scratch_shapes=[pltpu.VMEM((LOCAL_ROWS, d), F32), pltpu.VMEM((EXPERT_TILE, d), F32),
                        pltpu.SemaphoreType.DMA(())],
    )
    return pl.pallas_call(
        _dispatch_blocks_kernel,
        grid_spec=grid_spec,
        out_shape=jax.ShapeDtypeStruct((plan["n_rows"], d), F32),
        compiler_params=_cp(("arbitrary", "arbitrary")),
        name="dispatch_blocks",
    )(plan["ends"], plan["padded"], plan["used"], plan["gdst"], plan["ldst_rows"], h)


def _combine_blocks_kernel(used_ref, gdst_ref, x_ref, ldst_ref, gate_ref, mod_ref, fw_ref, y_hbm, xo_ref, buf_ref, sem,
                           *, n_batch, d, blk_off, final):
    b, j = pl.program_id(0), pl.program_id(1) + blk_off
    blk = pl.program_id(0) * pl.num_programs(1) + pl.program_id(1)

    @pl.when(blk == 0)
    def _():
        buf_ref[...] = jnp.zeros_like(buf_ref)

    def copy(g):
        dst = pl.multiple_of(g * GRAN, GRAN)
        src = pl.multiple_of(gdst_ref[0, 0, g], GRAN)
        return _gran_copy(y_hbm.at[pl.ds(src, GRAN), :], buf_ref.at[pl.ds(dst, GRAN), :], sem)

    def issue(g, carry):
        copy(g).start()
        return carry

    def drain(g, carry):
        copy(g).wait()
        return carry

    lax.fori_loop(0, used_ref[blk], issue, 0)
    lane = lax.broadcasted_iota(jnp.int32, (TM, LOCAL_ROWS), 1)
    ldst, gate = ldst_ref[0], gate_ref[0]
    weight = jnp.where(lane == ldst[:, 0:1], gate[:, 0:1], 0.0)
    for k in range(1, TOP_K):
        weight = weight + jnp.where(lane == ldst[:, k:k + 1], gate[:, k:k + 1], 0.0)
    w_hi = weight.astype(BF16)
    w_lo = (weight - w_hi.astype(F32)).astype(BF16)
    lax.fori_loop(0, used_ref[blk], drain, 0)
    y = buf_ref[...].astype(BF16)
    acc = jnp.dot(w_hi, y, preferred_element_type=F32) + jnp.dot(w_lo, y, preferred_element_type=F32)
    (g2,) = _mod_rows(mod_ref, b, j, n_batch, d, 5, 1)
    xo = x_ref[0] + g2 * acc
    if final:
        xo = xo * lax.rsqrt(jnp.mean(xo * xo, axis=-1, keepdims=True) + EPS) * fw_ref[...]
    xo_ref[0] = xo


def _combine_blocks(plan, x, gate, mod, fw, y, n_batch, blk_off, final):
    bsz, l, d = x.shape
    n_blk = l // TM - blk_off
    ls = n_blk * TM
    n_gran = LOCAL_ROWS // GRAN
    grid_spec = pltpu.PrefetchScalarGridSpec(
        num_scalar_prefetch=1,
        grid=(bsz, n_blk),
        in_specs=[pl.BlockSpec((1, 1, n_gran), lambda b, j, *_: (b * n_blk + j, 0, 0), memory_space=pltpu.SMEM),
                  pl.BlockSpec((1, TM, d), lambda b, j, *_: (b, j + blk_off, 0)),
                  pl.BlockSpec((1, TM, TOP_K), lambda b, j, *_: (b, j, 0)),
                  pl.BlockSpec((1, TM, TOP_K), lambda b, j, *_: (b, j, 0)),
                  _const_spec(mod.shape), _const_spec(fw.shape),
                  pl.BlockSpec(memory_space=pl.ANY)],
        out_specs=pl.BlockSpec((1, TM, d), lambda b, j, *_: (b, j, 0)),
        scratch_shapes=[pltpu.VMEM((LOCAL_ROWS, d), F32), pltpu.SemaphoreType.DMA(())],
    )
    return pl.pallas_call(
        functools.partial(_combine_blocks_kernel, n_batch=n_batch, d=d, blk_off=blk_off, final=final),
        grid_spec=grid_spec,
        out_shape=jax.ShapeDtypeStruct((bsz, ls, d), F32),
        compiler_params=_cp(("arbitrary", "arbitrary")),
        name="combine_blocks",
    )(plan["used"], plan["gdst"], x, plan["ldst_cols"], gate, mod, fw, y)


def _moe_blocks(x, mod, nw, rw, rb, wup, bup, wdn, bdn, fw, n_batch, blk_off, final):
    n_exp = rw.shape[1]
    d = x.shape[-1]
    h, idx, gate = _route(x, mod, nw, rw.T, rb.reshape(n_exp, 1), n_batch, blk_off)
    plan = _block_plan(idx, n_exp)
    hs = _dispatch_blocks(plan, h)
    y = _experts(plan["tile_expert"], plan["n_used"], hs, wup.astype(BF16), bup.reshape(n_exp, 1, -1),
                 wdn.astype(BF16), bdn.reshape(n_exp, 1, d))
    return _combine_blocks(plan, x, jnp.transpose(gate, (0, 2, 1)), mod, fw, y, n_batch, blk_off, final)


def _pad_heads(w, n_heads, width):
    lead = w.shape[:-1]
    dh = w.shape[-1] // n_heads
    w = w.reshape(lead + (n_heads, dh))
    w = jnp.pad(w, [(0, 0)] * len(lead) + [(0, 0), (0, width - dh)])
    return w.reshape(lead + (n_heads * width,))


def _layer0_weights(w_in, w2_f, b2_f, w2_b, b2_b):
    d = w_in.shape[0]
    sizes = (A_HEADS * A_DK, A_HEADS * A_DK, A_HEADS * A_DV, A_RANK, A_RANK, A_HEADS * A_DV,
             B_HEADS * B_DK, B_HEADS * B_DK, B_HEADS * B_DK, B_HEADS * B_DV, B_HEADS * B_DV)
    aq, ak, av, ar_f, ar_b, aog, bq, bz_f, bz_b, bi, bog = jnp.split(w_in, np.cumsum(sizes)[:-1].tolist(), axis=1)
    ar = jnp.concatenate([ar_f, ar_b, jnp.zeros((d, HEAD_W - 2 * A_RANK), w_in.dtype)], axis=1)
    w = jnp.concatenate([_pad_heads(aq, A_HEADS, HEAD_W), bq, av, bi, aog, bog,
                         _pad_heads(ak, A_HEADS, HEAD_W), bz_f, bz_b, ar], axis=1).astype(BF16)
    hw = A_HEADS * HEAD_W
    w2 = jnp.zeros((2, HEAD_W, hw), F32)
    w2 = w2.at[0, 0:A_RANK].set(_pad_heads(w2_f, A_HEADS, HEAD_W))
    w2 = w2.at[1, A_RANK:2 * A_RANK].set(_pad_heads(w2_b, A_HEADS, HEAD_W))
    b2 = jnp.stack([_pad_heads(b2_f, A_HEADS, HEAD_W), _pad_heads(b2_b, A_HEADS, HEAD_W)]).reshape(2, 1, hw)
    return w, w2.astype(BF16), b2


QKV_W = 2 * C_HEADS * C_DK + C_HEADS * C_DV
OG_W = C_HEADS * C_DV
DQ_W = D_HEADS * D_HD
DKV_W = D_KV_HEADS * HEAD_W


def _proj1_kernel(x_ref, mod_ref, nw_ref, w_ref, cos_ref, sin_ref,
                  qkv_ref, gb_ref, og_ref, dq_ref, dk_ref, dv_ref, *, n_batch, d):
    b, j = pl.program_id(0), pl.program_id(1)
    shift, scale = _mod_rows(mod_ref, b, j, n_batch, d, 0, 2)
    h = _norm_mod(x_ref[0], nw_ref[...], shift, scale).astype(BF16)

    def mm(c0, c1):
        return jnp.dot(h, w_ref[:, c0:c1], preferred_element_type=F32)

    c = 0
    qkv_ref[0] = mm(c, c + QKV_W); c += QKV_W
    gb_ref[0] = mm(c, c + HEAD_W); c += HEAD_W
    og_ref[0] = mm(c, c + OG_W); c += OG_W
    cos, sin = cos_ref[...], sin_ref[...]
    for s in range(DQ_W // HEAD_W):
        xs = mm(c + s * HEAD_W, c + (s + 1) * HEAD_W)
        xp = mm(c + DQ_W + s * HEAD_W, c + DQ_W + (s + 1) * HEAD_W)
        dq_ref[0, :, s * HEAD_W:(s + 1) * HEAD_W] = (xs * cos + xp * sin) * (D_HD ** -0.5)
    c += 2 * DQ_W
    for s in range(DKV_W // HEAD_W):
        xs = mm(c + s * HEAD_W, c + (s + 1) * HEAD_W)
        xp = mm(c + DKV_W + s * HEAD_W, c + DKV_W + (s + 1) * HEAD_W)
        dk_ref[0, :, s * HEAD_W:(s + 1) * HEAD_W] = xs * cos + xp * sin
    c += 2 * DKV_W
    dv_ref[0] = mm(c, c + DKV_W)


def _proj1(x, mod, nw, w, cos, sin, n_batch):
    bsz, l, d = x.shape
    widths = (QKV_W, HEAD_W, OG_W, DQ_W, DKV_W, DKV_W)
    return pl.pallas_call(
        functools.partial(_proj1_kernel, n_batch=n_batch, d=d),
        grid=(bsz, l // TM),
        in_specs=[pl.BlockSpec((1, TM, d), lambda b, j: (b, j, 0)),
                  _const_spec(mod.shape), _const_spec(nw.shape), _const_spec(w.shape),
                  pl.BlockSpec((TM, HEAD_W), lambda b, j: (j, 0)),
                  pl.BlockSpec((TM, HEAD_W), lambda b, j: (j, 0))],
        out_specs=[pl.BlockSpec((1, TM, wd), lambda b, j: (b, j, 0)) for wd in widths],
        out_shape=[jax.ShapeDtypeStruct((bsz, l, wd), F32) for wd in widths],
        compiler_params=_cp(("parallel", "arbitrary")),
        name="proj1",
    )(x, mod, nw, w, cos, sin)


HALO = 8


def _conv1_kernel(cur_ref, prev_ref, next_ref, cw_ref, gbr_ref, alog_ref, dtb_ref,
                  q_ref, k_ref, v_ref, gb_ref, xe_ref):
    j, n_blk = pl.program_id(1), pl.num_programs(1)
    prev_ok = j >= 2
    next_ok = jnp.logical_and(j >= 1, j < n_blk - 1)
    xe_ref[0:HALO, :] = jnp.where(prev_ok, prev_ref[0], 0.0)
    xe_ref[HALO:HALO + TM, :] = cur_ref[0]
    xe_ref[HALO + TM:2 * HALO + TM, :] = jnp.where(next_ok, next_ref[0], 0.0)
    pad = CONV_W // 2
    hw = C_HEADS * HEAD_W
    for g, o_ref in enumerate((q_ref, k_ref, v_ref)):
        cols = slice(g * hw, (g + 1) * hw)
        acc = cw_ref[0:1, cols] * xe_ref[HALO - pad:HALO - pad + TM, cols]
        for i in range(1, CONV_W):
            acc = acc + cw_ref[i:i + 1, cols] * xe_ref[HALO - pad + i:HALO - pad + i + TM, cols]
        y = _silu(acc)
        if g < 2:
            parts = []
            for hh in range(C_HEADS):
                yh = y[:, hh * HEAD_W:(hh + 1) * HEAD_W]
                yh = yh * lax.rsqrt(jnp.sum(yh * yh, axis=-1, keepdims=True) + EPS)
                parts.append(yh * (C_DK ** -0.5) if g == 0 else yh)
            y = jnp.concatenate(parts, axis=-1)
        o_ref[0] = y
    raw = gbr_ref[0]
    z = raw + dtb_ref[...]
    softplus = jnp.maximum(z, 0.0) + jnp.log1p(jnp.exp(-jnp.abs(z)))
    lane = lax.broadcasted_iota(jnp.int32, raw.shape, 1)
    gb_ref[0] = jnp.where(lane < 2 * C_HEADS, jax.nn.sigmoid(raw), -jnp.exp(alog_ref[...]) * softplus)


def _conv1(qkv, conv_w, gb_raw, alog, dtb):
    bsz, l, wd = qkv.shape
    n_blk = l // TM
    per = TM // HALO
    last = l // HALO - 1
    hw = C_HEADS * HEAD_W
    ob = pl.BlockSpec((1, TM, hw), lambda b, j: (b, j, 0))
    gbs = pl.BlockSpec((1, TM, HEAD_W), lambda b, j: (b, j, 0))
    return pl.pallas_call(
        _conv1_kernel,
        grid=(bsz, n_blk),
        in_specs=[pl.BlockSpec((1, TM, wd), lambda b, j: (b, j, 0)),
                  pl.BlockSpec((1, HALO, wd), lambda b, j: (b, jnp.maximum(j * per - 1, 0), 0)),
                  pl.BlockSpec((1, HALO, wd), lambda b, j: (b, jnp.minimum((j + 1) * per, last), 0)),
                  _const_spec(conv_w.shape), gbs, _const_spec(alog.shape), _const_spec(dtb.shape)],
        out_specs=[ob, ob, ob, gbs],
        out_shape=[jax.ShapeDtypeStruct((bsz, l, hw), F32)] * 3 + [jax.ShapeDtypeStruct((bsz, l, HEAD_W), F32)],
        scratch_shapes=[pltpu.VMEM((TM + 2 * HALO, wd), F32)],
        compiler_params=_cp(("parallel", "arbitrary")),
        name="conv1",
    )(qkv, qkv, qkv, conv_w, gb_raw, alog, dtb)


_MERGE = (16, 32, 64)
_BASE = 8


def _delta_masks():
    t = np.arange(CHUNK)[:, None]
    s = np.arange(CHUNK)[None, :]
    out = [s <= t, s < t, s >= t, s > t, (t // _BASE) == (s // _BASE)]
    for m2 in _MERGE:
        out.append(((t // m2) == (s // m2)) & ((t // (m2 // 2)) != (s // (m2 // 2))))
    return np.stack(out).astype(np.float32)


def _bdot(a, b):
    return jnp.dot(a.astype(BF16), b.astype(BF16), preferred_element_type=F32)


DELTA_HEADS = 4


def _delta_prepare(chains, dm):
    n = range(len(chains))
    qs, ks, vs, gs, betas, revs = zip(*chains)
    incl = [dm[2] if r else dm[0] for r in revs]
    strict = [dm[3] if r else dm[1] for r in revs]
    cum = [_cumsum_rows(jnp.broadcast_to(gs[i], qs[i].shape), revs[i]) for i in n]
    decay = [incl[i] * jnp.exp(jnp.where(incl[i] > 0, cum[i][:, 0:CHUNK] - cum[i].T[0:CHUNK, :], 0.0)) for i in n]
    kbeta = [ks[i] * betas[i] for i in n]
    kbf = [ks[i].astype(BF16) for i in n]
    a = [strict[i] * _dot_nt(kbeta[i].astype(BF16), kbf[i]) * decay[i] for i in n]
    eye = dm[0] * dm[2]
    n0 = [-(a[i] * dm[4]) for i in n]
    n2 = [_bdot(n0[i], n0[i]) for i in n]
    n4 = [_bdot(n2[i], n2[i]) for i in n]
    t = [eye + n0[i] for i in n]
    t = [t[i] + _bdot(t[i], n2[i]) for i in n]
    t = [t[i] + _bdot(t[i], n4[i]) for i in n]
    for li in range(len(_MERGE)):
        inner = [_bdot(a[i] * dm[5 + li], t[i]) for i in n]
        t = [t[i] - _bdot(t[i], inner[i]) for i in n]
    uw = [_bdot(t[i], jnp.concatenate([vs[i] * betas[i], kbeta[i] * jnp.exp(cum[i])], axis=1)) for i in n]
    scores = [(_dot_nt(qs[i].astype(BF16), kbf[i]) * decay[i]).astype(BF16) for i in n]
    edge = [cum[i][0:1] if revs[i] else cum[i][CHUNK - 1:CHUNK] for i in n]
    q_in = [(qs[i] * jnp.exp(cum[i])).astype(BF16) for i in n]
    k_out = [(ks[i] * jnp.exp(edge[i] - cum[i])).astype(BF16) for i in n]
    carry = [jnp.exp(edge[i]) for i in n]
    return [(uw[i][:, 0:HEAD_W], uw[i][:, HEAD_W:].astype(BF16), scores[i], q_in[i], k_out[i], carry[i]) for i in n]


def _delta_step(prep, states):
    n = range(len(prep))
    sb = [states[i].astype(BF16) for i in n]
    v_new = [prep[i][0] - jnp.dot(prep[i][1], sb[i], preferred_element_type=F32) for i in n]
    vb = [v_new[i].astype(BF16) for i in n]
    o = [jnp.dot(prep[i][3], sb[i], preferred_element_type=F32)
         + jnp.dot(prep[i][2], vb[i], preferred_element_type=F32) for i in n]
    new = [states[i] * prep[i][5] + _dot_tn(prep[i][4], vb[i]) for i in n]
    return o, new


def _lane_col(x, lane_idx):
    lane = lax.broadcasted_iota(jnp.int32, x.shape, 1)
    return jnp.sum(jnp.where(lane == lane_idx, x, 0.0), axis=-1, keepdims=True)


def _delta_kernel(dm_ref, qf_ref, kf_ref, vf_ref, gbf_ref, qb_ref, kb_ref, vb_ref, gbb_ref,
                  of_ref, ob_ref, st_ref):
    @pl.when(pl.program_id(2) == 0)
    def _():
        st_ref[...] = jnp.zeros_like(st_ref)

    dm = [dm_ref[i] for i in range(dm_ref.shape[0])]
    n_ch = TM // CHUNK
    gbf, gbb = gbf_ref[0], gbb_ref[0]
    chains = []
    cols = []
    for hh in range(DELTA_HEADS):
        h = pl.program_id(1) * DELTA_HEADS + hh
        cols.append((_lane_col(gbf, h), _lane_col(gbf, 2 * C_HEADS + h),
                     _lane_col(gbb, C_HEADS + h), _lane_col(gbb, 3 * C_HEADS + h)))
    for c in range(n_ch):
        rf = slice(c * CHUNK, (c + 1) * CHUNK)
        rb = slice((n_ch - 1 - c) * CHUNK, (n_ch - c) * CHUNK)
        for hh in range(DELTA_HEADS):
            lanes = slice(hh * HEAD_W, (hh + 1) * HEAD_W)
            beta_f, g_f, beta_b, g_b = cols[hh]
            chains.append((qf_ref[0, rf, lanes], kf_ref[0, rf, lanes], vf_ref[0, rf, lanes], g_f[rf], beta_f[rf], False))
            chains.append((qb_ref[0, rb, lanes], kb_ref[0, rb, lanes], vb_ref[0, rb, lanes], g_b[rb], beta_b[rb], True))
    prep = _delta_prepare(chains, dm)
    per = 2 * DELTA_HEADS
    states = [st_ref[i] for i in range(per)]
    for c in range(n_ch):
        rf = slice(c * CHUNK, (c + 1) * CHUNK)
        rb = slice((n_ch - 1 - c) * CHUNK, (n_ch - c) * CHUNK)
        outs, states = _delta_step(prep[c * per:(c + 1) * per], states)
        for hh in range(DELTA_HEADS):
            lanes = slice(hh * HEAD_W, (hh + 1) * HEAD_W)
            of_ref[0, rf, lanes] = outs[2 * hh]
            ob_ref[0, rb, lanes] = outs[2 * hh + 1]
    for i in range(per):
        st_ref[i] = states[i]


def _delta(dmasks, q, k, v, gb):
    bsz, l, wide = q.shape
    n_blk = l // TM
    bw = DELTA_HEADS * HEAD_W
    fwd = pl.BlockSpec((1, TM, bw), lambda b, h, i: (b, i, h))
    bwd = pl.BlockSpec((1, TM, bw), lambda b, h, i: (b, _bwd_block(i, n_blk), h))
    gf = pl.BlockSpec((1, TM, HEAD_W), lambda b, h, i: (b, i, 0))
    gbw = pl.BlockSpec((1, TM, HEAD_W), lambda b, h, i: (b, _bwd_block(i, n_blk), 0))
    return pl.pallas_call(
        _delta_kernel,
        grid=(bsz, wide // bw, n_blk),
        in_specs=[_const_spec(dmasks.shape), fwd, fwd, fwd, gf, bwd, bwd, bwd, gbw],
        out_specs=[fwd, bwd],
        out_shape=[jax.ShapeDtypeStruct((bsz, l, wide), F32)] * 2,
        scratch_shapes=[pltpu.VMEM((2 * DELTA_HEADS, HEAD_W, HEAD_W), F32)],
        compiler_params=_cp(("parallel", "parallel", "arbitrary")),
        name="delta",
    )(dmasks, q, k, v, gb, q, k, v, gb)


def _attn_kernel(q_ref, kp_ref, kc_ref, kn_ref, vp_ref, vc_ref, vn_ref, kx_ref, vx_ref, sink_ref, o_ref, *, t_len):
    i = pl.program_id(1)
    q = q_ref[0]
    kl = jnp.concatenate([kp_ref[0], kc_ref[0], kn_ref[0]], axis=0).astype(BF16)
    vl = jnp.concatenate([vp_ref[0], vc_ref[0], vn_ref[0]], axis=0).astype(BF16)
    kx, vx = kx_ref[0].astype(BF16), vx_ref[0].astype(BF16)
    qpos = lax.broadcasted_iota(jnp.int32, (WINDOW, 3 * WINDOW), 0)
    kpos = lax.broadcasted_iota(jnp.int32, (WINDOW, 3 * WINDOW), 1) - WINDOW
    k_abs = i * WINDOW + kpos
    valid = (jnp.abs(kpos - qpos) <= WINDOW) & (k_abs >= 0) & (k_abs < t_len)
    low = lax.broadcasted_iota(jnp.int32, (WINDOW, HEAD_W), 1) < D_HD
    group = D_HEADS // D_KV_HEADS
    heads = range(D_HEADS)
    cols = [slice((h // group) * HEAD_W, (h // group + 1) * HEAD_W) for h in heads]
    qm = [jnp.where(low if h % 2 == 0 else jnp.logical_not(low), q[:, (h // 2) * HEAD_W:(h // 2 + 1) * HEAD_W], 0.0)
          .astype(BF16) for h in heads]
    s_l = [jnp.where(valid, _dot_nt(qm[h], kl[:, cols[h]]), -jnp.inf) for h in heads]
    s_x = [_dot_nt(qm[h], kx[:, cols[h]]) for h in heads]
    sink = [sink_ref[:, h:h + 1] for h in heads]
    def fold(a, b, op):
        slabs = [a[:, c:c + HEAD_W] for c in range(0, a.shape[1], HEAD_W)]
        slabs += [b[:, c:c + HEAD_W] for c in range(0, b.shape[1], HEAD_W)]
        acc = slabs[0]
        for s in slabs[1:]:
            acc = op(acc, s)
        return acc

    m = [jnp.maximum(jnp.max(fold(s_l[h], s_x[h], jnp.maximum), axis=-1, keepdims=True), sink[h]) for h in heads]
    p_l = [jnp.exp(s_l[h] - m[h]) for h in heads]
    p_x = [jnp.exp(s_x[h] - m[h]) for h in heads]
    den = [jnp.sum(fold(p_l[h], p_x[h], jnp.add), axis=-1, keepdims=True) + jnp.exp(sink[h] - m[h]) for h in heads]
    o = [(jnp.dot(p_l[h].astype(BF16), vl[:, cols[h]], preferred_element_type=F32)
          + jnp.dot(p_x[h].astype(BF16), vx[:, cols[h]], preferred_element_type=F32)) / den[h] for h in heads]
    for p in range(D_HEADS // 2):
        o_ref[0, :, p * HEAD_W:(p + 1) * HEAD_W] = jnp.where(low, o[2 * p], o[2 * p + 1])


def _attn(dq, dk, dv, sinks, n_ctx):
    bsz, l, _ = dq.shape
    t_len = l - n_ctx
    nq = t_len // WINDOW
    off = n_ctx // WINDOW

    def kv(delta):
        return pl.BlockSpec((1, WINDOW, DKV_W), lambda b, i: (b, off + jnp.clip(i + delta, 0, nq - 1), 0))

    ctx = pl.BlockSpec((1, n_ctx, DKV_W), lambda b, i: (b, 0, 0))
    return pl.pallas_call(
        functools.partial(_attn_kernel, t_len=t_len),
        grid=(bsz, nq),
        in_specs=[pl.BlockSpec((1, WINDOW, DQ_W), lambda b, i: (b, off + i, 0)),
                  kv(-1), kv(0), kv(1), kv(-1), kv(0), kv(1), ctx, ctx, _const_spec(sinks.shape)],
        out_specs=pl.BlockSpec((1, WINDOW, DQ_W), lambda b, i: (b, i, 0)),
        out_shape=jax.ShapeDtypeStruct((bsz, t_len, DQ_W), F32),
        compiler_params=_cp(("parallel", "arbitrary")),
        name="attn",
    )(dq, dk, dk, dk, dv, dv, dv, dk, dv, sinks)


def _read1_kernel(x_ref, of_ref, ob_ref, og_ref, od_ref, nw_ref, wout_ref, mod_ref, xo_ref, *, n_batch, d):
    b, j = pl.program_id(0), pl.program_id(1) + 1
    (g1,) = _mod_rows(mod_ref, b, j, n_batch, d, 2, 1)
    o = of_ref[0] + ob_ref[0]
    parts = [_head_rms(o[:, hh * HEAD_W:(hh + 1) * HEAD_W], nw_ref[...]) for hh in range(C_HEADS)]
    y = jnp.concatenate(parts, axis=-1) * _silu(og_ref[0])
    y = jnp.concatenate([y, od_ref[0]], axis=-1)
    xo_ref[0] = x_ref[0] + g1 * jnp.dot(y.astype(BF16), wout_ref[...], preferred_element_type=F32)


def _read1(x, of, ob, og, od, nw, wout, mod, n_batch):
    bsz, l, d = x.shape
    lat = lambda wd: pl.BlockSpec((1, TM, wd), lambda b, j: (b, j + 1, 0))
    return pl.pallas_call(
        functools.partial(_read1_kernel, n_batch=n_batch, d=d),
        grid=(bsz, l // TM - 1),
        in_specs=[lat(d), lat(OG_W), lat(OG_W), lat(OG_W),
                  pl.BlockSpec((1, TM, DQ_W), lambda b, j: (b, j, 0)),
                  _const_spec(nw.shape), _const_spec(wout.shape), _const_spec(mod.shape)],
        out_specs=lat(d),
        out_shape=jax.ShapeDtypeStruct(x.shape, F32),
        input_output_aliases={0: 0},
        compiler_params=_cp(("parallel", "arbitrary")),
        name="read1",
    )(x, of, ob, og, od, nw, wout, mod)


def _swap_halves(w, n_heads):
    lead = w.shape[:-1]
    dh = w.shape[-1] // n_heads
    w = w.reshape(lead + (n_heads, 2, dh // 2))
    return w[..., ::-1, :].reshape(lead + (n_heads * dh,))


def _dup_heads(w, n_heads):
    lead = w.shape[:-1]
    dh = w.shape[-1] // n_heads
    w = w.reshape(lead + (n_heads, 1, dh))
    return jnp.concatenate([w, w], axis=-2).reshape(lead + (2 * n_heads * dh,))


def _layer1_weights(w_in):
    d = w_in.shape[0]
    sizes = (C_HEADS * C_DK, C_HEADS * C_DK, C_HEADS * C_DV, C_HEADS, C_HEADS, C_HEADS, C_HEADS,
             C_HEADS * C_DV, D_HEADS * D_HD, D_KV_HEADS * D_HD, D_KV_HEADS * D_HD)
    cq, ck, cv, bt_f, bt_b, a_f, a_b, og, dq, dk, dv = jnp.split(w_in, np.cumsum(sizes)[:-1].tolist(), axis=1)
    gates = jnp.concatenate([bt_f, bt_b, a_f, a_b, jnp.zeros((d, HEAD_W - 4 * C_HEADS), w_in.dtype)], axis=1)
    return jnp.concatenate([cq, ck, cv, gates, og, dq, _swap_halves(dq, D_HEADS),
                            _dup_heads(dk, D_KV_HEADS), _dup_heads(_swap_halves(dk, D_KV_HEADS), D_KV_HEADS),
                            _dup_heads(dv, D_KV_HEADS)], axis=1).astype(BF16)


def _rope_tables(n_ctx, t_len):
    rows = t_len // GRID_W
    row = jnp.repeat(jnp.arange(rows, dtype=F32), GRID_W)
    col = jnp.tile(jnp.arange(GRID_W, dtype=F32), rows)
    n_freq = D_HD // 4
    inv = ROPE_BASE ** (-jnp.arange(n_freq, dtype=F32) / n_freq)
    ang = jnp.concatenate([row[:, None] * inv, col[:, None] * inv], axis=-1)
    cos, sin = jnp.cos(ang), jnp.sin(ang)
    cos = jnp.concatenate([jnp.ones((n_ctx, D_HD // 2), F32), cos], axis=0)
    sin = jnp.concatenate([jnp.zeros((n_ctx, D_HD // 2), F32), sin], axis=0)
    return jnp.concatenate([cos] * 4, axis=1), jnp.concatenate([-sin, sin] * 2, axis=1)


def _mixer1(xx, mod, nw, w_in, w_out, conv_w, a_log_f, dt_bias_f, a_log_b, dt_bias_b, dn_norm_w, sinks, n_batch, n_ctx):
    l = xx.shape[1]
    cos, sin = _rope_tables(n_ctx, l - n_ctx)
    qkv, gb_raw, og, dq, dk, dv = _proj1(xx, mod, nw, _layer1_weights(w_in), cos, sin, n_batch)
    zero4 = jnp.zeros((C_HEADS,), F32)
    fill = jnp.zeros((HEAD_W - 4 * C_HEADS,), F32)
    alog = jnp.concatenate([zero4, zero4, a_log_f, a_log_b, fill]).reshape(1, HEAD_W)
    dtb = jnp.concatenate([zero4, zero4, dt_bias_f, dt_bias_b, fill]).reshape(1, HEAD_W)
    cq, ck, cv, gb = _conv1(qkv, conv_w, gb_raw, alog, dtb)
    of, ob = _delta(jnp.asarray(_delta_masks()), cq, ck, cv, gb)
    sink_row = jnp.concatenate([sinks, jnp.zeros((HEAD_W - D_HEADS,), F32)]).reshape(1, HEAD_W)
    od = _attn(dq, dk, dv, sink_row, n_ctx)
    return _read1(xx, of, ob, og, od, dn_norm_w.reshape(1, -1), w_out.astype(BF16), mod, n_batch)


def kernel(x, c, ctx, c_ctx, l0_ada_w, l0_ada_b, l0_norm_mix_w, l0_w_in, l0_w_out, l0_gla_w2_f, l0_gla_b_f, l0_gla_w2_b, l0_gla_b_b, l0_gla_norm_w, l0_hgrn_norm_w, hgrn_lb_logits, l0_norm_ffn_w, l0_router_w, l0_router_b, l0_w_up, l0_b_up, l0_w_down, l0_b_down, l1_ada_w, l1_ada_b, l1_norm_mix_w, l1_w_in, l1_w_out, l1_conv_w, l1_a_log_f, l1_dt_bias_f, l1_a_log_b, l1_dt_bias_b, l1_dn_norm_w, l1_sinks, l1_norm_ffn_w, l1_router_w, l1_router_b, l1_w_up, l1_b_up, l1_w_down, l1_b_down, final_norm_w):
    bsz, t, d = x.shape
    n_ctx = ctx.shape[1]
    assert n_ctx == TM and t % TM == 0
    xx = jnp.concatenate([ctx, x], axis=1)
    mod_rows = -(-(bsz + 1) // 8) * 8
    c_all = jnp.zeros((mod_rows, d), F32).at[:bsz].set(c).at[bsz].set(c_ctx)
    ones = jnp.ones((1, d), F32)

    mod0 = _ada_table(c_all, l0_ada_w, l0_ada_b)
    w0, w2, b2 = _layer0_weights(l0_w_in, l0_gla_w2_f, l0_gla_b_f, l0_gla_w2_b, l0_gla_b_b)
    q, kf, kb, v, lff, lfb, og = _proj0(xx, mod0, l0_norm_mix_w.reshape(1, d), w0, w2, b2, hgrn_lb_logits, bsz)
    of, ob = _scan0(jnp.asarray(_level_masks()), jnp.asarray(_tri3(), BF16), q, kf, kb, v, lff, lfb)
    xx = _read0(xx, of, ob, og, l0_gla_norm_w.reshape(1, -1), l0_hgrn_norm_w.reshape(1, -1), l0_w_out.astype(BF16), mod0, bsz)
    xx = _moe_blocks(xx, mod0, l0_norm_ffn_w.reshape(1, d), l0_router_w, l0_router_b, l0_w_up, l0_b_up, l0_w_down, l0_b_down,
              ones, bsz, 0, False)

    mod1 = _ada_table(c_all, l1_ada_w, l1_ada_b)
    xx = _mixer1(xx, mod1, l1_norm_mix_w.reshape(1, d), l1_w_in, l1_w_out, l1_conv_w, l1_a_log_f, l1_dt_bias_f,
                 l1_a_log_b, l1_dt_bias_b, l1_dn_norm_w, l1_sinks, bsz, n_ctx)
    return _moe_blocks(xx, mod1, l1_norm_ffn_w.reshape(1, d), l1_router_w, l1_router_b, l1_w_up, l1_b_up, l1_w_down, l1_b_down,
                final_norm_w.reshape(1, d), bsz, 1, True)
```

```python
import functools

import numpy as np
import jax
import jax.numpy as jnp
from jax import lax
from jax.experimental import pallas as pl
from jax.experimental.pallas import tpu as pltpu

F32 = jnp.float32
BF16 = jnp.bfloat16
EPS = 1e-6

CHUNK = 64
A_HEADS, A_DK, A_DV, A_RANK = 4, 64, 128, 16
GATE_TAU = 16.0
B_HEADS, B_DK, B_DV = 4, 128, 128
C_HEADS, C_DK, C_DV = 4, 128, 128
CONV_W = 5
D_HEADS, D_KV_HEADS, D_HD = 8, 2, 64
WINDOW = 128
GRID_W = 64
ROPE_BASE = 10000.0
TOP_K = 4
SWIGLU_LIMIT = 7.0
SWIGLU_ALPHA = 1.702

TM = 256
HEAD_W = 128
EXPERT_TILE = 512
FF_CHUNK = 512
VMEM_LIMIT = 48 * 1024 * 1024


def _cp(sem, vmem=VMEM_LIMIT):
    return pltpu.CompilerParams(dimension_semantics=sem, vmem_limit_bytes=vmem)


def _const_spec(shape):
    nd = len(shape)
    return pl.BlockSpec(shape, lambda *_: (0,) * nd)


def _silu(x):
    return x * jax.nn.sigmoid(x)


def _log1p_exp_neg_abs(z):
    return jnp.log(1.0 + jnp.exp(-jnp.abs(z)))


def _log_sigmoid(z):
    return jnp.minimum(z, 0.0) - _log1p_exp_neg_abs(z)


def _norm_mod(x, nw, shift, scale):
    y = x * lax.rsqrt(jnp.mean(x * x, axis=-1, keepdims=True) + EPS) * nw
    return y * (1.0 + scale) + shift


def _mod_rows(mod_ref, batch, blk, n_batch, d, first, count):
    row = jnp.where(blk == 0, n_batch, batch)
    return [mod_ref[pl.ds(row, 1), (first + i) * d:(first + i + 1) * d] for i in range(count)]


def _ada_kernel(c_ref, w_ref, b_ref, o_ref):
    s = _silu(c_ref[...])
    o_ref[...] = jnp.dot(s, w_ref[...], precision=lax.Precision.HIGHEST,
                         preferred_element_type=F32) + b_ref[...]


def _ada_table(c_all, w, b):
    rows, d = c_all.shape
    n = w.shape[1]
    bn = d
    return pl.pallas_call(
        _ada_kernel,
        grid=(n // bn,),
        in_specs=[pl.BlockSpec((rows, d), lambda j: (0, 0)),
                  pl.BlockSpec((d, bn), lambda j: (0, j)),
                  pl.BlockSpec((1, bn), lambda j: (0, j))],
        out_specs=pl.BlockSpec((rows, bn), lambda j: (0, j)),
        out_shape=jax.ShapeDtypeStruct((rows, n), F32),
        compiler_params=_cp(("arbitrary",)),
        name="ada_table",
    )(c_all, w, b.reshape(1, n))


def _proj0_kernel(x_ref, mod_ref, nw_ref, w_ref, w2_ref, b2_ref, lbl_ref,
                  q_ref, kf_ref, kb_ref, v_ref, lff_ref, lfb_ref, og_ref, *, n_batch, d):
    b, j = pl.program_id(0), pl.program_id(1)
    shift, scale = _mod_rows(mod_ref, b, j, n_batch, d, 0, 2)
    h = _norm_mod(x_ref[0], nw_ref[...], shift, scale).astype(BF16)

    def mm(c0, c1):
        return jnp.dot(h, w_ref[:, c0:c1], preferred_element_type=F32)

    hw = A_HEADS * HEAD_W

    def plain_outputs():
        q_ref[0, :, 0:hw] = mm(0, hw) * (A_DK ** -0.5)
        q_ref[0, :, hw:2 * hw] = mm(hw, 2 * hw)
        v_ref[0] = mm(2 * hw, 4 * hw)
        og_ref[0] = mm(4 * hw, 6 * hw)
        kg = mm(6 * hw, 7 * hw)
        kf_ref[0, :, 0:hw] = kg
        kb_ref[0, :, 0:hw] = kg

    lg = lbl_ref[...]
    e = jnp.exp(lg - jnp.max(lg, axis=0, keepdims=True))
    lb = e[0:1] / jnp.sum(e, axis=0, keepdims=True)
    log_lb, log_1m = jnp.log(lb), jnp.log1p(-lb)
    ar = mm(9 * hw, 9 * hw + HEAD_W).astype(BF16)
    for di, (k_ref, lf_ref) in enumerate(((kf_ref, lff_ref), (kb_ref, lfb_ref))):
        z = mm((7 + di) * hw, (8 + di) * hw)
        s1 = log_1m + _log_sigmoid(z)
        lf_ref[0, :, hw:2 * hw] = jnp.maximum(log_lb, s1) + _log1p_exp_neg_abs(log_lb - s1)
        k_ref[0, :, hw:2 * hw] = (1.0 - lb) * jax.nn.sigmoid(-z)
        za = jnp.dot(ar, w2_ref[di], preferred_element_type=F32) + b2_ref[di]
        lf_ref[0, :, 0:hw] = _log_sigmoid(za) * (1.0 / GATE_TAU)
    plain_outputs()


def _proj0(x, mod, nw, w, w2, b2, lbl, n_batch):
    bsz, l, d = x.shape
    n_blk = l // TM
    wide = 2 * A_HEADS * HEAD_W
    blk = pl.BlockSpec((1, TM, wide), lambda b, j: (b, j, 0))
    return pl.pallas_call(
        functools.partial(_proj0_kernel, n_batch=n_batch, d=d),
        grid=(bsz, n_blk),
        in_specs=[pl.BlockSpec((1, TM, d), lambda b, j: (b, j, 0)),
                  _const_spec(mod.shape), _const_spec(nw.shape), _const_spec(w.shape),
                  _const_spec(w2.shape), _const_spec(b2.shape), _const_spec(lbl.shape)],
        out_specs=[blk] * 7,
        out_shape=[jax.ShapeDtypeStruct((bsz, l, wide), F32)] * 7,
        compiler_params=_cp(("parallel", "arbitrary")),
        name="proj0",
    )(x, mod, nw, w, w2, b2, lbl)


_LEVELS = (32, 16, 8, 4, 2, 1)


def _level_masks():
    t = np.arange(CHUNK)[:, None]
    s = np.arange(CHUNK)[None, :]
    out = np.zeros((2, len(_LEVELS) + 1, CHUNK, CHUNK), np.float32)
    for li, m in enumerate(_LEVELS):
        same = (t // (2 * m)) == (s // (2 * m))
        fwd = same & (t % (2 * m) >= m) & (s % (2 * m) < m)
        out[0, li] = fwd
        out[1, li] = fwd.T
    out[:, -1] = np.eye(CHUNK)
    return out


def _cumsum_rows(x, reverse):
    n = x.shape[0]
    r = lax.broadcasted_iota(jnp.int32, x.shape, 0)
    sh = 1
    while sh < n:
        if reverse:
            x = x + jnp.where(r < n - sh, pltpu.roll(x, n - sh, 0), 0.0)
        else:
            x = x + jnp.where(r >= sh, pltpu.roll(x, sh, 0), 0.0)
        sh *= 2
    return x


def _level_ref(cum, m, reverse):
    n = cum.shape[0]
    tgt = m if reverse else m - 1
    if 2 * m >= 8:
        parts = [jnp.broadcast_to(cum[g + tgt:g + tgt + 1, :], (2 * m, cum.shape[1]))
                 for g in range(0, n, 2 * m)]
        return parts[0] if len(parts) == 1 else jnp.concatenate(parts, axis=0)
    pos = lax.broadcasted_iota(jnp.int32, cum.shape, 0) % (2 * m)
    out = cum
    for p in range(2 * m):
        if p == tgt:
            continue
        shift = (p - tgt) % n
        out = jnp.where(pos == p, pltpu.roll(cum, shift, 0), out)
    return out


def _dot_nt(a, b):
    return lax.dot_general(a, b, (((1,), (1,)), ((), ())), preferred_element_type=F32)


def _dot_tn(a, b):
    return lax.dot_general(a, b, (((0,), (0,)), ((), ())), preferred_element_type=F32)


def _tri3():
    t = np.arange(CHUNK)[:, None]
    s = np.arange(CHUNK)[None, :]
    return np.stack([np.tile(s <= t, (1, 3)), np.tile(s >= t, (1, 3))]).astype(np.float32)


def _cumsum_split(x, tri):
    hi = x.astype(BF16)
    r1 = x - hi.astype(F32)
    mid = r1.astype(BF16)
    lo = (r1 - mid.astype(F32)).astype(BF16)
    return jnp.dot(tri, jnp.concatenate([hi, mid, lo], axis=0), preferred_element_type=F32)


def _gated_prepare(chains, masks, tri3):
    n = range(len(chains))
    qs, ks, vs, lfs, revs = zip(*chains)
    cum = [_cumsum_split(lfs[i], tri3[revs[i]]) for i in n]
    qb = [qs[i].astype(BF16) for i in n]
    kb = [ks[i].astype(BF16) for i in n]
    vb = [vs[i].astype(BF16) for i in n]
    scores = [masks[revs[i]][len(_LEVELS)] * _dot_nt(qb[i], kb[i]) for i in n]
    for li, m in enumerate(_LEVELS):
        w = [jnp.exp(-jnp.abs(cum[i] - _level_ref(cum[i], m, revs[i]))).astype(BF16) for i in n]
        part = [_dot_nt(qb[i] * w[i], kb[i] * w[i]) for i in n]
        scores = [scores[i] + masks[revs[i]][li] * part[i] for i in n]
    intra = [jnp.dot(scores[i].astype(BF16), vb[i], preferred_element_type=F32) for i in n]
    q_in = [(qs[i] * jnp.exp(cum[i])).astype(BF16) for i in n]
    edge = [cum[i][0:1] if revs[i] else cum[i][CHUNK - 1:CHUNK] for i in n]
    update = [_dot_tn(vb[i], (ks[i] * jnp.exp(edge[i] - cum[i])).astype(BF16)) for i in n]
    carry = [jnp.exp(edge[i]) for i in n]
    return [(intra[i], q_in[i], update[i], carry[i]) for i in n]


SCAN_HEADS = 4


def _scan0_kernel(m_ref, tri_ref, qf_ref, kf_ref, vf_ref, lff_ref, qb_ref, kb_ref, vb_ref, lfb_ref,
                  of_ref, ob_ref, st_ref):
    @pl.when(pl.program_id(2) == 0)
    def _():
        st_ref[...] = jnp.zeros_like(st_ref)

    masks = [[m_ref[r, i] for i in range(len(_LEVELS) + 1)] for r in range(2)]
    tri3 = [tri_ref[0], tri_ref[1]]
    n_ch = TM // CHUNK
    fwd_rows = [slice(c * CHUNK, (c + 1) * CHUNK) for c in range(n_ch)]
    bwd_rows = [slice((n_ch - 1 - c) * CHUNK, (n_ch - c) * CHUNK) for c in range(n_ch)]
    lanes = [slice(hh * HEAD_W, (hh + 1) * HEAD_W) for hh in range(SCAN_HEADS)]
    chains = []
    for c in range(n_ch):
        rf, rb = fwd_rows[c], bwd_rows[c]
        for ln in lanes:
            chains.append((qf_ref[0, rf, ln], kf_ref[0, rf, ln], vf_ref[0, rf, ln], lff_ref[0, rf, ln], 0))
            chains.append((qb_ref[0, rb, ln], kb_ref[0, rb, ln], vb_ref[0, rb, ln], lfb_ref[0, rb, ln], 1))
    prep = _gated_prepare(chains, masks, tri3)
    per = 2 * SCAN_HEADS
    states = [st_ref[i] for i in range(per)]
    for c in range(n_ch):
        for hh, ln in enumerate(lanes):
            for r, (o_ref, rows) in enumerate(((of_ref, fwd_rows[c]), (ob_ref, bwd_rows[c]))):
                s = 2 * hh + r
                intra, q_in, update, carry = prep[c * per + s]
                o_ref[0, rows, ln] = intra + _dot_nt(q_in, states[s].astype(BF16))
                states[s] = states[s] * carry + update
    for i in range(per):
        st_ref[i] = states[i]


def _bwd_block(i, n_blk):
    return jnp.where(i == 0, 0, n_blk - i)


def _scan0(masks, tri3, q, kf, kb, v, lff, lfb):
    bsz, l, wide = q.shape
    n_blk = l // TM
    bw = SCAN_HEADS * HEAD_W
    fwd = pl.BlockSpec((1, TM, bw), lambda b, h, i: (b, i, h))
    bwd = pl.BlockSpec((1, TM, bw), lambda b, h, i: (b, _bwd_block(i, n_blk), h))
    return pl.pallas_call(
        _scan0_kernel,
        grid=(bsz, wide // bw, n_blk),
        in_specs=[_const_spec(masks.shape), _const_spec(tri3.shape), fwd, fwd, fwd, fwd, bwd, bwd, bwd, bwd],
        out_specs=[fwd, bwd],
        out_shape=[jax.ShapeDtypeStruct((bsz, l, wide), F32)] * 2,
        scratch_shapes=[pltpu.VMEM((2 * SCAN_HEADS, HEAD_W, HEAD_W), F32)],
        compiler_params=_cp(("parallel", "parallel", "arbitrary")),
        name="scan0",
    )(masks, tri3, q, kf, v, lff, q, kb, v, lfb)


def _head_rms(o, w):
    return o * lax.rsqrt(jnp.mean(o * o, axis=-1, keepdims=True) + EPS) * w


def _read0_kernel(x_ref, of_ref, ob_ref, og_ref, nwa_ref, nwb_ref, wout_ref, mod_ref, xo_ref, *, n_batch, d):
    b, j = pl.program_id(0), pl.program_id(1)
    (g1,) = _mod_rows(mod_ref, b, j, n_batch, d, 2, 1)
    o = of_ref[0] + ob_ref[0]
    parts = []
    for hh in range(A_HEADS + B_HEADS):
        nw = nwa_ref[...] if hh < A_HEADS else nwb_ref[...]
        parts.append(_head_rms(o[:, hh * HEAD_W:(hh + 1) * HEAD_W], nw))
    y = jnp.concatenate(parts, axis=-1) * _silu(og_ref[0])
    yo = jnp.dot(y.astype(BF16), wout_ref[...], preferred_element_type=F32)
    xo_ref[0] = x_ref[0] + g1 * yo


def _read0(x, of, ob, og, nwa, nwb, wout, mod, n_batch):
    bsz, l, d = x.shape
    wide = of.shape[-1]
    xb = pl.BlockSpec((1, TM, d), lambda b, j: (b, j, 0))
    wb = pl.BlockSpec((1, TM, wide), lambda b, j: (b, j, 0))
    return pl.pallas_call(
        functools.partial(_read0_kernel, n_batch=n_batch, d=d),
        grid=(bsz, l // TM),
        in_specs=[xb, wb, wb, wb, _const_spec(nwa.shape), _const_spec(nwb.shape),
                  _const_spec(wout.shape), _const_spec(mod.shape)],
        out_specs=xb,
        out_shape=jax.ShapeDtypeStruct(x.shape, F32),
        compiler_params=_cp(("parallel", "arbitrary")),
        name="read0",
    )(x, of, ob, og, nwa, nwb, wout, mod)


def _route_kernel(x_ref, mod_ref, nw_ref, rwt_ref, rb_ref, h_ref, idx_ref, gate_ref, *, n_batch, d, blk_off):
    b, j = pl.program_id(0), pl.program_id(1) + blk_off
    shift, scale = _mod_rows(mod_ref, b, j, n_batch, d, 3, 2)
    h = _norm_mod(x_ref[0], nw_ref[...], shift, scale)
    h_ref[0] = h
    logits = lax.dot_general(rwt_ref[...], h, (((1,), (1,)), ((), ())), precision=lax.Precision.HIGHEST,
                             preferred_element_type=F32) + rb_ref[...]
    n_exp = logits.shape[0]
    rows = lax.broadcasted_iota(jnp.int32, logits.shape, 0)
    vals, idxs = [], []
    for _ in range(TOP_K):
        m = jnp.max(logits, axis=0, keepdims=True)
        i = jnp.min(jnp.where(logits == m, rows, n_exp), axis=0, keepdims=True)
        vals.append(m)
        idxs.append(i)
        logits = jnp.where(rows == i, -jnp.inf, logits)
    ex = [jnp.exp(v - vals[0]) for v in vals]
    tot = ex[0] + ex[1] + ex[2] + ex[3]
    idx_ref[0] = jnp.concatenate(idxs, axis=0)
    gate_ref[0] = jnp.concatenate([e / tot for e in ex], axis=0)


def _route(x, mod, nw, rwt, rb, n_batch, blk_off):
    bsz, l, d = x.shape
    n_blk = l // TM - blk_off
    ls = n_blk * TM
    n_exp = rwt.shape[0]
    return pl.pallas_call(
        functools.partial(_route_kernel, n_batch=n_batch, d=d, blk_off=blk_off),
        grid=(bsz, n_blk),
        in_specs=[pl.BlockSpec((1, TM, d), lambda b, j: (b, j + blk_off, 0)),
                  _const_spec(mod.shape), _const_spec(nw.shape), _const_spec(rwt.shape), _const_spec(rb.shape)],
        out_specs=[pl.BlockSpec((1, TM, d), lambda b, j: (b, j, 0)),
                   pl.BlockSpec((1, TOP_K, TM), lambda b, j: (b, 0, j)),
                   pl.BlockSpec((1, TOP_K, TM), lambda b, j: (b, 0, j))],
        out_shape=[jax.ShapeDtypeStruct((bsz, ls, d), F32),
                   jax.ShapeDtypeStruct((bsz, TOP_K, ls), jnp.int32),
                   jax.ShapeDtypeStruct((bsz, TOP_K, ls), F32)],
        compiler_params=_cp(("parallel", "arbitrary")),
        name="route",
    )(x, mod, nw, rwt, rb)


def _expert_kernel(te_ref, nu_ref, hs_ref, wup_ref, bup_ref, wdn_ref, bdn_ref, y_ref):
    del te_ref
    i = pl.program_id(0)

    @pl.when(i < nu_ref[0])
    def _():
        h = hs_ref[...].astype(BF16)
        ff = wdn_ref.shape[1]
        y = jnp.zeros(y_ref.shape, F32) + bdn_ref[0]
        fc = min(FF_CHUNK, ff)
        for c in range(0, ff, fc):
            glu = jnp.dot(h, wup_ref[0, :, c:c + fc], preferred_element_type=F32) + bup_ref[0, :, c:c + fc]
            lin = (jnp.dot(h, wup_ref[0, :, ff + c:ff + c + fc], preferred_element_type=F32)
                   + bup_ref[0, :, ff + c:ff + c + fc])
            glu = jnp.minimum(glu, SWIGLU_LIMIT)
            lin = jnp.clip(lin, -SWIGLU_LIMIT, SWIGLU_LIMIT)
            act = glu * jax.nn.sigmoid(SWIGLU_ALPHA * glu) * (lin + 1.0)
            y = y + jnp.dot(act.astype(BF16), wdn_ref[0, c:c + fc, :], preferred_element_type=F32)
        y_ref[...] = y

    @pl.when(i >= nu_ref[0])
    def _():
        y_ref[...] = jnp.zeros_like(y_ref)


def _experts(tile_expert, n_used, hs, wup, bup, wdn, bdn):
    n_rows, d = hs.shape
    n_tiles = n_rows // EXPERT_TILE
    ff = wdn.shape[1]
    grid_spec = pltpu.PrefetchScalarGridSpec(
        num_scalar_prefetch=2,
        grid=(n_tiles,),
        in_specs=[pl.BlockSpec((EXPERT_TILE, d), lambda i, te, nu: (i, 0)),
                  pl.BlockSpec((1, d, 2 * ff), lambda i, te, nu: (te[i], 0, 0)),
                  pl.BlockSpec((1, 1, 2 * ff), lambda i, te, nu: (te[i], 0, 0)),
                  pl.BlockSpec((1, ff, d), lambda i, te, nu: (te[i], 0, 0)),
                  pl.BlockSpec((1, 1, d), lambda i, te, nu: (te[i], 0, 0))],
        out_specs=pl.BlockSpec((EXPERT_TILE, d), lambda i, te, nu: (i, 0)),
    )
    return pl.pallas_call(
        _expert_kernel,
        grid_spec=grid_spec,
        out_shape=jax.ShapeDtypeStruct((n_rows, d), F32),
        compiler_params=_cp(("arbitrary",)),
        name="experts",
    )(tile_expert, n_used, hs, wup, bup, wdn, bdn)


GRAN = 8
LOCAL_ROWS = 1280


def _block_plan(idx, n_exp):
    bsz, top_k, ls = idx.shape
    n_blk = ls // TM
    nb = bsz * n_blk
    pairs = top_k * TM
    assert pairs + n_exp * (GRAN - 1) <= LOCAL_ROWS
    e = jnp.transpose(idx.reshape(bsz, top_k, n_blk, TM), (0, 2, 1, 3)).reshape(nb, pairs)
    onehot = e[:, :, None] == jnp.arange(n_exp, dtype=jnp.int32)[None, None, :]
    grouped = onehot.reshape(nb, top_k, TM, n_exp).astype(BF16)
    tri = jnp.tril(jnp.ones((TM, TM), BF16))
    local = jnp.einsum('ts,ngse->ngte', tri, grouped, preferred_element_type=F32)
    g_tot = local[:, :, -1, :]
    g_off = jnp.cumsum(g_tot, axis=1) - g_tot
    cs = (local + g_off[:, :, None, :]).reshape(nb, pairs, n_exp)
    rank = jnp.sum(jnp.where(onehot, cs, 0.0), axis=2).astype(jnp.int32) - 1
    cnt = (g_off[:, -1, :] + g_tot[:, -1, :]).astype(jnp.int32)
    cpad = ((cnt + GRAN - 1) // GRAN) * GRAN
    l_end = jnp.cumsum(cpad, axis=1)
    l_start = l_end - cpad
    ldst = jnp.sum(jnp.where(onehot, l_start[:, None, :], 0), axis=2) + rank
    tot = jnp.sum(cpad, axis=0)
    padded = ((tot + EXPERT_TILE - 1) // EXPERT_TILE) * EXPERT_TILE
    ends = jnp.cumsum(padded)
    starts = ends - padded
    g_start = starts[None, :] + jnp.cumsum(cpad, axis=0) - cpad
    n_gran = LOCAL_ROWS // GRAN
    row0 = jnp.arange(n_gran, dtype=jnp.int32) * GRAN
    gran_e = jnp.sum((row0[None, :, None] >= l_end[:, None, :]).astype(jnp.int32), axis=2)
    gran_oh = jnp.minimum(gran_e, n_exp - 1)[:, :, None] == jnp.arange(n_exp, dtype=jnp.int32)[None, None, :]
    gdst = jnp.sum(jnp.where(gran_oh, (g_start - l_start)[:, None, :], 0), axis=2) + row0[None, :]
    used = (l_end[:, -1] // GRAN).astype(jnp.int32)
    gdst = jnp.where(row0[None, :] < l_end[:, -1:], gdst, 0).astype(jnp.int32)
    n_rows = -(-(nb * pairs + nb * n_exp * (GRAN - 1) + n_exp * EXPERT_TILE) // EXPERT_TILE) * EXPERT_TILE
    n_tiles = n_rows // EXPERT_TILE
    n_used = (ends[-1] // EXPERT_TILE).astype(jnp.int32)
    tile_start = jnp.minimum(jnp.arange(n_tiles, dtype=jnp.int32), n_used - 1) * EXPERT_TILE
    tile_expert = jnp.minimum(jnp.sum((tile_start[:, None] >= ends[None, :]).astype(jnp.int32), axis=1), n_exp - 1)
    ldst = ldst.astype(jnp.int32).reshape(bsz, n_blk, top_k, TM)
    return dict(ldst_rows=jnp.transpose(ldst, (0, 2, 1, 3)).reshape(bsz, top_k, ls), ldst_cols=jnp.transpose(ldst, (0, 1, 3, 2)).reshape(bsz, ls, top_k),
                gdst=gdst.reshape(nb, 1, n_gran), used=used, ends=ends.astype(jnp.int32), padded=padded.astype(jnp.int32),
                tile_expert=tile_expert, n_used=n_used.reshape(1), n_rows=n_rows)


def _gran_copy(src, dst, sem):
    return pltpu.make_async_copy(src, dst, sem)


def _dispatch_blocks_kernel(ends_ref, padded_ref, used_ref, gdst_ref, ldst_ref, h_ref, hs_ref, buf_ref, zero_ref, sem):
    n_blk = pl.num_programs(1)
    blk = pl.program_id(0) * n_blk + pl.program_id(1)

    @pl.when(blk == 0)
    def _():
        zero_ref[...] = jnp.zeros_like(zero_ref)
        for e in range(ends_ref.shape[0]):
            @pl.when(padded_ref[e] > 0)
            def _():
                start = pl.multiple_of(ends_ref[e] - EXPERT_TILE, EXPERT_TILE)
                fill = _gran_copy(zero_ref, hs_ref.at[pl.ds(start, EXPERT_TILE), :], sem)
                fill.start()
                fill.wait()

    rows = lax.broadcasted_iota(jnp.int32, (LOCAL_ROWS, TM), 0)
    hit = rows == ldst_ref[0, 0:1, :]
    for k in range(1, TOP_K):
        hit = jnp.logical_or(hit, rows == ldst_ref[0, k:k + 1, :])
    perm = jnp.where(hit, 1.0, 0.0).astype(BF16)
    buf_ref[...] = jnp.dot(perm, h_ref[0].astype(BF16), preferred_element_type=F32)

    def copy(j):
        src = pl.multiple_of(j * GRAN, GRAN)
        dst = pl.multiple_of(gdst_ref[0, 0, j], GRAN)
        return _gran_copy(buf_ref.at[pl.ds(src, GRAN), :], hs_ref.at[pl.ds(dst, GRAN), :], sem)

    def issue(j, carry):
        copy(j).start()
        return carry

    def drain(j, carry):
        copy(j).wait()
        return carry

    lax.fori_loop(0, used_ref[blk], issue, 0)
    lax.fori_loop(0, used_ref[blk], drain, 0)


def _dispatch_blocks(plan, h):
    bsz, ls, d = h.shape
    n_blk = ls // TM
    n_gran = LOCAL_ROWS // GRAN
    grid_spec = pltpu.PrefetchScalarGridSpec(
        num_scalar_prefetch=3,
        grid=(bsz, n_blk),
        in_specs=[pl.BlockSpec((1, 1, n_gran), lambda b, j, *_: (b * n_blk + j, 0, 0), memory_space=pltpu.SMEM),
                  pl.BlockSpec((1, TOP_K, TM), lambda b, j, *_: (b, 0, j)),
                  pl.BlockSpec((1, TM, d), lambda b, j, *_: (b, j, 0))],
        out_specs=pl.BlockSpec(memory_space=pl.ANY),
        scratch_shapes=[pltpu.VMEM((LOCAL_ROWS, d), F32), pltpu.VMEM((EXPERT_TILE, d), F32),
                        pltpu.SemaphoreType.DMA(())],
    )
    return pl.pallas_call(
        _dispatch_blocks_kernel,
        grid_spec=grid_spec,
        out_shape=jax.ShapeDtypeStruct((plan["n_rows"], d), F32),
        compiler_params=_cp(("arbitrary", "arbitrary")),
        name="dispatch_blocks",
    )(plan["ends"], plan["padded"], plan["used"], plan["gdst"], plan["ldst_rows"], h)


def _combine_blocks_kernel(used_ref, gdst_ref, x_ref, ldst_ref, gate_ref, mod_ref, fw_ref, y_hbm, xo_ref, buf_ref, sem,
                           *, n_batch, d, blk_off, final):
    b, j = pl.program_id(0), pl.program_id(1) + blk_off
    blk = pl.program_id(0) * pl.num_programs(1) + pl.program_id(1)

    @pl.when(blk == 0)
    def _():
        buf_ref[...] = jnp.zeros_like(buf_ref)

    def copy(g):
        dst = pl.multiple_of(g * GRAN, GRAN)
        src = pl.multiple_of(gdst_ref[0, 0, g], GRAN)
        return _gran_copy(y_hbm.at[pl.ds(src, GRAN), :], buf_ref.at[pl.ds(dst, GRAN), :], sem)

    def issue(g, carry):
        copy(g).start()
        return carry

    def drain(g, carry):
        copy(g).wait()
        return carry

    lax.fori_loop(0, used_ref[blk], issue, 0)
    lane = lax.broadcasted_iota(jnp.int32, (TM, LOCAL_ROWS), 1)
    ldst, gate = ldst_ref[0], gate_ref[0]
    weight = jnp.where(lane == ldst[:, 0:1], gate[:, 0:1], 0.0)
    for k in range(1, TOP_K):
        weight = weight + jnp.where(lane == ldst[:, k:k + 1], gate[:, k:k + 1], 0.0)
    lax.fori_loop(0, used_ref[blk], drain, 0)
    acc = jnp.dot(weight.astype(BF16), buf_ref[...].astype(BF16), preferred_element_type=F32)
    (g2,) = _mod_rows(mod_ref, b, j, n_batch, d, 5, 1)
    xo = x_ref[0] + g2 * acc
    if final:
        xo = xo * lax.rsqrt(jnp.mean(xo * xo, axis=-1, keepdims=True) + EPS) * fw_ref[...]
    xo_ref[0] = xo


def _combine_blocks(plan, x, gate, mod, fw, y, n_batch, blk_off, final):
    bsz, l, d = x.shape
    n_blk = l // TM - blk_off
    ls = n_blk * TM
    n_gran = LOCAL_ROWS // GRAN
    grid_spec = pltpu.PrefetchScalarGridSpec(
        num_scalar_prefetch=1,
        grid=(bsz, n_blk),
        in_specs=[pl.BlockSpec((1, 1, n_gran), lambda b, j, *_: (b * n_blk + j, 0, 0), memory_space=pltpu.SMEM),
                  pl.BlockSpec((1, TM, d), lambda b, j, *_: (b, j + blk_off, 0)),
                  pl.BlockSpec((1, TM, TOP_K), lambda b, j, *_: (b, j, 0)),
                  pl.BlockSpec((1, TM, TOP_K), lambda b, j, *_: (b, j, 0)),
                  _const_spec(mod.shape), _const_spec(fw.shape),
                  pl.BlockSpec(memory_space=pl.ANY)],
        out_specs=pl.BlockSpec((1, TM, d), lambda b, j, *_: (b, j, 0)),
        scratch_shapes=[pltpu.VMEM((LOCAL_ROWS, d), F32), pltpu.SemaphoreType.DMA(())],
    )
    return pl.pallas_call(
        functools.partial(_combine_blocks_kernel, n_batch=n_batch, d=d, blk_off=blk_off, final=final),
        grid_spec=grid_spec,
        out_shape=jax.ShapeDtypeStruct((bsz, ls, d), F32),
        compiler_params=_cp(("arbitrary", "arbitrary")),
        name="combine_blocks",
    )(plan["used"], plan["gdst"], x, plan["ldst_cols"], gate, mod, fw, y)


def _moe_blocks(x, mod, nw, rw, rb, wup, bup, wdn, bdn, fw, n_batch, blk_off, final):
    n_exp = rw.shape[1]
    d = x.shape[-1]
    h, idx, gate = _route(x, mod, nw, rw.T, rb.reshape(n_exp, 1), n_batch, blk_off)
    plan = _block_plan(idx, n_exp)
    hs = _dispatch_blocks(plan, h)
    y = _experts(plan["tile_expert"], plan["n_used"], hs, wup.astype(BF16), bup.reshape(n_exp, 1, -1),
                 wdn.astype(BF16), bdn.reshape(n_exp, 1, d))
    return _combine_blocks(plan, x, jnp.transpose(gate, (0, 2, 1)), mod, fw, y, n_batch, blk_off, final)


def _pad_heads(w, n_heads, width):
    lead = w.shape[:-1]
    dh = w.shape[-1] // n_heads
    w = w.reshape(lead + (n_heads, dh))
    w = jnp.pad(w, [(0, 0)] * len(lead) + [(0, 0), (0, width - dh)])
    return w.reshape(lead + (n_heads * width,))


def _layer0_weights(w_in, w2_f, b2_f, w2_b, b2_b):
    d = w_in.shape[0]
    sizes = (A_HEADS * A_DK, A_HEADS * A_DK, A_HEADS * A_DV, A_RANK, A_RANK, A_HEADS * A_DV,
             B_HEADS * B_DK, B_HEADS * B_DK, B_HEADS * B_DK, B_HEADS * B_DV, B_HEADS * B_DV)
    aq, ak, av, ar_f, ar_b, aog, bq, bz_f, bz_b, bi, bog = jnp.split(w_in, np.cumsum(sizes)[:-1].tolist(), axis=1)
    ar = jnp.concatenate([ar_f, ar_b, jnp.zeros((d, HEAD_W - 2 * A_RANK), w_in.dtype)], axis=1)
    w = jnp.concatenate([_pad_heads(aq, A_HEADS, HEAD_W), bq, av, bi, aog, bog,
                         _pad_heads(ak, A_HEADS, HEAD_W), bz_f, bz_b, ar], axis=1).astype(BF16)
    hw = A_HEADS * HEAD_W
    w2 = jnp.zeros((2, HEAD_W, hw), F32)
    w2 = w2.at[0, 0:A_RANK].set(_pad_heads(w2_f, A_HEADS, HEAD_W))
    w2 = w2.at[1, A_RANK:2 * A_RANK].set(_pad_heads(w2_b, A_HEADS, HEAD_W))
    b2 = jnp.stack([_pad_heads(b2_f, A_HEADS, HEAD_W), _pad_heads(b2_b, A_HEADS, HEAD_W)]).reshape(2, 1, hw)
    return w, w2.astype(BF16), b2


QKV_W = 2 * C_HEADS * C_DK + C_HEADS * C_DV
OG_W = C_HEADS * C_DV
DQ_W = D_HEADS * D_HD
DKV_W = D_KV_HEADS * HEAD_W


def _proj1_kernel(x_ref, mod_ref, nw_ref, w_ref, cos_ref, sin_ref,
                  qkv_ref, gb_ref, og_ref, dq_ref, dk_ref, dv_ref, *, n_batch, d):
    b, j = pl.program_id(0), pl.program_id(1)
    shift, scale = _mod_rows(mod_ref, b, j, n_batch, d, 0, 2)
    h = _norm_mod(x_ref[0], nw_ref[...], shift, scale).astype(BF16)

    def mm(c0, c1):
        return jnp.dot(h, w_ref[:, c0:c1], preferred_element_type=F32)

    c = 0
    qkv_ref[0] = mm(c, c + QKV_W); c += QKV_W
    gb_ref[0] = mm(c, c + HEAD_W); c += HEAD_W
    og_ref[0] = mm(c, c + OG_W); c += OG_W
    cos, sin = cos_ref[...], sin_ref[...]
    for s in range(DQ_W // HEAD_W):
        xs = mm(c + s * HEAD_W, c + (s + 1) * HEAD_W)
        xp = mm(c + DQ_W + s * HEAD_W, c + DQ_W + (s + 1) * HEAD_W)
        dq_ref[0, :, s * HEAD_W:(s + 1) * HEAD_W] = (xs * cos + xp * sin) * (D_HD ** -0.5)
    c += 2 * DQ_W
    for s in range(DKV_W // HEAD_W):
        xs = mm(c + s * HEAD_W, c + (s + 1) * HEAD_W)
        xp = mm(c + DKV_W + s * HEAD_W, c + DKV_W + (s + 1) * HEAD_W)
        dk_ref[0, :, s * HEAD_W:(s + 1) * HEAD_W] = xs * cos + xp * sin
    c += 2 * DKV_W
    dv_ref[0] = mm(c, c + DKV_W)


def _proj1(x, mod, nw, w, cos, sin, n_batch):
    bsz, l, d = x.shape
    widths = (QKV_W, HEAD_W, OG_W, DQ_W, DKV_W, DKV_W)
    return pl.pallas_call(
        functools.partial(_proj1_kernel, n_batch=n_batch, d=d),
        grid=(bsz, l // TM),
        in_specs=[pl.BlockSpec((1, TM, d), lambda b, j: (b, j, 0)),
                  _const_spec(mod.shape), _const_spec(nw.shape), _const_spec(w.shape),
                  pl.BlockSpec((TM, HEAD_W), lambda b, j: (j, 0)),
                  pl.BlockSpec((TM, HEAD_W), lambda b, j: (j, 0))],
        out_specs=[pl.BlockSpec((1, TM, wd), lambda b, j: (b, j, 0)) for wd in widths],
        out_shape=[jax.ShapeDtypeStruct((bsz, l, wd), F32) for wd in widths],
        compiler_params=_cp(("parallel", "arbitrary")),
        name="proj1",
    )(x, mod, nw, w, cos, sin)


HALO = 8


def _conv1_kernel(cur_ref, prev_ref, next_ref, cw_ref, gbr_ref, alog_ref, dtb_ref,
                  q_ref, k_ref, v_ref, gb_ref, xe_ref):
    j, n_blk = pl.program_id(1), pl.num_programs(1)
    prev_ok = j >= 2
    next_ok = jnp.logical_and(j >= 1, j < n_blk - 1)
    xe_ref[0:HALO, :] = jnp.where(prev_ok, prev_ref[0], 0.0)
    xe_ref[HALO:HALO + TM, :] = cur_ref[0]
    xe_ref[HALO + TM:2 * HALO + TM, :] = jnp.where(next_ok, next_ref[0], 0.0)
    pad = CONV_W // 2
    hw = C_HEADS * HEAD_W
    for g, o_ref in enumerate((q_ref, k_ref, v_ref)):
        cols = slice(g * hw, (g + 1) * hw)
        acc = cw_ref[0:1, cols] * xe_ref[HALO - pad:HALO - pad + TM, cols]
        for i in range(1, CONV_W):
            acc = acc + cw_ref[i:i + 1, cols] * xe_ref[HALO - pad + i:HALO - pad + i + TM, cols]
        y = _silu(acc)
        if g < 2:
            parts = []
            for hh in range(C_HEADS):
                yh = y[:, hh * HEAD_W:(hh + 1) * HEAD_W]
                yh = yh * lax.rsqrt(jnp.sum(yh * yh, axis=-1, keepdims=True) + EPS)
                parts.append(yh * (C_DK ** -0.5) if g == 0 else yh)
            y = jnp.concatenate(parts, axis=-1)
        o_ref[0] = y
    raw = gbr_ref[0]
    z = raw + dtb_ref[...]
    softplus = jnp.maximum(z, 0.0) + _log1p_exp_neg_abs(z)
    lane = lax.broadcasted_iota(jnp.int32, raw.shape, 1)
    gb_ref[0] = jnp.where(lane < 2 * C_HEADS, jax.nn.sigmoid(raw), -jnp.exp(alog_ref[...]) * softplus)


def _conv1(qkv, conv_w, gb_raw, alog, dtb):
    bsz, l, wd = qkv.shape
    n_blk = l // TM
    per = TM // HALO
    last = l // HALO - 1
    hw = C_HEADS * HEAD_W
    ob = pl.BlockSpec((1, TM, hw), lambda b, j: (b, j, 0))
    gbs = pl.BlockSpec((1, TM, HEAD_W), lambda b, j: (b, j, 0))
    return pl.pallas_call(
        _conv1_kernel,
        grid=(bsz, n_blk),
        in_specs=[pl.BlockSpec((1, TM, wd), lambda b, j: (b, j, 0)),
                  pl.BlockSpec((1, HALO, wd), lambda b, j: (b, jnp.maximum(j * per - 1, 0), 0)),
                  pl.BlockSpec((1, HALO, wd), lambda b, j: (b, jnp.minimum((j + 1) * per, last), 0)),
                  _const_spec(conv_w.shape), gbs, _const_spec(alog.shape), _const_spec(dtb.shape)],
        out_specs=[ob, ob, ob, gbs],
        out_shape=[jax.ShapeDtypeStruct((bsz, l, hw), F32)] * 3 + [jax.ShapeDtypeStruct((bsz, l, HEAD_W), F32)],
        scratch_shapes=[pltpu.VMEM((TM + 2 * HALO, wd), F32)],
        compiler_params=_cp(("parallel", "arbitrary")),
        name="conv1",
    )(qkv, qkv, qkv, conv_w, gb_raw, alog, dtb)


_MERGE = (16, 32, 64)
_BASE = 8


def _delta_masks():
    t = np.arange(CHUNK)[:, None]
    s = np.arange(CHUNK)[None, :]
    out = [s <= t, s < t, s >= t, s > t, (t // _BASE) == (s // _BASE)]
    for m2 in _MERGE:
        out.append(((t // m2) == (s // m2)) & ((t // (m2 // 2)) != (s // (m2 // 2))))
    return np.stack(out).astype(np.float32)


def _bdot(a, b):
    return jnp.dot(a.astype(BF16), b.astype(BF16), preferred_element_type=F32)


DELTA_HEADS = 4


def _delta_prepare(chains, dm):
    n = range(len(chains))
    qs, ks, vs, gs, betas, revs = zip(*chains)
    incl = [dm[2] if r else dm[0] for r in revs]
    strict = [dm[3] if r else dm[1] for r in revs]
    cum = [_cumsum_rows(jnp.broadcast_to(gs[i], qs[i].shape), revs[i]) for i in n]
    decay = [incl[i] * jnp.exp(jnp.where(incl[i] > 0, cum[i][:, 0:CHUNK] - cum[i].T[0:CHUNK, :], 0.0)) for i in n]
    kbeta = [ks[i] * betas[i] for i in n]
    kbf = [ks[i].astype(BF16) for i in n]
    a = [strict[i] * _dot_nt(kbeta[i].astype(BF16), kbf[i]) * decay[i] for i in n]
    eye = dm[0] * dm[2]
    n0 = [-(a[i] * dm[4]) for i in n]
    n2 = [_bdot(n0[i], n0[i]) for i in n]
    n4 = [_bdot(n2[i], n2[i]) for i in n]
    t = [eye + n0[i] for i in n]
    t = [t[i] + _bdot(t[i], n2[i]) for i in n]
    t = [t[i] + _bdot(t[i], n4[i]) for i in n]
    for li in range(len(_MERGE)):
        inner = [_bdot(a[i] * dm[5 + li], t[i]) for i in n]
        t = [t[i] - _bdot(t[i], inner[i]) for i in n]
    uw = [_bdot(t[i], jnp.concatenate([vs[i] * betas[i], kbeta[i] * jnp.exp(cum[i])], axis=1)) for i in n]
    scores = [(_dot_nt(qs[i].astype(BF16), kbf[i]) * decay[i]).astype(BF16) for i in n]
    edge = [cum[i][0:1] if revs[i] else cum[i][CHUNK - 1:CHUNK] for i in n]
    q_in = [(qs[i] * jnp.exp(cum[i])).astype(BF16) for i in n]
    k_out = [(ks[i] * jnp.exp(edge[i] - cum[i])).astype(BF16) for i in n]
    carry = [jnp.exp(edge[i]) for i in n]
    return [(uw[i][:, 0:HEAD_W], uw[i][:, HEAD_W:].astype(BF16), scores[i], q_in[i], k_out[i], carry[i]) for i in n]


def _delta_step(prep, states):
    n = range(len(prep))
    sb = [states[i].astype(BF16) for i in n]
    v_new = [prep[i][0] - jnp.dot(prep[i][1], sb[i], preferred_element_type=F32) for i in n]
    vb = [v_new[i].astype(BF16) for i in n]
    o = [jnp.dot(prep[i][3], sb[i], preferred_element_type=F32)
         + jnp.dot(prep[i][2], vb[i], preferred_element_type=F32) for i in n]
    new = [states[i] * prep[i][5] + _dot_tn(prep[i][4], vb[i]) for i in n]
    return o, new


def _lane_col(x, lane_idx):
    lane = lax.broadcasted_iota(jnp.int32, x.shape, 1)
    return jnp.sum(jnp.where(lane == lane_idx, x, 0.0), axis=-1, keepdims=True)


def _delta_kernel(dm_ref, qf_ref, kf_ref, vf_ref, gbf_ref, qb_ref, kb_ref, vb_ref, gbb_ref,
                  of_ref, ob_ref, st_ref):
    @pl.when(pl.program_id(2) == 0)
    def _():
        st_ref[...] = jnp.zeros_like(st_ref)

    dm = [dm_ref[i] for i in range(dm_ref.shape[0])]
    n_ch = TM // CHUNK
    gbf, gbb = gbf_ref[0], gbb_ref[0]
    chains = []
    cols = []
    for hh in range(DELTA_HEADS):
        h = pl.program_id(1) * DELTA_HEADS + hh
        cols.append((_lane_col(gbf, h), _lane_col(gbf, 2 * C_HEADS + h),
                     _lane_col(gbb, C_HEADS + h), _lane_col(gbb, 3 * C_HEADS + h)))
    for c in range(n_ch):
        rf = slice(c * CHUNK, (c + 1) * CHUNK)
        rb = slice((n_ch - 1 - c) * CHUNK, (n_ch - c) * CHUNK)
        for hh in range(DELTA_HEADS):
            lanes = slice(hh * HEAD_W, (hh + 1) * HEAD_W)
            beta_f, g_f, beta_b, g_b = cols[hh]
            chains.append((qf_ref[0, rf, lanes], kf_ref[0, rf, lanes], vf_ref[0, rf, lanes], g_f[rf], beta_f[rf], False))
            chains.append((qb_ref[0, rb, lanes], kb_ref[0, rb, lanes], vb_ref[0, rb, lanes], g_b[rb], beta_b[rb], True))
    prep = _delta_prepare(chains, dm)
    per = 2 * DELTA_HEADS
    states = [st_ref[i] for i in range(per)]
    for c in range(n_ch):
        rf = slice(c * CHUNK, (c + 1) * CHUNK)
        rb = slice((n_ch - 1 - c) * CHUNK, (n_ch - c) * CHUNK)
        outs, states = _delta_step(prep[c * per:(c + 1) * per], states)
        for hh in range(DELTA_HEADS):
            lanes = slice(hh * HEAD_W, (hh + 1) * HEAD_W)
            of_ref[0, rf, lanes] = outs[2 * hh]
            ob_ref[0, rb, lanes] = outs[2 * hh + 1]
    for i in range(per):
        st_ref[i] = states[i]


def _delta(dmasks, q, k, v, gb):
    bsz, l, wide = q.shape
    n_blk = l // TM
    bw = DELTA_HEADS * HEAD_W
    fwd = pl.BlockSpec((1, TM, bw), lambda b, h, i: (b, i, h))
    bwd = pl.BlockSpec((1, TM, bw), lambda b, h, i: (b, _bwd_block(i, n_blk), h))
    gf = pl.BlockSpec((1, TM, HEAD_W), lambda b, h, i: (b, i, 0))
    gbw = pl.BlockSpec((1, TM, HEAD_W), lambda b, h, i: (b, _bwd_block(i, n_blk), 0))
    return pl.pallas_call(
        _delta_kernel,
        grid=(bsz, wide // bw, n_blk),
        in_specs=[_const_spec(dmasks.shape), fwd, fwd, fwd, gf, bwd, bwd, bwd, gbw],
        out_specs=[fwd, bwd],
        out_shape=[jax.ShapeDtypeStruct((bsz, l, wide), F32)] * 2,
        scratch_shapes=[pltpu.VMEM((2 * DELTA_HEADS, HEAD_W, HEAD_W), F32)],
        compiler_params=_cp(("parallel", "parallel", "arbitrary")),
        name="delta",
    )(dmasks, q, k, v, gb, q, k, v, gb)


def _attn_kernel(q_ref, kp_ref, kc_ref, kn_ref, vp_ref, vc_ref, vn_ref, kx_ref, vx_ref, sink_ref, o_ref, *, t_len):
    i = pl.program_id(1)
    q = q_ref[0]
    kl = jnp.concatenate([kp_ref[0], kc_ref[0], kn_ref[0]], axis=0).astype(BF16)
    vl = jnp.concatenate([vp_ref[0], vc_ref[0], vn_ref[0]], axis=0).astype(BF16)
    kx, vx = kx_ref[0].astype(BF16), vx_ref[0].astype(BF16)
    qpos = lax.broadcasted_iota(jnp.int32, (WINDOW, 3 * WINDOW), 0)
    kpos = lax.broadcasted_iota(jnp.int32, (WINDOW, 3 * WINDOW), 1) - WINDOW
    k_abs = i * WINDOW + kpos
    valid = (jnp.abs(kpos - qpos) <= WINDOW) & (k_abs >= 0) & (k_abs < t_len)
    low = lax.broadcasted_iota(jnp.int32, (WINDOW, HEAD_W), 1) < D_HD
    group = D_HEADS // D_KV_HEADS
    heads = range(D_HEADS)
    cols = [slice((h // group) * HEAD_W, (h // group + 1) * HEAD_W) for h in heads]
    qm = [jnp.where(low if h % 2 == 0 else jnp.logical_not(low), q[:, (h // 2) * HEAD_W:(h // 2 + 1) * HEAD_W], 0.0)
          .astype(BF16) for h in heads]
    s_l = [jnp.where(valid, _dot_nt(qm[h], kl[:, cols[h]]), -jnp.inf) for h in heads]
    s_x = [_dot_nt(qm[h], kx[:, cols[h]]) for h in heads]
    sink = [sink_ref[:, h:h + 1] for h in heads]
    def fold(a, b, op):
        slabs = [a[:, c:c + HEAD_W] for c in range(0, a.shape[1], HEAD_W)]
        slabs += [b[:, c:c + HEAD_W] for c in range(0, b.shape[1], HEAD_W)]
        acc = slabs[0]
        for s in slabs[1:]:
            acc = op(acc, s)
        return acc

    m = [jnp.maximum(jnp.max(fold(s_l[h], s_x[h], jnp.maximum), axis=-1, keepdims=True), sink[h]) for h in heads]
    p_l = [jnp.exp(s_l[h] - m[h]) for h in heads]
    p_x = [jnp.exp(s_x[h] - m[h]) for h in heads]
    den = [jnp.sum(fold(p_l[h], p_x[h], jnp.add), axis=-1, keepdims=True) + jnp.exp(sink[h] - m[h]) for h in heads]
    o = [(jnp.dot(p_l[h].astype(BF16), vl[:, cols[h]], preferred_element_type=F32)
          + jnp.dot(p_x[h].astype(BF16), vx[:, cols[h]], preferred_element_type=F32)) / den[h] for h in heads]
    for p in range(D_HEADS // 2):
        o_ref[0, :, p * HEAD_W:(p + 1) * HEAD_W] = jnp.where(low, o[2 * p], o[2 * p + 1])


def _attn(dq, dk, dv, sinks, n_ctx):
    bsz, l, _ = dq.shape
    t_len = l - n_ctx
    nq = t_len // WINDOW
    off = n_ctx // WINDOW

    def kv(delta):
        return pl.BlockSpec((1, WINDOW, DKV_W), lambda b, i: (b, off + jnp.clip(i + delta, 0, nq - 1), 0))

    ctx = pl.BlockSpec((1, n_ctx, DKV_W), lambda b, i: (b, 0, 0))
    return pl.pallas_call(
        functools.partial(_attn_kernel, t_len=t_len),
        grid=(bsz, nq),
        in_specs=[pl.BlockSpec((1, WINDOW, DQ_W), lambda b, i: (b, off + i, 0)),
                  kv(-1), kv(0), kv(1), kv(-1), kv(0), kv(1), ctx, ctx, _const_spec(sinks.shape)],
        out_specs=pl.BlockSpec((1, WINDOW, DQ_W), lambda b, i: (b, i, 0)),
        out_shape=jax.ShapeDtypeStruct((bsz, t_len, DQ_W), F32),
        compiler_params=_cp(("parallel", "arbitrary")),
        name="attn",
    )(dq, dk, dk, dk, dv, dv, dv, dk, dv, sinks)


def _read1_kernel(x_ref, of_ref, ob_ref, og_ref, od_ref, nw_ref, wout_ref, mod_ref, xo_ref, *, n_batch, d):
    b, j = pl.program_id(0), pl.program_id(1) + 1
    (g1,) = _mod_rows(mod_ref, b, j, n_batch, d, 2, 1)
    o = of_ref[0] + ob_ref[0]
    parts = [_head_rms(o[:, hh * HEAD_W:(hh + 1) * HEAD_W], nw_ref[...]) for hh in range(C_HEADS)]
    y = jnp.concatenate(parts, axis=-1) * _silu(og_ref[0])
    y = jnp.concatenate([y, od_ref[0]], axis=-1)
    xo_ref[0] = x_ref[0] + g1 * jnp.dot(y.astype(BF16), wout_ref[...], preferred_element_type=F32)


def _read1(x, of, ob, og, od, nw, wout, mod, n_batch):
    bsz, l, d = x.shape
    lat = lambda wd: pl.BlockSpec((1, TM, wd), lambda b, j: (b, j + 1, 0))
    return pl.pallas_call(
        functools.partial(_read1_kernel, n_batch=n_batch, d=d),
        grid=(bsz, l // TM - 1),
        in_specs=[lat(d), lat(OG_W), lat(OG_W), lat(OG_W),
                  pl.BlockSpec((1, TM, DQ_W), lambda b, j: (b, j, 0)),
                  _const_spec(nw.shape), _const_spec(wout.shape), _const_spec(mod.shape)],
        out_specs=lat(d),
        out_shape=jax.ShapeDtypeStruct(x.shape, F32),
        input_output_aliases={0: 0},
        compiler_params=_cp(("parallel", "arbitrary")),
        name="read1",
    )(x, of, ob, og, od, nw, wout, mod)


def _swap_halves(w, n_heads):
    lead = w.shape[:-1]
    dh = w.shape[-1] // n_heads
    w = w.reshape(lead + (n_heads, 2, dh // 2))
    return w[..., ::-1, :].reshape(lead + (n_heads * dh,))


def _dup_heads(w, n_heads):
    lead = w.shape[:-1]
    dh = w.shape[-1] // n_heads
    w = w.reshape(lead + (n_heads, 1, dh))
    return jnp.concatenate([w, w], axis=-2).reshape(lead + (2 * n_heads * dh,))


def _layer1_weights(w_in):
    d = w_in.shape[0]
    sizes = (C_HEADS * C_DK, C_HEADS * C_DK, C_HEADS * C_DV, C_HEADS, C_HEADS, C_HEADS, C_HEADS,
             C_HEADS * C_DV, D_HEADS * D_HD, D_KV_HEADS * D_HD, D_KV_HEADS * D_HD)
    cq, ck, cv, bt_f, bt_b, a_f, a_b, og, dq, dk, dv = jnp.split(w_in, np.cumsum(sizes)[:-1].tolist(), axis=1)
    gates = jnp.concatenate([bt_f, bt_b, a_f, a_b, jnp.zeros((d, HEAD_W - 4 * C_HEADS), w_in.dtype)], axis=1)
    return jnp.concatenate([cq, ck, cv, gates, og, dq, _swap_halves(dq, D_HEADS),
                            _dup_heads(dk, D_KV_HEADS), _dup_heads(_swap_halves(dk, D_KV_HEADS), D_KV_HEADS),
                            _dup_heads(dv, D_KV_HEADS)], axis=1).astype(BF16)


def _rope_tables(n_ctx, t_len):
    rows = t_len // GRID_W
    row = jnp.repeat(jnp.arange(rows, dtype=F32), GRID_W)
    col = jnp.tile(jnp.arange(GRID_W, dtype=F32), rows)
    n_freq = D_HD // 4
    inv = ROPE_BASE ** (-jnp.arange(n_freq, dtype=F32) / n_freq)
    ang = jnp.concatenate([row[:, None] * inv, col[:, None] * inv], axis=-1)
    cos, sin = jnp.cos(ang), jnp.sin(ang)
    cos = jnp.concatenate([jnp.ones((n_ctx, D_HD // 2), F32), cos], axis=0)
    sin = jnp.concatenate([jnp.zeros((n_ctx, D_HD // 2), F32), sin], axis=0)
    return jnp.concatenate([cos] * 4, axis=1), jnp.concatenate([-sin, sin] * 2, axis=1)


def _mixer1(xx, mod, nw, w_in, w_out, conv_w, a_log_f, dt_bias_f, a_log_b, dt_bias_b, dn_norm_w, sinks, n_batch, n_ctx):
    l = xx.shape[1]
    cos, sin = _rope_tables(n_ctx, l - n_ctx)
    qkv, gb_raw, og, dq, dk, dv = _proj1(xx, mod, nw, _layer1_weights(w_in), cos, sin, n_batch)
    zero4 = jnp.zeros((C_HEADS,), F32)
    fill = jnp.zeros((HEAD_W - 4 * C_HEADS,), F32)
    alog = jnp.concatenate([zero4, zero4, a_log_f, a_log_b, fill]).reshape(1, HEAD_W)
    dtb = jnp.concatenate([zero4, zero4, dt_bias_f, dt_bias_b, fill]).reshape(1, HEAD_W)
    cq, ck, cv, gb = _conv1(qkv, conv_w, gb_raw, alog, dtb)
    of, ob = _delta(jnp.asarray(_delta_masks()), cq, ck, cv, gb)
    sink_row = jnp.concatenate([sinks, jnp.zeros((HEAD_W - D_HEADS,), F32)]).reshape(1, HEAD_W)
    od = _attn(dq, dk, dv, sink_row, n_ctx)
    return _read1(xx, of, ob, og, od, dn_norm_w.reshape(1, -1), w_out.astype(BF16), mod, n_batch)


def kernel(x, c, ctx, c_ctx, l0_ada_w, l0_ada_b, l0_norm_mix_w, l0_w_in, l0_w_out, l0_gla_w2_f, l0_gla_b_f, l0_gla_w2_b, l0_gla_b_b, l0_gla_norm_w, l0_hgrn_norm_w, hgrn_lb_logits, l0_norm_ffn_w, l0_router_w, l0_router_b, l0_w_up, l0_b_up, l0_w_down, l0_b_down, l1_ada_w, l1_ada_b, l1_norm_mix_w, l1_w_in, l1_w_out, l1_conv_w, l1_a_log_f, l1_dt_bias_f, l1_a_log_b, l1_dt_bias_b, l1_dn_norm_w, l1_sinks, l1_norm_ffn_w, l1_router_w, l1_router_b, l1_w_up, l1_b_up, l1_w_down, l1_b_down, final_norm_w):
    bsz, t, d = x.shape
    n_ctx = ctx.shape[1]
    assert n_ctx == TM and t % TM == 0
    xx = jnp.concatenate([ctx, x], axis=1)
    mod_rows = -(-(bsz + 1) // 8) * 8
    c_all = jnp.zeros((mod_rows, d), F32).at[:bsz].set(c).at[bsz].set(c_ctx)
    ones = jnp.ones((1, d), F32)

    mod0 = _ada_table(c_all, l0_ada_w, l0_ada_b)
    w0, w2, b2 = _layer0_weights(l0_w_in, l0_gla_w2_f, l0_gla_b_f, l0_gla_w2_b, l0_gla_b_b)
    q, kf, kb, v, lff, lfb, og = _proj0(xx, mod0, l0_norm_mix_w.reshape(1, d), w0, w2, b2, hgrn_lb_logits, bsz)
    of, ob = _scan0(jnp.asarray(_level_masks()), jnp.asarray(_tri3(), BF16), q, kf, kb, v, lff, lfb)
    xx = _read0(xx, of, ob, og, l0_gla_norm_w.reshape(1, -1), l0_hgrn_norm_w.reshape(1, -1), l0_w_out.astype(BF16), mod0, bsz)
    xx = _moe_blocks(xx, mod0, l0_norm_ffn_w.reshape(1, d), l0_router_w, l0_router_b, l0_w_up, l0_b_up, l0_w_down, l0_b_down,
              ones, bsz, 0, False)

    mod1 = _ada_table(c_all, l1_ada_w, l1_ada_b)
    xx = _mixer1(xx, mod1, l1_norm_mix_w.reshape(1, d), l1_w_in, l1_w_out, l1_conv_w, l1_a_log_f, l1_dt_bias_f,
                 l1_a_log_b, l1_dt_bias_b, l1_dn_norm_w, l1_sinks, bsz, n_ctx)
    return _moe_blocks(xx, mod1, l1_norm_ffn_w.reshape(1, d), l1_router_w, l1_router_b, l1_w_up, l1_b_up, l1_w_down, l1_b_down,
                final_norm_w.reshape(1, d), bsz, 1, True)
```

```python
import functools

import numpy as np
import jax
import jax.numpy as jnp
from jax import lax
from jax.experimental import pallas as pl
from jax.experimental.pallas import tpu as pltpu

F32 = jnp.float32
BF16 = jnp.bfloat16
EPS = 1e-6

CHUNK = 64
A_HEADS, A_DK, A_DV, A_RANK = 4, 64, 128, 16
GATE_TAU = 16.0
B_HEADS, B_DK, B_DV = 4, 128, 128
C_HEADS, C_DK, C_DV = 4, 128, 128
CONV_W = 5
D_HEADS, D_KV_HEADS, D_HD = 8, 2, 64
WINDOW = 128
GRID_W = 64
ROPE_BASE = 10000.0
TOP_K = 4
SWIGLU_LIMIT = 7.0
SWIGLU_ALPHA = 1.702

TM = 256
HEAD_W = 128
EXPERT_TILE = 512
FF_CHUNK = 512
VMEM_LIMIT = 48 * 1024 * 1024
EXPERT_VMEM_LIMIT = 56 * 1024 * 1024


def _cp(sem, vmem=VMEM_LIMIT):
    return pltpu.CompilerParams(dimension_semantics=sem, vmem_limit_bytes=vmem)


def _const_spec(shape):
    nd = len(shape)
    return pl.BlockSpec(shape, lambda *_: (0,) * nd)


def _silu(x):
    return x * jax.nn.sigmoid(x)


def _log1p_exp_neg_abs(z):
    return jnp.log(1.0 + jnp.exp(-jnp.abs(z)))


def _log_sigmoid(z):
    return jnp.minimum(z, 0.0) - _log1p_exp_neg_abs(z)


def _norm_mod(x, nw, shift, scale):
    y = x * lax.rsqrt(jnp.mean(x * x, axis=-1, keepdims=True) + EPS) * nw
    return y * (1.0 + scale) + shift


def _mod_rows(mod_ref, batch, blk, n_batch, d, first, count):
    row = jnp.where(blk == 0, n_batch, batch)
    return [mod_ref[pl.ds(row, 1), (first + i) * d:(first + i + 1) * d] for i in range(count)]


def _ada_kernel(c_ref, w_ref, b_ref, o_ref):
    s = _silu(c_ref[...])
    o_ref[...] = jnp.dot(s, w_ref[...], precision=lax.Precision.HIGHEST,
                         preferred_element_type=F32) + b_ref[...]


def _ada_table(c_all, w, b):
    rows, d = c_all.shape
    n = w.shape[1]
    bn = d
    return pl.pallas_call(
        _ada_kernel,
        grid=(n // bn,),
        in_specs=[pl.BlockSpec((rows, d), lambda j: (0, 0)),
                  pl.BlockSpec((d, bn), lambda j: (0, j)),
                  pl.BlockSpec((1, bn), lambda j: (0, j))],
        out_specs=pl.BlockSpec((rows, bn), lambda j: (0, j)),
        out_shape=jax.ShapeDtypeStruct((rows, n), F32),
        compiler_params=_cp(("arbitrary",)),
        name="ada_table",
    )(c_all, w, b.reshape(1, n))


def _proj0_kernel(x_ref, mod_ref, nw_ref, w_ref, w2_ref, b2_ref, lbl_ref,
                  q_ref, kf_ref, kb_ref, v_ref, lff_ref, lfb_ref, og_ref, *, n_batch, d):
    b, j = pl.program_id(0), pl.program_id(1)
    shift, scale = _mod_rows(mod_ref, b, j, n_batch, d, 0, 2)
    h = _norm_mod(x_ref[0], nw_ref[...], shift, scale).astype(BF16)

    def mm(c0, c1):
        return jnp.dot(h, w_ref[:, c0:c1], preferred_element_type=F32)

    hw = A_HEADS * HEAD_W

    def plain_outputs():
        q_ref[0, :, 0:hw] = mm(0, hw) * (A_DK ** -0.5)
        q_ref[0, :, hw:2 * hw] = mm(hw, 2 * hw)
        v_ref[0] = mm(2 * hw, 4 * hw)
        og_ref[0] = mm(4 * hw, 6 * hw)
        kg = mm(6 * hw, 7 * hw)
        kf_ref[0, :, 0:hw] = kg
        kb_ref[0, :, 0:hw] = kg

    lg = lbl_ref[...]
    e = jnp.exp(lg - jnp.max(lg, axis=0, keepdims=True))
    lb = e[0:1] / jnp.sum(e, axis=0, keepdims=True)
    log_lb, log_1m = jnp.log(lb), jnp.log1p(-lb)
    ar = mm(9 * hw, 9 * hw + HEAD_W).astype(BF16)
    for di, (k_ref, lf_ref) in enumerate(((kf_ref, lff_ref), (kb_ref, lfb_ref))):
        z = mm((7 + di) * hw, (8 + di) * hw)
        s1 = log_1m + _log_sigmoid(z)
        lf_ref[0, :, hw:2 * hw] = jnp.maximum(log_lb, s1) + _log1p_exp_neg_abs(log_lb - s1)
        k_ref[0, :, hw:2 * hw] = (1.0 - lb) * jax.nn.sigmoid(-z)
        za = jnp.dot(ar, w2_ref[di], preferred_element_type=F32) + b2_ref[di]
        lf_ref[0, :, 0:hw] = _log_sigmoid(za) * (1.0 / GATE_TAU)
    plain_outputs()


def _proj0(x, mod, nw, w, w2, b2, lbl, n_batch):
    bsz, l, d = x.shape
    n_blk = l // TM
    wide = 2 * A_HEADS * HEAD_W
    blk = pl.BlockSpec((1, TM, wide), lambda b, j: (b, j, 0))
    return pl.pallas_call(
        functools.partial(_proj0_kernel, n_batch=n_batch, d=d),
        grid=(bsz, n_blk),
        in_specs=[pl.BlockSpec((1, TM, d), lambda b, j: (b, j, 0)),
                  _const_spec(mod.shape), _const_spec(nw.shape), _const_spec(w.shape),
                  _const_spec(w2.shape), _const_spec(b2.shape), _const_spec(lbl.shape)],
        out_specs=[blk] * 7,
        out_shape=[jax.ShapeDtypeStruct((bsz, l, wide), F32)] * 7,
        compiler_params=_cp(("parallel", "arbitrary")),
        name="proj0",
    )(x, mod, nw, w, w2, b2, lbl)


_LEVELS = (32, 16, 8, 4, 2, 1)


def _level_masks():
    t = np.arange(CHUNK)[:, None]
    s = np.arange(CHUNK)[None, :]
    out = np.zeros((2, len(_LEVELS) + 1, CHUNK, CHUNK), np.float32)
    for li, m in enumerate(_LEVELS):
        same = (t // (2 * m)) == (s // (2 * m))
        fwd = same & (t % (2 * m) >= m) & (s % (2 * m) < m)
        out[0, li] = fwd
        out[1, li] = fwd.T
    out[:, -1] = np.eye(CHUNK)
    return out


def _cumsum_rows(x, reverse):
    n = x.shape[0]
    r = lax.broadcasted_iota(jnp.int32, x.shape, 0)
    sh = 1
    while sh < n:
        if reverse:
            x = x + jnp.where(r < n - sh, pltpu.roll(x, n - sh, 0), 0.0)
        else:
            x = x + jnp.where(r >= sh, pltpu.roll(x, sh, 0), 0.0)
        sh *= 2
    return x


def _level_ref(cum, m, reverse):
    n = cum.shape[0]
    tgt = m if reverse else m - 1
    if 2 * m >= 8:
        parts = [jnp.broadcast_to(cum[g + tgt:g + tgt + 1, :], (2 * m, cum.shape[1]))
                 for g in range(0, n, 2 * m)]
        return parts[0] if len(parts) == 1 else jnp.concatenate(parts, axis=0)
    pos = lax.broadcasted_iota(jnp.int32, cum.shape, 0) % (2 * m)
    out = cum
    for p in range(2 * m):
        if p == tgt:
            continue
        shift = (p - tgt) % n
        out = jnp.where(pos == p, pltpu.roll(cum, shift, 0), out)
    return out


def _dot_nt(a, b):
    return lax.dot_general(a, b, (((1,), (1,)), ((), ())), preferred_element_type=F32)


def _dot_tn(a, b):
    return lax.dot_general(a, b, (((0,), (0,)), ((), ())), preferred_element_type=F32)


def _tri3():
    t = np.arange(CHUNK)[:, None]
    s = np.arange(CHUNK)[None, :]
    return np.stack([np.tile(s <= t, (1, 3)), np.tile(s >= t, (1, 3))]).astype(np.float32)


def _cumsum_split(x, tri):
    hi = x.astype(BF16)
    r1 = x - hi.astype(F32)
    mid = r1.astype(BF16)
    lo = (r1 - mid.astype(F32)).astype(BF16)
    return jnp.dot(tri, jnp.concatenate([hi, mid, lo], axis=0), preferred_element_type=F32)


def _gated_prepare(chains, masks, tri3):
    n = range(len(chains))
    qs, ks, vs, lfs, revs = zip(*chains)
    cum = [_cumsum_split(lfs[i], tri3[revs[i]]) for i in n]
    qb = [qs[i].astype(BF16) for i in n]
    kb = [ks[i].astype(BF16) for i in n]
    vb = [vs[i].astype(BF16) for i in n]
    scores = [masks[revs[i]][len(_LEVELS)] * _dot_nt(qb[i], kb[i]) for i in n]
    for li, m in enumerate(_LEVELS):
        w = [jnp.exp(-jnp.abs(cum[i] - _level_ref(cum[i], m, revs[i]))).astype(BF16) for i in n]
        part = [_dot_nt(qb[i] * w[i], kb[i] * w[i]) for i in n]
        scores = [scores[i] + masks[revs[i]][li] * part[i] for i in n]
    intra = [jnp.dot(scores[i].astype(BF16), vb[i], preferred_element_type=F32) for i in n]
    q_in = [(qs[i] * jnp.exp(cum[i])).astype(BF16) for i in n]
    edge = [cum[i][0:1] if revs[i] else cum[i][CHUNK - 1:CHUNK] for i in n]
    update = [_dot_tn(vb[i], (ks[i] * jnp.exp(edge[i] - cum[i])).astype(BF16)) for i in n]
    carry = [jnp.exp(edge[i]) for i in n]
    return [(intra[i], q_in[i], update[i], carry[i]) for i in n]


SCAN_HEADS = 4


def _scan0_kernel(m_ref, tri_ref, qf_ref, kf_ref, vf_ref, lff_ref, qb_ref, kb_ref, vb_ref, lfb_ref,
                  of_ref, ob_ref, st_ref):
    @pl.when(pl.program_id(2) == 0)
    def _():
        st_ref[...] = jnp.zeros_like(st_ref)

    masks = [[m_ref[r, i] for i in range(len(_LEVELS) + 1)] for r in range(2)]
    tri3 = [tri_ref[0], tri_ref[1]]
    n_ch = TM // CHUNK
    fwd_rows = [slice(c * CHUNK, (c + 1) * CHUNK) for c in range(n_ch)]
    bwd_rows = [slice((n_ch - 1 - c) * CHUNK, (n_ch - c) * CHUNK) for c in range(n_ch)]
    lanes = [slice(hh * HEAD_W, (hh + 1) * HEAD_W) for hh in range(SCAN_HEADS)]
    chains = []
    for c in range(n_ch):
        rf, rb = fwd_rows[c], bwd_rows[c]
        for ln in lanes:
            chains.append((qf_ref[0, rf, ln], kf_ref[0, rf, ln], vf_ref[0, rf, ln], lff_ref[0, rf, ln], 0))
            chains.append((qb_ref[0, rb, ln], kb_ref[0, rb, ln], vb_ref[0, rb, ln], lfb_ref[0, rb, ln], 1))
    prep = _gated_prepare(chains, masks, tri3)
    per = 2 * SCAN_HEADS
    states = [st_ref[i] for i in range(per)]
    for c in range(n_ch):
        for hh, ln in enumerate(lanes):
            for r, (o_ref, rows) in enumerate(((of_ref, fwd_rows[c]), (ob_ref, bwd_rows[c]))):
                s = 2 * hh + r
                intra, q_in, update, carry = prep[c * per + s]
                o_ref[0, rows, ln] = intra + _dot_nt(q_in, states[s].astype(BF16))
                states[s] = states[s] * carry + update
    for i in range(per):
        st_ref[i] = states[i]


def _bwd_block(i, n_blk):
    return jnp.where(i == 0, 0, n_blk - i)


def _scan0(masks, tri3, q, kf, kb, v, lff, lfb):
    bsz, l, wide = q.shape
    n_blk = l // TM
    bw = SCAN_HEADS * HEAD_W
    fwd = pl.BlockSpec((1, TM, bw), lambda b, h, i: (b, i, h))
    bwd = pl.BlockSpec((1, TM, bw), lambda b, h, i: (b, _bwd_block(i, n_blk), h))
    return pl.pallas_call(
        _scan0_kernel,
        grid=(bsz, wide // bw, n_blk),
        in_specs=[_const_spec(masks.shape), _const_spec(tri3.shape), fwd, fwd, fwd, fwd, bwd, bwd, bwd, bwd],
        out_specs=[fwd, bwd],
        out_shape=[jax.ShapeDtypeStruct((bsz, l, wide), F32)] * 2,
        scratch_shapes=[pltpu.VMEM((2 * SCAN_HEADS, HEAD_W, HEAD_W), F32)],
        compiler_params=_cp(("parallel", "parallel", "arbitrary")),
        name="scan0",
    )(masks, tri3, q, kf, v, lff, q, kb, v, lfb)


def _head_rms(o, w):
    return o * lax.rsqrt(jnp.mean(o * o, axis=-1, keepdims=True) + EPS) * w


def _read0_kernel(x_ref, of_ref, ob_ref, og_ref, nwa_ref, nwb_ref, wout_ref, mod_ref, xo_ref, *, n_batch, d):
    b, j = pl.program_id(0), pl.program_id(1)
    (g1,) = _mod_rows(mod_ref, b, j, n_batch, d, 2, 1)
    o = of_ref[0] + ob_ref[0]
    parts = []
    for hh in range(A_HEADS + B_HEADS):
        nw = nwa_ref[...] if hh < A_HEADS else nwb_ref[...]
        parts.append(_head_rms(o[:, hh * HEAD_W:(hh + 1) * HEAD_W], nw))
    y = jnp.concatenate(parts, axis=-1) * _silu(og_ref[0])
    yo = jnp.dot(y.astype(BF16), wout_ref[...], preferred_element_type=F32)
    xo_ref[0] = x_ref[0] + g1 * yo


def _read0(x, of, ob, og, nwa, nwb, wout, mod, n_batch):
    bsz, l, d = x.shape
    wide = of.shape[-1]
    xb = pl.BlockSpec((1, TM, d), lambda b, j: (b, j, 0))
    wb = pl.BlockSpec((1, TM, wide), lambda b, j: (b, j, 0))
    return pl.pallas_call(
        functools.partial(_read0_kernel, n_batch=n_batch, d=d),
        grid=(bsz, l // TM),
        in_specs=[xb, wb, wb, wb, _const_spec(nwa.shape), _const_spec(nwb.shape),
                  _const_spec(wout.shape), _const_spec(mod.shape)],
        out_specs=xb,
        out_shape=jax.ShapeDtypeStruct(x.shape, F32),
        compiler_params=_cp(("parallel", "arbitrary")),
        name="read0",
    )(x, of, ob, og, nwa, nwb, wout, mod)


def _route_kernel(x_ref, mod_ref, nw_ref, rwt_ref, rb_ref, h_ref, idx_ref, gate_ref, *, n_batch, d, blk_off):
    b, j = pl.program_id(0), pl.program_id(1) + blk_off
    shift, scale = _mod_rows(mod_ref, b, j, n_batch, d, 3, 2)
    h = _norm_mod(x_ref[0], nw_ref[...], shift, scale)
    h_ref[0] = h
    logits = lax.dot_general(rwt_ref[...], h, (((1,), (1,)), ((), ())), precision=lax.Precision.HIGHEST,
                             preferred_element_type=F32) + rb_ref[...]
    n_exp = logits.shape[0]
    rows = lax.broadcasted_iota(jnp.int32, logits.shape, 0)
    vals, idxs = [], []
    for _ in range(TOP_K):
        m = jnp.max(logits, axis=0, keepdims=True)
        i = jnp.min(jnp.where(logits == m, rows, n_exp), axis=0, keepdims=True)
        vals.append(m)
        idxs.append(i)
        logits = jnp.where(rows == i, -jnp.inf, logits)
    ex = [jnp.exp(v - vals[0]) for v in vals]
    tot = ex[0] + ex[1] + ex[2] + ex[3]
    idx_ref[0] = jnp.concatenate(idxs, axis=0)
    gate_ref[0] = jnp.concatenate([e / tot for e in ex], axis=0)


def _route(x, mod, nw, rwt, rb, n_batch, blk_off):
    bsz, l, d = x.shape
    n_blk = l // TM - blk_off
    ls = n_blk * TM
    n_exp = rwt.shape[0]
    return pl.pallas_call(
        functools.partial(_route_kernel, n_batch=n_batch, d=d, blk_off=blk_off),
        grid=(bsz, n_blk),
        in_specs=[pl.BlockSpec((1, TM, d), lambda b, j: (b, j + blk_off, 0)),
                  _const_spec(mod.shape), _const_spec(nw.shape), _const_spec(rwt.shape), _const_spec(rb.shape)],
        out_specs=[pl.BlockSpec((1, TM, d), lambda b, j: (b, j, 0)),
                   pl.BlockSpec((1, TOP_K, TM), lambda b, j: (b, 0, j)),
                   pl.BlockSpec((1, TOP_K, TM), lambda b, j: (b, 0, j))],
        out_shape=[jax.ShapeDtypeStruct((bsz, ls, d), F32),
                   jax.ShapeDtypeStruct((bsz, TOP_K, ls), jnp.int32),
                   jax.ShapeDtypeStruct((bsz, TOP_K, ls), F32)],
        compiler_params=_cp(("parallel", "arbitrary")),
        name="route",
    )(x, mod, nw, rwt, rb)


def _expert_kernel(te_ref, nu_ref, hs_ref, wup_ref, bup_ref, wdn_ref, bdn_ref, y_ref, wup_bf, wdn_bf):
    i = pl.program_id(0)

    @pl.when(jnp.logical_or(i == 0, te_ref[i] != te_ref[jnp.maximum(i - 1, 0)]))
    def _():
        wup_bf[...] = wup_ref[0].astype(BF16)
        wdn_bf[...] = wdn_ref[0].astype(BF16)

    @pl.when(i < nu_ref[0])
    def _():
        h = hs_ref[...].astype(BF16)
        ff = wdn_ref.shape[1]
        y = jnp.zeros(y_ref.shape, F32) + bdn_ref[0]
        fc = min(FF_CHUNK, ff)
        for c in range(0, ff, fc):
            glu = jnp.dot(h, wup_bf[:, c:c + fc], preferred_element_type=F32) + bup_ref[0, :, c:c + fc]
            lin = (jnp.dot(h, wup_bf[:, ff + c:ff + c + fc], preferred_element_type=F32)
                   + bup_ref[0, :, ff + c:ff + c + fc])
            glu = jnp.minimum(glu, SWIGLU_LIMIT)
            lin = jnp.clip(lin, -SWIGLU_LIMIT, SWIGLU_LIMIT)
            act = glu * jax.nn.sigmoid(SWIGLU_ALPHA * glu) * (lin + 1.0)
            y = y + jnp.dot(act.astype(BF16), wdn_bf[c:c + fc, :], preferred_element_type=F32)
        y_ref[...] = y

    @pl.when(i >= nu_ref[0])
    def _():
        y_ref[...] = jnp.zeros_like(y_ref)


def _experts(tile_expert, n_used, hs, wup, bup, wdn, bdn):
    n_rows, d = hs.shape
    n_tiles = n_rows // EXPERT_TILE
    ff = wdn.shape[1]
    grid_spec = pltpu.PrefetchScalarGridSpec(
        num_scalar_prefetch=2,
        grid=(n_tiles,),
        in_specs=[pl.BlockSpec((EXPERT_TILE, d), lambda i, te, nu: (i, 0)),
                  pl.BlockSpec((1, d, 2 * ff), lambda i, te, nu: (te[i], 0, 0)),
                  pl.BlockSpec((1, 1, 2 * ff), lambda i, te, nu: (te[i], 0, 0)),
                  pl.BlockSpec((1, ff, d), lambda i, te, nu: (te[i], 0, 0)),
                  pl.BlockSpec((1, 1, d), lambda i, te, nu: (te[i], 0, 0))],
        out_specs=pl.BlockSpec((EXPERT_TILE, d), lambda i, te, nu: (i, 0)),
        scratch_shapes=[pltpu.VMEM((d, 2 * ff), BF16), pltpu.VMEM((ff, d), BF16)],
    )
    return pl.pallas_call(
        _expert_kernel,
        grid_spec=grid_spec,
        out_shape=jax.ShapeDtypeStruct((n_rows, d), F32),
        compiler_params=_cp(("arbitrary",), vmem=EXPERT_VMEM_LIMIT),
        name="experts",
    )(tile_expert, n_used, hs, wup, bup, wdn, bdn)


GRAN = 8
LOCAL_ROWS = 1280


def _block_plan(idx, n_exp):
    bsz, top_k, ls = idx.shape
    n_blk = ls // TM
    nb = bsz * n_blk
    pairs = top_k * TM
    assert pairs + n_exp * (GRAN - 1) <= LOCAL_ROWS
    e = jnp.transpose(idx.reshape(bsz, top_k, n_blk, TM), (0, 2, 1, 3)).reshape(nb, pairs)
    onehot = e[:, :, None] == jnp.arange(n_exp, dtype=jnp.int32)[None, None, :]
    grouped = onehot.reshape(nb, top_k, TM, n_exp).astype(BF16)
    tri = jnp.tril(jnp.ones((TM, TM), BF16))
    local = jnp.einsum('ts,ngse->ngte', tri, grouped, preferred_element_type=F32)
    g_tot = local[:, :, -1, :]
    g_off = jnp.cumsum(g_tot, axis=1) - g_tot
    cs = (local + g_off[:, :, None, :]).reshape(nb, pairs, n_exp)
    rank = jnp.sum(jnp.where(onehot, cs, 0.0), axis=2).astype(jnp.int32) - 1
    cnt = (g_off[:, -1, :] + g_tot[:, -1, :]).astype(jnp.int32)
    cpad = ((cnt + GRAN - 1) // GRAN) * GRAN
    l_end = jnp.cumsum(cpad, axis=1)
    l_start = l_end - cpad
    ldst = jnp.sum(jnp.where(onehot, l_start[:, None, :], 0), axis=2) + rank
    tot = jnp.sum(cpad, axis=0)
    padded = ((tot + EXPERT_TILE - 1) // EXPERT_TILE) * EXPERT_TILE
    ends = jnp.cumsum(padded)
    starts = ends - padded
    g_start = starts[None, :] + jnp.cumsum(cpad, axis=0) - cpad
    n_gran = LOCAL_ROWS // GRAN
    row0 = jnp.arange(n_gran, dtype=jnp.int32) * GRAN
    gran_e = jnp.sum((row0[None, :, None] >= l_end[:, None, :]).astype(jnp.int32), axis=2)
    gran_oh = jnp.minimum(gran_e, n_exp - 1)[:, :, None] == jnp.arange(n_exp, dtype=jnp.int32)[None, None, :]
    gdst = jnp.sum(jnp.where(gran_oh, (g_start - l_start)[:, None, :], 0), axis=2) + row0[None, :]
    used = (l_end[:, -1] // GRAN).astype(jnp.int32)
    gdst = jnp.where(row0[None, :] < l_end[:, -1:], gdst, 0).astype(jnp.int32)
    n_rows = -(-(nb * pairs + nb * n_exp * (GRAN - 1) + n_exp * EXPERT_TILE) // EXPERT_TILE) * EXPERT_TILE
    n_tiles = n_rows // EXPERT_TILE
    n_used = (ends[-1] // EXPERT_TILE).astype(jnp.int32)
    tile_start = jnp.minimum(jnp.arange(n_tiles, dtype=jnp.int32), n_used - 1) * EXPERT_TILE
    tile_expert = jnp.minimum(jnp.sum((tile_start[:, None] >= ends[None, :]).astype(jnp.int32), axis=1), n_exp - 1)
    ldst = ldst.astype(jnp.int32).reshape(bsz, n_blk, top_k, TM)
    return dict(ldst_rows=jnp.transpose(ldst, (0, 2, 1, 3)).reshape(bsz, top_k, ls), ldst_cols=jnp.transpose(ldst, (0, 1, 3, 2)).reshape(bsz, ls, top_k),
                gdst=gdst.reshape(nb, 1, n_gran), used=used, ends=ends.astype(jnp.int32), padded=padded.astype(jnp.int32),
                tile_expert=tile_expert, n_used=n_used.reshape(1), n_rows=n_rows)


def _gran_copy(src, dst, sem):
    return pltpu.make_async_copy(src, dst, sem)


def _dispatch_blocks_kernel(ends_ref, padded_ref, used_ref, gdst_ref, ldst_ref, h_ref, hs_ref, buf_ref, zero_ref, sem):
    n_blk = pl.num_programs(1)
    blk = pl.program_id(0) * n_blk + pl.program_id(1)

    @pl.when(blk == 0)
    def _():
        zero_ref[...] = jnp.zeros_like(zero_ref)
        for e in range(ends_ref.shape[0]):
            @pl.when(padded_ref[e] > 0)
            def _():
                start = pl.multiple_of(ends_ref[e] - EXPERT_TILE, EXPERT_TILE)
                fill = _gran_copy(zero_ref, hs_ref.at[pl.ds(start, EXPERT_TILE), :], sem)
                fill.start()
                fill.wait()

    rows = lax.broadcasted_iota(jnp.int32, (LOCAL_ROWS, TM), 0)
    hit = rows == ldst_ref[0, 0:1, :]
    for k in range(1, TOP_K):
        hit = jnp.logical_or(hit, rows == ldst_ref[0, k:k + 1, :])
    perm = jnp.where(hit, 1.0, 0.0).astype(BF16)
    buf_ref[...] = jnp.dot(perm, h_ref[0].astype(BF16), preferred_element_type=F32)

    def copy(j):
        src = pl.multiple_of(j * GRAN, GRAN)
        dst = pl.multiple_of(gdst_ref[0, 0, j], GRAN)
        return _gran_copy(buf_ref.at[pl.ds(src, GRAN), :], hs_ref.at[pl.ds(dst, GRAN), :], sem)

    def issue(j, carry):
        copy(j).start()
        return carry

    def drain(j, carry):
        copy(j).wait()
        return carry

    lax.fori_loop(0, used_ref[blk], issue, 0)
    lax.fori_loop(0, used_ref[blk], drain, 0)


def _dispatch_blocks(plan, h):
    bsz, ls, d = h.shape
    n_blk = ls // TM
    n_gran = LOCAL_ROWS // GRAN
    grid_spec = pltpu.PrefetchScalarGridSpec(
        num_scalar_prefetch=3,
        grid=(bsz, n_blk),
        in_specs=[pl.BlockSpec((1, 1, n_gran), lambda b, j, *_: (b * n_blk + j, 0, 0), memory_space=pltpu.SMEM),
                  pl.BlockSpec((1, TOP_K, TM), lambda b, j, *_: (b, 0, j)),
                  pl.BlockSpec((1, TM, d), lambda b, j, *_: (b, j, 0))],
        out_specs=pl.BlockSpec(memory_space=pl.ANY),
        scratch_shapes=[pltpu.VMEM((LOCAL_ROWS, d), F32), pltpu.VMEM((EXPERT_TILE, d), F32),
                        pltpu.SemaphoreType.DMA(())],
    )
    return pl.pallas_call(
        _dispatch_blocks_kernel,
        grid_spec=grid_spec,
        out_shape=jax.ShapeDtypeStruct((plan["n_rows"], d), F32),
        compiler_params=_cp(("arbitrary", "arbitrary")),
        name="dispatch_blocks",
    )(plan["ends"], plan["padded"], plan["used"], plan["gdst"], plan["ldst_rows"], h)


def _combine_blocks_kernel(used_ref, gdst_ref, x_ref, ldst_ref, gate_ref, mod_ref, fw_ref, y_hbm, xo_ref, buf_ref, sem,
                           *, n_batch, d, blk_off, final):
    b, j = pl.program_id(0), pl.program_id(1) + blk_off
    blk = pl.program_id(0) * pl.num_programs(1) + pl.program_id(1)

    @pl.when(blk == 0)
    def _():
        buf_ref[...] = jnp.zeros_like(buf_ref)

    def copy(g):
        dst = pl.multiple_of(g * GRAN, GRAN)
        src = pl.multiple_of(gdst_ref[0, 0, g], GRAN)
        return _gran_copy(y_hbm.at[pl.ds(src, GRAN), :], buf_ref.at[pl.ds(dst, GRAN), :], sem)

    def issue(g, carry):
        copy(g).start()
        return carry

    def drain(g, carry):
        copy(g).wait()
        return carry

    lax.fori_loop(0, used_ref[blk], issue, 0)
    lane = lax.broadcasted_iota(jnp.int32, (TM, LOCAL_ROWS), 1)
    ldst, gate = ldst_ref[0], gate_ref[0]
    weight = jnp.where(lane == ldst[:, 0:1], gate[:, 0:1], 0.0)
    for k in range(1, TOP_K):
        weight = weight + jnp.where(lane == ldst[:, k:k + 1], gate[:, k:k + 1], 0.0)
    lax.fori_loop(0, used_ref[blk], drain, 0)
    acc = jnp.dot(weight.astype(BF16), buf_ref[...].astype(BF16), preferred_element_type=F32)
    (g2,) = _mod_rows(mod_ref, b, j, n_batch, d, 5, 1)
    xo = x_ref[0] + g2 * acc
    if final:
        xo = xo * lax.rsqrt(jnp.mean(xo * xo, axis=-1, keepdims=True) + EPS) * fw_ref[...]
    xo_ref[0] = xo


def _combine_blocks(plan, x, gate, mod, fw, y, n_batch, blk_off, final):
    bsz, l, d = x.shape
    n_blk = l // TM - blk_off
    ls = n_blk * TM
    n_gran = LOCAL_ROWS // GRAN
    grid_spec = pltpu.PrefetchScalarGridSpec(
        num_scalar_prefetch=1,
        grid=(bsz, n_blk),
        in_specs=[pl.BlockSpec((1, 1, n_gran), lambda b, j, *_: (b * n_blk + j, 0, 0), memory_space=pltpu.SMEM),
                  pl.BlockSpec((1, TM, d), lambda b, j, *_: (b, j + blk_off, 0)),
                  pl.BlockSpec((1, TM, TOP_K), lambda b, j, *_: (b, j, 0)),
                  pl.BlockSpec((1, TM, TOP_K), lambda b, j, *_: (b, j, 0)),
                  _const_spec(mod.shape), _const_spec(fw.shape),
                  pl.BlockSpec(memory_space=pl.ANY)],
        out_specs=pl.BlockSpec((1, TM, d), lambda b, j, *_: (b, j, 0)),
        scratch_shapes=[pltpu.VMEM((LOCAL_ROWS, d), F32), pltpu.SemaphoreType.DMA(())],
    )
    return pl.pallas_call(
        functools.partial(_combine_blocks_kernel, n_batch=n_batch, d=d, blk_off=blk_off, final=final),
        grid_spec=grid_spec,
        out_shape=jax.ShapeDtypeStruct((bsz, ls, d), F32),
        compiler_params=_cp(("arbitrary", "arbitrary")),
        name="combine_blocks",
    )(plan["used"], plan["gdst"], x, plan["ldst_cols"], gate, mod, fw, y)


def _moe_blocks(x, mod, nw, rw, rb, wup, bup, wdn, bdn, fw, n_batch, blk_off, final):
    n_exp = rw.shape[1]
    d = x.shape[-1]
    h, idx, gate = _route(x, mod, nw, rw.T, rb.reshape(n_exp, 1), n_batch, blk_off)
    plan = _block_plan(idx, n_exp)
    hs = _dispatch_blocks(plan, h)
    y = _experts(plan["tile_expert"], plan["n_used"], hs, wup, bup.reshape(n_exp, 1, -1), wdn, bdn.reshape(n_exp, 1, d))
    return _combine_blocks(plan, x, jnp.transpose(gate, (0, 2, 1)), mod, fw, y, n_batch, blk_off, final)


def _pad_heads(w, n_heads, width):
    lead = w.shape[:-1]
    dh = w.shape[-1] // n_heads
    w = w.reshape(lead + (n_heads, dh))
    w = jnp.pad(w, [(0, 0)] * len(lead) + [(0, 0), (0, width - dh)])
    return w.reshape(lead + (n_heads * width,))


def _layer0_weights(w_in, w2_f, b2_f, w2_b, b2_b):
    d = w_in.shape[0]
    sizes = (A_HEADS * A_DK, A_HEADS * A_DK, A_HEADS * A_DV, A_RANK, A_RANK, A_HEADS * A_DV,
             B_HEADS * B_DK, B_HEADS * B_DK, B_HEADS * B_DK, B_HEADS * B_DV, B_HEADS * B_DV)
    aq, ak, av, ar_f, ar_b, aog, bq, bz_f, bz_b, bi, bog = jnp.split(w_in, np.cumsum(sizes)[:-1].tolist(), axis=1)
    ar = jnp.concatenate([ar_f, ar_b, jnp.zeros((d, HEAD_W - 2 * A_RANK), w_in.dtype)], axis=1)
    w = jnp.concatenate([_pad_heads(aq, A_HEADS, HEAD_W), bq, av, bi, aog, bog,
                         _pad_heads(ak, A_HEADS, HEAD_W), bz_f, bz_b, ar], axis=1).astype(BF16)
    hw = A_HEADS * HEAD_W
    w2 = jnp.zeros((2, HEAD_W, hw), F32)
    w2 = w2.at[0, 0:A_RANK].set(_pad_heads(w2_f, A_HEADS, HEAD_W))
    w2 = w2.at[1, A_RANK:2 * A_RANK].set(_pad_heads(w2_b, A_HEADS, HEAD_W))
    b2 = jnp.stack([_pad_heads(b2_f, A_HEADS, HEAD_W), _pad_heads(b2_b, A_HEADS, HEAD_W)]).reshape(2, 1, hw)
    return w, w2.astype(BF16), b2


QKV_W = 2 * C_HEADS * C_DK + C_HEADS * C_DV
OG_W = C_HEADS * C_DV
DQ_W = D_HEADS * D_HD
DKV_W = D_KV_HEADS * HEAD_W


def _proj1_kernel(x_ref, mod_ref, nw_ref, w_ref, cos_ref, sin_ref,
                  qkv_ref, gb_ref, og_ref, dq_ref, dk_ref, dv_ref, *, n_batch, d):
    b, j = pl.program_id(0), pl.program_id(1)
    shift, scale = _mod_rows(mod_ref, b, j, n_batch, d, 0, 2)
    h = _norm_mod(x_ref[0], nw_ref[...], shift, scale).astype(BF16)

    def mm(c0, c1):
        return jnp.dot(h, w_ref[:, c0:c1], preferred_element_type=F32)

    c = 0
    qkv_ref[0] = mm(c, c + QKV_W); c += QKV_W
    gb_ref[0] = mm(c, c + HEAD_W); c += HEAD_W
    og_ref[0] = mm(c, c + OG_W); c += OG_W
    cos, sin = cos_ref[...], sin_ref[...]
    for s in range(DQ_W // HEAD_W):
        xs = mm(c + s * HEAD_W, c + (s + 1) * HEAD_W)
        xp = mm(c + DQ_W + s * HEAD_W, c + DQ_W + (s + 1) * HEAD_W)
        dq_ref[0, :, s * HEAD_W:(s + 1) * HEAD_W] = (xs * cos + xp * sin) * (D_HD ** -0.5)
    c += 2 * DQ_W
    for s in range(DKV_W // HEAD_W):
        xs = mm(c + s * HEAD_W, c + (s + 1) * HEAD_W)
        xp = mm(c + DKV_W + s * HEAD_W, c + DKV_W + (s + 1) * HEAD_W)
        dk_ref[0, :, s * HEAD_W:(s + 1) * HEAD_W] = xs * cos + xp * sin
    c += 2 * DKV_W
    dv_ref[0] = mm(c, c + DKV_W)


def _proj1(x, mod, nw, w, cos, sin, n_batch):
    bsz, l, d = x.shape
    widths = (QKV_W, HEAD_W, OG_W, DQ_W, DKV_W, DKV_W)
    return pl.pallas_call(
        functools.partial(_proj1_kernel, n_batch=n_batch, d=d),
        grid=(bsz, l // TM),
        in_specs=[pl.BlockSpec((1, TM, d), lambda b, j: (b, j, 0)),
                  _const_spec(mod.shape), _const_spec(nw.shape), _const_spec(w.shape),
                  pl.BlockSpec((TM, HEAD_W), lambda b, j: (j, 0)),
                  pl.BlockSpec((TM, HEAD_W), lambda b, j: (j, 0))],
        out_specs=[pl.BlockSpec((1, TM, wd), lambda b, j: (b, j, 0)) for wd in widths],
        out_shape=[jax.ShapeDtypeStruct((bsz, l, wd), F32) for wd in widths],
        compiler_params=_cp(("parallel", "arbitrary")),
        name="proj1",
    )(x, mod, nw, w, cos, sin)


HALO = 8


def _conv1_kernel(cur_ref, prev_ref, next_ref, cw_ref, gbr_ref, alog_ref, dtb_ref,
                  q_ref, k_ref, v_ref, gb_ref, xe_ref):
    j, n_blk = pl.program_id(1), pl.num_programs(1)
    prev_ok = j >= 2
    next_ok = jnp.logical_and(j >= 1, j < n_blk - 1)
    xe_ref[0:HALO, :] = jnp.where(prev_ok, prev_ref[0], 0.0)
    xe_ref[HALO:HALO + TM, :] = cur_ref[0]
    xe_ref[HALO + TM:2 * HALO + TM, :] = jnp.where(next_ok, next_ref[0], 0.0)
    pad = CONV_W // 2
    hw = C_HEADS * HEAD_W
    for g, o_ref in enumerate((q_ref, k_ref, v_ref)):
        cols = slice(g * hw, (g + 1) * hw)
        acc = cw_ref[0:1, cols] * xe_ref[HALO - pad:HALO - pad + TM, cols]
        for i in range(1, CONV_W):
            acc = acc + cw_ref[i:i + 1, cols] * xe_ref[HALO - pad + i:HALO - pad + i + TM, cols]
        y = _silu(acc)
        if g < 2:
            parts = []
            for hh in range(C_HEADS):
                yh = y[:, hh * HEAD_W:(hh + 1) * HEAD_W]
                yh = yh * lax.rsqrt(jnp.sum(yh * yh, axis=-1, keepdims=True) + EPS)
                parts.append(yh * (C_DK ** -0.5) if g == 0 else yh)
            y = jnp.concatenate(parts, axis=-1)
        o_ref[0] = y
    raw = gbr_ref[0]
    z = raw + dtb_ref[...]
    softplus = jnp.maximum(z, 0.0) + _log1p_exp_neg_abs(z)
    lane = lax.broadcasted_iota(jnp.int32, raw.shape, 1)
    gb_ref[0] = jnp.where(lane < 2 * C_HEADS, jax.nn.sigmoid(raw), -jnp.exp(alog_ref[...]) * softplus)


def _conv1(qkv, conv_w, gb_raw, alog, dtb):
    bsz, l, wd = qkv.shape
    n_blk = l // TM
    per = TM // HALO
    last = l // HALO - 1
    hw = C_HEADS * HEAD_W
    ob = pl.BlockSpec((1, TM, hw), lambda b, j: (b, j, 0))
    gbs = pl.BlockSpec((1, TM, HEAD_W), lambda b, j: (b, j, 0))
    return pl.pallas_call(
        _conv1_kernel,
        grid=(bsz, n_blk),
        in_specs=[pl.BlockSpec((1, TM, wd), lambda b, j: (b, j, 0)),
                  pl.BlockSpec((1, HALO, wd), lambda b, j: (b, jnp.maximum(j * per - 1, 0), 0)),
                  pl.BlockSpec((1, HALO, wd), lambda b, j: (b, jnp.minimum((j + 1) * per, last), 0)),
                  _const_spec(conv_w.shape), gbs, _const_spec(alog.shape), _const_spec(dtb.shape)],
        out_specs=[ob, ob, ob, gbs],
        out_shape=[jax.ShapeDtypeStruct((bsz, l, hw), F32)] * 3 + [jax.ShapeDtypeStruct((bsz, l, HEAD_W), F32)],
        scratch_shapes=[pltpu.VMEM((TM + 2 * HALO, wd), F32)],
        compiler_params=_cp(("parallel", "arbitrary")),
        name="conv1",
    )(qkv, qkv, qkv, conv_w, gb_raw, alog, dtb)


_MERGE = (16, 32, 64)
_BASE = 8


def _delta_masks():
    t = np.arange(CHUNK)[:, None]
    s = np.arange(CHUNK)[None, :]
    out = [s <= t, s < t, s >= t, s > t, (t // _BASE) == (s // _BASE)]
    for m2 in _MERGE:
        out.append(((t // m2) == (s // m2)) & ((t // (m2 // 2)) != (s // (m2 // 2))))
    return np.stack(out).astype(np.float32)


def _bdot(a, b):
    return jnp.dot(a.astype(BF16), b.astype(BF16), preferred_element_type=F32)


DELTA_HEADS = 4


def _delta_prepare(chains, dm):
    n = range(len(chains))
    qs, ks, vs, gs, betas, revs = zip(*chains)
    incl = [dm[2] if r else dm[0] for r in revs]
    strict = [dm[3] if r else dm[1] for r in revs]
    cum = [_cumsum_rows(jnp.broadcast_to(gs[i], qs[i].shape), revs[i]) for i in n]
    decay = [incl[i] * jnp.exp(jnp.where(incl[i] > 0, cum[i][:, 0:CHUNK] - cum[i].T[0:CHUNK, :], 0.0)) for i in n]
    kbeta = [ks[i] * betas[i] for i in n]
    kbf = [ks[i].astype(BF16) for i in n]
    a = [strict[i] * _dot_nt(kbeta[i].astype(BF16), kbf[i]) * decay[i] for i in n]
    eye = dm[0] * dm[2]
    n0 = [-(a[i] * dm[4]) for i in n]
    n2 = [_bdot(n0[i], n0[i]) for i in n]
    n4 = [_bdot(n2[i], n2[i]) for i in n]
    t = [eye + n0[i] for i in n]
    t = [t[i] + _bdot(t[i], n2[i]) for i in n]
    t = [t[i] + _bdot(t[i], n4[i]) for i in n]
    for li in range(len(_MERGE)):
        inner = [_bdot(a[i] * dm[5 + li], t[i]) for i in n]
        t = [t[i] - _bdot(t[i], inner[i]) for i in n]
    uw = [_bdot(t[i], jnp.concatenate([vs[i] * betas[i], kbeta[i] * jnp.exp(cum[i])], axis=1)) for i in n]
    scores = [(_dot_nt(qs[i].astype(BF16), kbf[i]) * decay[i]).astype(BF16) for i in n]
    edge = [cum[i][0:1] if revs[i] else cum[i][CHUNK - 1:CHUNK] for i in n]
    q_in = [(qs[i] * jnp.exp(cum[i])).astype(BF16) for i in n]
    k_out = [(ks[i] * jnp.exp(edge[i] - cum[i])).astype(BF16) for i in n]
    carry = [jnp.exp(edge[i]) for i in n]
    return [(uw[i][:, 0:HEAD_W], uw[i][:, HEAD_W:].astype(BF16), scores[i], q_in[i], k_out[i], carry[i]) for i in n]


def _delta_step(prep, states):
    n = range(len(prep))
    sb = [states[i].astype(BF16) for i in n]
    v_new = [prep[i][0] - jnp.dot(prep[i][1], sb[i], preferred_element_type=F32) for i in n]
    vb = [v_new[i].astype(BF16) for i in n]
    o = [jnp.dot(prep[i][3], sb[i], preferred_element_type=F32)
         + jnp.dot(prep[i][2], vb[i], preferred_element_type=F32) for i in n]
    new = [states[i] * prep[i][5] + _dot_tn(prep[i][4], vb[i]) for i in n]
    return o, new


def _lane_col(x, lane_idx):
    lane = lax.broadcasted_iota(jnp.int32, x.shape, 1)
    return jnp.sum(jnp.where(lane == lane_idx, x, 0.0), axis=-1, keepdims=True)


def _delta_kernel(dm_ref, qf_ref, kf_ref, vf_ref, gbf_ref, qb_ref, kb_ref, vb_ref, gbb_ref,
                  of_ref, ob_ref, st_ref):
    @pl.when(pl.program_id(2) == 0)
    def _():
        st_ref[...] = jnp.zeros_like(st_ref)

    dm = [dm_ref[i] for i in range(dm_ref.shape[0])]
    n_ch = TM // CHUNK
    gbf, gbb = gbf_ref[0], gbb_ref[0]
    chains = []
    cols = []
    for hh in range(DELTA_HEADS):
        h = pl.program_id(1) * DELTA_HEADS + hh
        cols.append((_lane_col(gbf, h), _lane_col(gbf, 2 * C_HEADS + h),
                     _lane_col(gbb, C_HEADS + h), _lane_col(gbb, 3 * C_HEADS + h)))
    for c in range(n_ch):
        rf = slice(c * CHUNK, (c + 1) * CHUNK)
        rb = slice((n_ch - 1 - c) * CHUNK, (n_ch - c) * CHUNK)
        for hh in range(DELTA_HEADS):
            lanes = slice(hh * HEAD_W, (hh + 1) * HEAD_W)
            beta_f, g_f, beta_b, g_b = cols[hh]
            chains.append((qf_ref[0, rf, lanes], kf_ref[0, rf, lanes], vf_ref[0, rf, lanes], g_f[rf], beta_f[rf], False))
            chains.append((qb_ref[0, rb, lanes], kb_ref[0, rb, lanes], vb_ref[0, rb, lanes], g_b[rb], beta_b[rb], True))
    prep = _delta_prepare(chains, dm)
    per = 2 * DELTA_HEADS
    states = [st_ref[i] for i in range(per)]
    for c in range(n_ch):
        rf = slice(c * CHUNK, (c + 1) * CHUNK)
        rb = slice((n_ch - 1 - c) * CHUNK, (n_ch - c) * CHUNK)
        outs, states = _delta_step(prep[c * per:(c + 1) * per], states)
        for hh in range(DELTA_HEADS):
            lanes = slice(hh * HEAD_W, (hh + 1) * HEAD_W)
            of_ref[0, rf, lanes] = outs[2 * hh]
            ob_ref[0, rb, lanes] = outs[2 * hh + 1]
    for i in range(per):
        st_ref[i] = states[i]


def _delta(dmasks, q, k, v, gb):
    bsz, l, wide = q.shape
    n_blk = l // TM
    bw = DELTA_HEADS * HEAD_W
    fwd = pl.BlockSpec((1, TM, bw), lambda b, h, i: (b, i, h))
    bwd = pl.BlockSpec((1, TM, bw), lambda b, h, i: (b, _bwd_block(i, n_blk), h))
    gf = pl.BlockSpec((1, TM, HEAD_W), lambda b, h, i: (b, i, 0))
    gbw = pl.BlockSpec((1, TM, HEAD_W), lambda b, h, i: (b, _bwd_block(i, n_blk), 0))
    return pl.pallas_call(
        _delta_kernel,
        grid=(bsz, wide // bw, n_blk),
        in_specs=[_const_spec(dmasks.shape), fwd, fwd, fwd, gf, bwd, bwd, bwd, gbw],
        out_specs=[fwd, bwd],
        out_shape=[jax.ShapeDtypeStruct((bsz, l, wide), F32)] * 2,
        scratch_shapes=[pltpu.VMEM((2 * DELTA_HEADS, HEAD_W, HEAD_W), F32)],
        compiler_params=_cp(("parallel", "parallel", "arbitrary")),
        name="delta",
    )(dmasks, q, k, v, gb, q, k, v, gb)


def _attn_kernel(q_ref, kp_ref, kc_ref, kn_ref, vp_ref, vc_ref, vn_ref, kx_ref, vx_ref, sink_ref, o_ref, *, t_len):
    i = pl.program_id(1)
    q = q_ref[0]
    kl = jnp.concatenate([kp_ref[0], kc_ref[0], kn_ref[0]], axis=0).astype(BF16)
    vl = jnp.concatenate([vp_ref[0], vc_ref[0], vn_ref[0]], axis=0).astype(BF16)
    kx, vx = kx_ref[0].astype(BF16), vx_ref[0].astype(BF16)
    qpos = lax.broadcasted_iota(jnp.int32, (WINDOW, 3 * WINDOW), 0)
    kpos = lax.broadcasted_iota(jnp.int32, (WINDOW, 3 * WINDOW), 1) - WINDOW
    k_abs = i * WINDOW + kpos
    valid = (jnp.abs(kpos - qpos) <= WINDOW) & (k_abs >= 0) & (k_abs < t_len)
    low = lax.broadcasted_iota(jnp.int32, (WINDOW, HEAD_W), 1) < D_HD
    group = D_HEADS // D_KV_HEADS
    heads = range(D_HEADS)
    cols = [slice((h // group) * HEAD_W, (h // group + 1) * HEAD_W) for h in heads]
    qm = [jnp.where(low if h % 2 == 0 else jnp.logical_not(low), q[:, (h // 2) * HEAD_W:(h // 2 + 1) * HEAD_W], 0.0)
          .astype(BF16) for h in heads]
    s_l = [jnp.where(valid, _dot_nt(qm[h], kl[:, cols[h]]), -jnp.inf) for h in heads]
    s_x = [_dot_nt(qm[h], kx[:, cols[h]]) for h in heads]
    sink = [sink_ref[:, h:h + 1] for h in heads]
    def fold(a, b, op):
        slabs = [a[:, c:c + HEAD_W] for c in range(0, a.shape[1], HEAD_W)]
        slabs += [b[:, c:c + HEAD_W] for c in range(0, b.shape[1], HEAD_W)]
        acc = slabs[0]
        for s in slabs[1:]:
            acc = op(acc, s)
        return acc

    m = [jnp.maximum(jnp.max(fold(s_l[h], s_x[h], jnp.maximum), axis=-1, keepdims=True), sink[h]) for h in heads]
    p_l = [jnp.exp(s_l[h] - m[h]) for h in heads]
    p_x = [jnp.exp(s_x[h] - m[h]) for h in heads]
    den = [jnp.sum(fold(p_l[h], p_x[h], jnp.add), axis=-1, keepdims=True) + jnp.exp(sink[h] - m[h]) for h in heads]
    o = [(jnp.dot(p_l[h].astype(BF16), vl[:, cols[h]], preferred_element_type=F32)
          + jnp.dot(p_x[h].astype(BF16), vx[:, cols[h]], preferred_element_type=F32)) / den[h] for h in heads]
    for p in range(D_HEADS // 2):
        o_ref[0, :, p * HEAD_W:(p + 1) * HEAD_W] = jnp.where(low, o[2 * p], o[2 * p + 1])


def _attn(dq, dk, dv, sinks, n_ctx):
    bsz, l, _ = dq.shape
    t_len = l - n_ctx
    nq = t_len // WINDOW
    off = n_ctx // WINDOW

    def kv(delta):
        return pl.BlockSpec((1, WINDOW, DKV_W), lambda b, i: (b, off + jnp.clip(i + delta, 0, nq - 1), 0))

    ctx = pl.BlockSpec((1, n_ctx, DKV_W), lambda b, i: (b, 0, 0))
    return pl.pallas_call(
        functools.partial(_attn_kernel, t_len=t_len),
        grid=(bsz, nq),
        in_specs=[pl.BlockSpec((1, WINDOW, DQ_W), lambda b, i: (b, off + i, 0)),
                  kv(-1), kv(0), kv(1), kv(-1), kv(0), kv(1), ctx, ctx, _const_spec(sinks.shape)],
        out_specs=pl.BlockSpec((1, WINDOW, DQ_W), lambda b, i: (b, i, 0)),
        out_shape=jax.ShapeDtypeStruct((bsz, t_len, DQ_W), F32),
        compiler_params=_cp(("parallel", "arbitrary")),
        name="attn",
    )(dq, dk, dk, dk, dv, dv, dv, dk, dv, sinks)


def _read1_kernel(x_ref, of_ref, ob_ref, og_ref, od_ref, nw_ref, wout_ref, mod_ref, xo_ref, *, n_batch, d):
    b, j = pl.program_id(0), pl.program_id(1) + 1
    (g1,) = _mod_rows(mod_ref, b, j, n_batch, d, 2, 1)
    o = of_ref[0] + ob_ref[0]
    parts = [_head_rms(o[:, hh * HEAD_W:(hh + 1) * HEAD_W], nw_ref[...]) for hh in range(C_HEADS)]
    y = jnp.concatenate(parts, axis=-1) * _silu(og_ref[0])
    y = jnp.concatenate([y, od_ref[0]], axis=-1)
    xo_ref[0] = x_ref[0] + g1 * jnp.dot(y.astype(BF16), wout_ref[...], preferred_element_type=F32)


def _read1(x, of, ob, og, od, nw, wout, mod, n_batch):
    bsz, l, d = x.shape
    lat = lambda wd: pl.BlockSpec((1, TM, wd), lambda b, j: (b, j + 1, 0))
    return pl.pallas_call(
        functools.partial(_read1_kernel, n_batch=n_batch, d=d),
        grid=(bsz, l // TM - 1),
        in_specs=[lat(d), lat(OG_W), lat(OG_W), lat(OG_W),
                  pl.BlockSpec((1, TM, DQ_W), lambda b, j: (b, j, 0)),
                  _const_spec(nw.shape), _const_spec(wout.shape), _const_spec(mod.shape)],
        out_specs=lat(d),
        out_shape=jax.ShapeDtypeStruct(x.shape, F32),
        input_output_aliases={0: 0},
        compiler_params=_cp(("parallel", "arbitrary")),
        name="read1",
    )(x, of, ob, og, od, nw, wout, mod)


def _swap_halves(w, n_heads):
    lead = w.shape[:-1]
    dh = w.shape[-1] // n_heads
    w = w.reshape(lead + (n_heads, 2, dh // 2))
    return w[..., ::-1, :].reshape(lead + (n_heads * dh,))


def _dup_heads(w, n_heads):
    lead = w.shape[:-1]
    dh = w.shape[-1] // n_heads
    w = w.reshape(lead + (n_heads, 1, dh))
    return jnp.concatenate([w, w], axis=-2).reshape(lead + (2 * n_heads * dh,))


def _layer1_weights(w_in):
    d = w_in.shape[0]
    sizes = (C_HEADS * C_DK, C_HEADS * C_DK, C_HEADS * C_DV, C_HEADS, C_HEADS, C_HEADS, C_HEADS,
             C_HEADS * C_DV, D_HEADS * D_HD, D_KV_HEADS * D_HD, D_KV_HEADS * D_HD)
    cq, ck, cv, bt_f, bt_b, a_f, a_b, og, dq, dk, dv = jnp.split(w_in, np.cumsum(sizes)[:-1].tolist(), axis=1)
    gates = jnp.concatenate([bt_f, bt_b, a_f, a_b, jnp.zeros((d, HEAD_W - 4 * C_HEADS), w_in.dtype)], axis=1)
    return jnp.concatenate([cq, ck, cv, gates, og, dq, _swap_halves(dq, D_HEADS),
                            _dup_heads(dk, D_KV_HEADS), _dup_heads(_swap_halves(dk, D_KV_HEADS), D_KV_HEADS),
                            _dup_heads(dv, D_KV_HEADS)], axis=1).astype(BF16)


def _rope_tables(n_ctx, t_len):
    rows = t_len // GRID_W
    row = jnp.repeat(jnp.arange(rows, dtype=F32), GRID_W)
    col = jnp.tile(jnp.arange(GRID_W, dtype=F32), rows)
    n_freq = D_HD // 4
    inv = ROPE_BASE ** (-jnp.arange(n_freq, dtype=F32) / n_freq)
    ang = jnp.concatenate([row[:, None] * inv, col[:, None] * inv], axis=-1)
    cos, sin = jnp.cos(ang), jnp.sin(ang)
    cos = jnp.concatenate([jnp.ones((n_ctx, D_HD // 2), F32), cos], axis=0)
    sin = jnp.concatenate([jnp.zeros((n_ctx, D_HD // 2), F32), sin], axis=0)
    return jnp.concatenate([cos] * 4, axis=1), jnp.concatenate([-sin, sin] * 2, axis=1)


def _mixer1(xx, mod, nw, w_in, w_out, conv_w, a_log_f, dt_bias_f, a_log_b, dt_bias_b, dn_norm_w, sinks, n_batch, n_ctx):
    l = xx.shape[1]
    cos, sin = _rope_tables(n_ctx, l - n_ctx)
    qkv, gb_raw, og, dq, dk, dv = _proj1(xx, mod, nw, _layer1_weights(w_in), cos, sin, n_batch)
    zero4 = jnp.zeros((C_HEADS,), F32)
    fill = jnp.zeros((HEAD_W - 4 * C_HEADS,), F32)
    alog = jnp.concatenate([zero4, zero4, a_log_f, a_log_b, fill]).reshape(1, HEAD_W)
    dtb = jnp.concatenate([zero4, zero4, dt_bias_f, dt_bias_b, fill]).reshape(1, HEAD_W)
    cq, ck, cv, gb = _conv1(qkv, conv_w, gb_raw, alog, dtb)
    of, ob = _delta(jnp.asarray(_delta_masks()), cq, ck, cv, gb)
    sink_row = jnp.concatenate([sinks, jnp.zeros((HEAD_W - D_HEADS,), F32)]).reshape(1, HEAD_W)
    od = _attn(dq, dk, dv, sink_row, n_ctx)
    return _read1(xx, of, ob, og, od, dn_norm_w.reshape(1, -1), w_out.astype(BF16), mod, n_batch)


def kernel(x, c, ctx, c_ctx, l0_ada_w, l0_ada_b, l0_norm_mix_w, l0_w_in, l0_w_out, l0_gla_w2_f, l0_gla_b_f, l0_gla_w2_b, l0_gla_b_b, l0_gla_norm_w, l0_hgrn_norm_w, hgrn_lb_logits, l0_norm_ffn_w, l0_router_w, l0_router_b, l0_w_up, l0_b_up, l0_w_down, l0_b_down, l1_ada_w, l1_ada_b, l1_norm_mix_w, l1_w_in, l1_w_out, l1_conv_w, l1_a_log_f, l1_dt_bias_f, l1_a_log_b, l1_dt_bias_b, l1_dn_norm_w, l1_sinks, l1_norm_ffn_w, l1_router_w, l1_router_b, l1_w_up, l1_b_up, l1_w_down, l1_b_down, final_norm_w):
    bsz, t, d = x.shape
    n_ctx = ctx.shape[1]
    assert n_ctx == TM and t % TM == 0
    xx = jnp.concatenate([ctx, x], axis=1)
    mod_rows = -(-(bsz + 1) // 8) * 8
    c_all = jnp.zeros((mod_rows, d), F32).at[:bsz].set(c).at[bsz].set(c_ctx)
    ones = jnp.ones((1, d), F32)

    mod0 = _ada_table(c_all, l0_ada_w, l0_ada_b)
    w0, w2, b2 = _layer0_weights(l0_w_in, l0_gla_w2_f, l0_gla_b_f, l0_gla_w2_b, l0_gla_b_b)
    q, kf, kb, v, lff, lfb, og = _proj0(xx, mod0, l0_norm_mix_w.reshape(1, d), w0, w2, b2, hgrn_lb_logits, bsz)
    of, ob = _scan0(jnp.asarray(_level_masks()), jnp.asarray(_tri3(), BF16), q, kf, kb, v, lff, lfb)
    xx = _read0(xx, of, ob, og, l0_gla_norm_w.reshape(1, -1), l0_hgrn_norm_w.reshape(1, -1), l0_w_out.astype(BF16), mod0, bsz)
    xx = _moe_blocks(xx, mod0, l0_norm_ffn_w.reshape(1, d), l0_router_w, l0_router_b, l0_w_up, l0_b_up, l0_w_down, l0_b_down,
              ones, bsz, 0, False)

    mod1 = _ada_table(c_all, l1_ada_w, l1_ada_b)
    xx = _mixer1(xx, mod1, l1_norm_mix_w.reshape(1, d), l1_w_in, l1_w_out, l1_conv_w, l1_a_log_f, l1_dt_bias_f,
                 l1_a_log_b, l1_dt_bias_b, l1_dn_norm_w, l1_sinks, bsz, n_ctx)
    return _moe_blocks(xx, mod1, l1_norm_ffn_w.reshape(1, d), l1_router_w, l1_router_b, l1_w_up, l1_b_up, l1_w_down, l1_b_down,
                final_norm_w.reshape(1, d), bsz, 1, True)
```

```python
import functools

import numpy as np
import jax
import jax.numpy as jnp
from jax import lax
from jax.experimental import pallas as pl
from jax.experimental.pallas import tpu as pltpu

F32 = jnp.float32
BF16 = jnp.bfloat16
EPS = 1e-6

CHUNK = 64
A_HEADS, A_DK, A_DV, A_RANK = 4, 64, 128, 16
GATE_TAU = 16.0
B_HEADS, B_DK, B_DV = 4, 128, 128
C_HEADS, C_DK, C_DV = 4, 128, 128
CONV_W = 5
D_HEADS, D_KV_HEADS, D_HD = 8, 2, 64
WINDOW = 128
GRID_W = 64
ROPE_BASE = 10000.0
TOP_K = 4
SWIGLU_LIMIT = 7.0
SWIGLU_ALPHA = 1.702

TM = 256
HEAD_W = 128
EXPERT_TILE = 512
FF_CHUNK = 512
VMEM_LIMIT = 48 * 1024 * 1024
EXPERT_VMEM_LIMIT = 56 * 1024 * 1024


def _cp(sem, vmem=VMEM_LIMIT):
    return pltpu.CompilerParams(dimension_semantics=sem, vmem_limit_bytes=vmem)


def _const_spec(shape):
    nd = len(shape)
    return pl.BlockSpec(shape, lambda *_: (0,) * nd)


def _silu(x):
    return x * jax.nn.sigmoid(x)


def _log1p_exp_neg_abs(z):
    return jnp.log(1.0 + jnp.exp(-jnp.abs(z)))


def _log_sigmoid(z):
    return jnp.minimum(z, 0.0) - _log1p_exp_neg_abs(z)


def _norm_mod(x, nw, shift, scale):
    y = x * lax.rsqrt(jnp.mean(x * x, axis=-1, keepdims=True) + EPS) * nw
    return y * (1.0 + scale) + shift


def _mod_rows(mod_ref, batch, blk, n_batch, d, first, count):
    row = jnp.where(blk == 0, n_batch, batch)
    return [mod_ref[pl.ds(row, 1), (first + i) * d:(first + i + 1) * d] for i in range(count)]


def _ada_kernel(c_ref, w_ref, b_ref, o_ref):
    s = _silu(c_ref[...])
    o_ref[...] = jnp.dot(s, w_ref[...], precision=lax.Precision.HIGHEST,
                         preferred_element_type=F32) + b_ref[...]


def _ada_table(c_all, w, b):
    rows, d = c_all.shape
    n = w.shape[1]
    bn = d
    return pl.pallas_call(
        _ada_kernel,
        grid=(n // bn,),
        in_specs=[pl.BlockSpec((rows, d), lambda j: (0, 0)),
                  pl.BlockSpec((d, bn), lambda j: (0, j)),
                  pl.BlockSpec((1, bn), lambda j: (0, j))],
        out_specs=pl.BlockSpec((rows, bn), lambda j: (0, j)),
        out_shape=jax.ShapeDtypeStruct((rows, n), F32),
        compiler_params=_cp(("arbitrary",)),
        name="ada_table",
    )(c_all, w, b.reshape(1, n))


def _proj0_kernel(x_ref, mod_ref, nw_ref, w_ref, w2_ref, b2_ref, lbl_ref,
                  q_ref, kf_ref, kb_ref, v_ref, lff_ref, lfb_ref, og_ref, *, n_batch, d):
    b, j = pl.program_id(0), pl.program_id(1)
    shift, scale = _mod_rows(mod_ref, b, j, n_batch, d, 0, 2)
    h = _norm_mod(x_ref[0], nw_ref[...], shift, scale).astype(BF16)

    def mm(c0, c1):
        return jnp.dot(h, w_ref[:, c0:c1], preferred_element_type=F32)

    hw = A_HEADS * HEAD_W

    def plain_outputs():
        q_ref[0, :, 0:hw] = mm(0, hw) * (A_DK ** -0.5)
        q_ref[0, :, hw:2 * hw] = mm(hw, 2 * hw)
        v_ref[0] = mm(2 * hw, 4 * hw)
        og_ref[0] = mm(4 * hw, 6 * hw)
        kg = mm(6 * hw, 7 * hw)
        kf_ref[0, :, 0:hw] = kg
        kb_ref[0, :, 0:hw] = kg

    lg = lbl_ref[...]
    e = jnp.exp(lg - jnp.max(lg, axis=0, keepdims=True))
    lb = e[0:1] / jnp.sum(e, axis=0, keepdims=True)
    log_lb, log_1m = jnp.log(lb), jnp.log1p(-lb)
    ar = mm(9 * hw, 9 * hw + HEAD_W).astype(BF16)
    for di, (k_ref, lf_ref) in enumerate(((kf_ref, lff_ref), (kb_ref, lfb_ref))):
        z = mm((7 + di) * hw, (8 + di) * hw)
        s1 = log_1m + _log_sigmoid(z)
        lf_ref[0, :, hw:2 * hw] = jnp.maximum(log_lb, s1) + _log1p_exp_neg_abs(log_lb - s1)
        k_ref[0, :, hw:2 * hw] = (1.0 - lb) * jax.nn.sigmoid(-z)
        za = jnp.dot(ar, w2_ref[di], preferred_element_type=F32) + b2_ref[di]
        lf_ref[0, :, 0:hw] = _log_sigmoid(za) * (1.0 / GATE_TAU)
    plain_outputs()


def _proj0(x, mod, nw, w, w2, b2, lbl, n_batch):
    bsz, l, d = x.shape
    n_blk = l // TM
    wide = 2 * A_HEADS * HEAD_W
    blk = pl.BlockSpec((1, TM, wide), lambda b, j: (b, j, 0))
    return pl.pallas_call(
        functools.partial(_proj0_kernel, n_batch=n_batch, d=d),
        grid=(bsz, n_blk),
        in_specs=[pl.BlockSpec((1, TM, d), lambda b, j: (b, j, 0)),
                  _const_spec(mod.shape), _const_spec(nw.shape), _const_spec(w.shape),
                  _const_spec(w2.shape), _const_spec(b2.shape), _const_spec(lbl.shape)],
        out_specs=[blk] * 7,
        out_shape=[jax.ShapeDtypeStruct((bsz, l, wide), F32)] * 7,
        compiler_params=_cp(("parallel", "arbitrary")),
        name="proj0",
    )(x, mod, nw, w, w2, b2, lbl)


_LEVELS = (32, 16, 8, 4, 2, 1)


def _level_masks():
    t = np.arange(CHUNK)[:, None]
    s = np.arange(CHUNK)[None, :]
    out = np.zeros((2, len(_LEVELS) + 1, CHUNK, CHUNK), np.float32)
    for li, m in enumerate(_LEVELS):
        same = (t // (2 * m)) == (s // (2 * m))
        fwd = same & (t % (2 * m) >= m) & (s % (2 * m) < m)
        out[0, li] = fwd
        out[1, li] = fwd.T
    out[:, -1] = np.eye(CHUNK)
    return out


def _cumsum_rows(x, reverse):
    n = x.shape[0]
    r = lax.broadcasted_iota(jnp.int32, x.shape, 0)
    sh = 1
    while sh < n:
        if reverse:
            x = x + jnp.where(r < n - sh, pltpu.roll(x, n - sh, 0), 0.0)
        else:
            x = x + jnp.where(r >= sh, pltpu.roll(x, sh, 0), 0.0)
        sh *= 2
    return x


def _level_ref(cum, m, reverse):
    n = cum.shape[0]
    tgt = m if reverse else m - 1
    if 2 * m >= 8:
        parts = [jnp.broadcast_to(cum[g + tgt:g + tgt + 1, :], (2 * m, cum.shape[1]))
                 for g in range(0, n, 2 * m)]
        return parts[0] if len(parts) == 1 else jnp.concatenate(parts, axis=0)
    pos = lax.broadcasted_iota(jnp.int32, cum.shape, 0) % (2 * m)
    out = cum
    for p in range(2 * m):
        if p == tgt:
            continue
        shift = (p - tgt) % n
        out = jnp.where(pos == p, pltpu.roll(cum, shift, 0), out)
    return out


def _dot_nt(a, b):
    return lax.dot_general(a, b, (((1,), (1,)), ((), ())), preferred_element_type=F32)


def _dot_tn(a, b):
    return lax.dot_general(a, b, (((0,), (0,)), ((), ())), preferred_element_type=F32)


def _tri3():
    t = np.arange(CHUNK)[:, None]
    s = np.arange(CHUNK)[None, :]
    return np.stack([np.tile(s <= t, (1, 3)), np.tile(s >= t, (1, 3))]).astype(np.float32)


def _cumsum_split(x, tri):
    hi = x.astype(BF16)
    r1 = x - hi.astype(F32)
    mid = r1.astype(BF16)
    lo = (r1 - mid.astype(F32)).astype(BF16)
    return jnp.dot(tri, jnp.concatenate([hi, mid, lo], axis=0), preferred_element_type=F32)


def _gated_prepare(chains, masks, tri3):
    n = range(len(chains))
    qs, ks, vs, lfs, revs = zip(*chains)
    cum = [_cumsum_split(lfs[i], tri3[revs[i]]) for i in n]
    qb = [qs[i].astype(BF16) for i in n]
    kb = [ks[i].astype(BF16) for i in n]
    vb = [vs[i].astype(BF16) for i in n]
    scores = [masks[revs[i]][len(_LEVELS)] * _dot_nt(qb[i], kb[i]) for i in n]
    for li, m in enumerate(_LEVELS):
        w = [jnp.exp(-jnp.abs(cum[i] - _level_ref(cum[i], m, revs[i]))).astype(BF16) for i in n]
        part = [_dot_nt(qb[i] * w[i], kb[i] * w[i]) for i in n]
        scores = [scores[i] + masks[revs[i]][li] * part[i] for i in n]
    intra = [jnp.dot(scores[i].astype(BF16), vb[i], preferred_element_type=F32) for i in n]
    q_in = [(qs[i] * jnp.exp(cum[i])).astype(BF16) for i in n]
    edge = [cum[i][0:1] if revs[i] else cum[i][CHUNK - 1:CHUNK] for i in n]
    update = [_dot_tn(vb[i], (ks[i] * jnp.exp(edge[i] - cum[i])).astype(BF16)) for i in n]
    carry = [jnp.exp(edge[i]) for i in n]
    return [(intra[i], q_in[i], update[i], carry[i]) for i in n]


SCAN_HEADS = 8


def _scan0_kernel(m_ref, tri_ref, qf_ref, kf_ref, vf_ref, lff_ref, qb_ref, kb_ref, vb_ref, lfb_ref,
                  of_ref, ob_ref, st_ref):
    @pl.when(pl.program_id(2) == 0)
    def _():
        st_ref[...] = jnp.zeros_like(st_ref)

    masks = [[m_ref[r, i] for i in range(len(_LEVELS) + 1)] for r in range(2)]
    tri3 = [tri_ref[0], tri_ref[1]]
    n_ch = TM // CHUNK
    fwd_rows = [slice(c * CHUNK, (c + 1) * CHUNK) for c in range(n_ch)]
    bwd_rows = [slice((n_ch - 1 - c) * CHUNK, (n_ch - c) * CHUNK) for c in range(n_ch)]
    lanes = [slice(hh * HEAD_W, (hh + 1) * HEAD_W) for hh in range(SCAN_HEADS)]
    chains = []
    for c in range(n_ch):
        rf, rb = fwd_rows[c], bwd_rows[c]
        for ln in lanes:
            chains.append((qf_ref[0, rf, ln], kf_ref[0, rf, ln], vf_ref[0, rf, ln], lff_ref[0, rf, ln], 0))
            chains.append((qb_ref[0, rb, ln], kb_ref[0, rb, ln], vb_ref[0, rb, ln], lfb_ref[0, rb, ln], 1))
    prep = _gated_prepare(chains, masks, tri3)
    per = 2 * SCAN_HEADS
    states = [st_ref[i] for i in range(per)]
    for c in range(n_ch):
        for hh, ln in enumerate(lanes):
            for r, (o_ref, rows) in enumerate(((of_ref, fwd_rows[c]), (ob_ref, bwd_rows[c]))):
                s = 2 * hh + r
                intra, q_in, update, carry = prep[c * per + s]
                o_ref[0, rows, ln] = intra + _dot_nt(q_in, states[s].astype(BF16))
                states[s] = states[s] * carry + update
    for i in range(per):
        st_ref[i] = states[i]


def _bwd_block(i, n_blk):
    return jnp.where(i == 0, 0, n_blk - i)


def _scan0(masks, tri3, q, kf, kb, v, lff, lfb):
    bsz, l, wide = q.shape
    n_blk = l // TM
    bw = SCAN_HEADS * HEAD_W
    fwd = pl.BlockSpec((1, TM, bw), lambda b, h, i: (b, i, h))
    bwd = pl.BlockSpec((1, TM, bw), lambda b, h, i: (b, _bwd_block(i, n_blk), h))
    return pl.pallas_call(
        _scan0_kernel,
        grid=(bsz, wide // bw, n_blk),
        in_specs=[_const_spec(masks.shape), _const_spec(tri3.shape), fwd, fwd, fwd, fwd, bwd, bwd, bwd, bwd],
        out_specs=[fwd, bwd],
        out_shape=[jax.ShapeDtypeStruct((bsz, l, wide), F32)] * 2,
        scratch_shapes=[pltpu.VMEM((2 * SCAN_HEADS, HEAD_W, HEAD_W), F32)],
        compiler_params=_cp(("parallel", "parallel", "arbitrary")),
        name="scan0",
    )(masks, tri3, q, kf, v, lff, q, kb, v, lfb)


def _head_rms(o, w):
    return o * lax.rsqrt(jnp.mean(o * o, axis=-1, keepdims=True) + EPS) * w


def _read0_kernel(x_ref, of_ref, ob_ref, og_ref, nwa_ref, nwb_ref, wout_ref, mod_ref, xo_ref, *, n_batch, d):
    b, j = pl.program_id(0), pl.program_id(1)
    (g1,) = _mod_rows(mod_ref, b, j, n_batch, d, 2, 1)
    o = of_ref[0] + ob_ref[0]
    parts = []
    for hh in range(A_HEADS + B_HEADS):
        nw = nwa_ref[...] if hh < A_HEADS else nwb_ref[...]
        parts.append(_head_rms(o[:, hh * HEAD_W:(hh + 1) * HEAD_W], nw))
    y = jnp.concatenate(parts, axis=-1) * _silu(og_ref[0])
    yo = jnp.dot(y.astype(BF16), wout_ref[...], preferred_element_type=F32)
    xo_ref[0] = x_ref[0] + g1 * yo


def _read0(x, of, ob, og, nwa, nwb, wout, mod, n_batch):
    bsz, l, d = x.shape
    wide = of.shape[-1]
    xb = pl.BlockSpec((1, TM, d), lambda b, j: (b, j, 0))
    wb = pl.BlockSpec((1, TM, wide), lambda b, j: (b, j, 0))
    return pl.pallas_call(
        functools.partial(_read0_kernel, n_batch=n_batch, d=d),
        grid=(bsz, l // TM),
        in_specs=[xb, wb, wb, wb, _const_spec(nwa.shape), _const_spec(nwb.shape),
                  _const_spec(wout.shape), _const_spec(mod.shape)],
        out_specs=xb,
        out_shape=jax.ShapeDtypeStruct(x.shape, F32),
        compiler_params=_cp(("parallel", "arbitrary")),
        name="read0",
    )(x, of, ob, og, nwa, nwb, wout, mod)


def _route_kernel(x_ref, mod_ref, nw_ref, rwt_ref, rb_ref, h_ref, idx_ref, gate_ref, *, n_batch, d, blk_off):
    b, j = pl.program_id(0), pl.program_id(1) + blk_off
    shift, scale = _mod_rows(mod_ref, b, j, n_batch, d, 3, 2)
    h = _norm_mod(x_ref[0], nw_ref[...], shift, scale)
    h_ref[0] = h.astype(h_ref.dtype)
    logits = lax.dot_general(rwt_ref[...], h, (((1,), (1,)), ((), ())), precision=lax.Precision.HIGHEST,
                             preferred_element_type=F32) + rb_ref[...]
    n_exp = logits.shape[0]
    rows = lax.broadcasted_iota(jnp.int32, logits.shape, 0)
    vals, idxs = [], []
    for _ in range(TOP_K):
        m = jnp.max(logits, axis=0, keepdims=True)
        i = jnp.min(jnp.where(logits == m, rows, n_exp), axis=0, keepdims=True)
        vals.append(m)
        idxs.append(i)
        logits = jnp.where(rows == i, -jnp.inf, logits)
    ex = [jnp.exp(v - vals[0]) for v in vals]
    tot = ex[0] + ex[1] + ex[2] + ex[3]
    idx_ref[0] = jnp.concatenate(idxs, axis=0)
    gate_ref[0] = jnp.concatenate([e / tot for e in ex], axis=0)


def _route(x, mod, nw, rwt, rb, n_batch, blk_off):
    bsz, l, d = x.shape
    n_blk = l // TM - blk_off
    ls = n_blk * TM
    n_exp = rwt.shape[0]
    return pl.pallas_call(
        functools.partial(_route_kernel, n_batch=n_batch, d=d, blk_off=blk_off),
        grid=(bsz, n_blk),
        in_specs=[pl.BlockSpec((1, TM, d), lambda b, j: (b, j + blk_off, 0)),
                  _const_spec(mod.shape), _const_spec(nw.shape), _const_spec(rwt.shape), _const_spec(rb.shape)],
        out_specs=[pl.BlockSpec((1, TM, d), lambda b, j: (b, j, 0)),
                   pl.BlockSpec((1, TOP_K, TM), lambda b, j: (b, 0, j)),
                   pl.BlockSpec((1, TOP_K, TM), lambda b, j: (b, 0, j))],
        out_shape=[jax.ShapeDtypeStruct((bsz, ls, d), BF16),
                   jax.ShapeDtypeStruct((bsz, TOP_K, ls), jnp.int32),
                   jax.ShapeDtypeStruct((bsz, TOP_K, ls), F32)],
        compiler_params=_cp(("parallel", "arbitrary")),
        name="route",
    )(x, mod, nw, rwt, rb)


def _expert_kernel(te_ref, nu_ref, hs_ref, wup_ref, bup_ref, wdn_ref, bdn_ref, y_ref, wup_bf, wdn_bf):
    i = pl.program_id(0)

    @pl.when(jnp.logical_or(i == 0, te_ref[i] != te_ref[jnp.maximum(i - 1, 0)]))
    def _():
        wup_bf[...] = wup_ref[0].astype(BF16)
        wdn_bf[...] = wdn_ref[0].astype(BF16)

    @pl.when(i < nu_ref[0])
    def _():
        h = hs_ref[...].astype(BF16)
        ff = wdn_ref.shape[1]
        y = jnp.zeros(y_ref.shape, F32) + bdn_ref[0]
        fc = min(FF_CHUNK, ff)
        for c in range(0, ff, fc):
            glu = jnp.dot(h, wup_bf[:, c:c + fc], preferred_element_type=F32) + bup_ref[0, :, c:c + fc]
            lin = (jnp.dot(h, wup_bf[:, ff + c:ff + c + fc], preferred_element_type=F32)
                   + bup_ref[0, :, ff + c:ff + c + fc])
            glu = jnp.minimum(glu, SWIGLU_LIMIT)
            lin = jnp.clip(lin, -SWIGLU_LIMIT, SWIGLU_LIMIT)
            act = glu * jax.nn.sigmoid(SWIGLU_ALPHA * glu) * (lin + 1.0)
            y = y + jnp.dot(act.astype(BF16), wdn_bf[c:c + fc, :], preferred_element_type=F32)
        y_ref[...] = y

    @pl.when(i >= nu_ref[0])
    def _():
        y_ref[...] = jnp.zeros_like(y_ref)


def _experts(tile_expert, n_used, hs, wup, bup, wdn, bdn):
    n_rows, d = hs.shape
    n_tiles = n_rows // EXPERT_TILE
    ff = wdn.shape[1]
    grid_spec = pltpu.PrefetchScalarGridSpec(
        num_scalar_prefetch=2,
        grid=(n_tiles,),
        in_specs=[pl.BlockSpec((EXPERT_TILE, d), lambda i, te, nu: (i, 0)),
                  pl.BlockSpec((1, d, 2 * ff), lambda i, te, nu: (te[i], 0, 0)),
                  pl.BlockSpec((1, 1, 2 * ff), lambda i, te, nu: (te[i], 0, 0)),
                  pl.BlockSpec((1, ff, d), lambda i, te, nu: (te[i], 0, 0)),
                  pl.BlockSpec((1, 1, d), lambda i, te, nu: (te[i], 0, 0))],
        out_specs=pl.BlockSpec((EXPERT_TILE, d), lambda i, te, nu: (i, 0)),
        scratch_shapes=[pltpu.VMEM((d, 2 * ff), BF16), pltpu.VMEM((ff, d), BF16)],
    )
    return pl.pallas_call(
        _expert_kernel,
        grid_spec=grid_spec,
        out_shape=jax.ShapeDtypeStruct((n_rows, d), F32),
        compiler_params=_cp(("arbitrary",), vmem=EXPERT_VMEM_LIMIT),
        name="experts",
    )(tile_expert, n_used, hs, wup, bup, wdn, bdn)


GRAN = 8
LOCAL_ROWS = 1280


def _block_plan(idx, n_exp):
    bsz, top_k, ls = idx.shape
    n_blk = ls // TM
    nb = bsz * n_blk
    pairs = top_k * TM
    assert pairs + n_exp * (GRAN - 1) <= LOCAL_ROWS
    e = jnp.transpose(idx.reshape(bsz, top_k, n_blk, TM), (0, 2, 1, 3)).reshape(nb, pairs)
    onehot = e[:, :, None] == jnp.arange(n_exp, dtype=jnp.int32)[None, None, :]
    grouped = onehot.reshape(nb, top_k, TM, n_exp).astype(BF16)
    tri = jnp.tril(jnp.ones((TM, TM), BF16))
    local = jnp.einsum('ts,ngse->ngte', tri, grouped, preferred_element_type=F32)
    g_tot = local[:, :, -1, :]
    g_off = jnp.cumsum(g_tot, axis=1) - g_tot
    cs = (local + g_off[:, :, None, :]).reshape(nb, pairs, n_exp)
    rank = jnp.sum(jnp.where(onehot, cs, 0.0), axis=2).astype(jnp.int32) - 1
    cnt = (g_off[:, -1, :] + g_tot[:, -1, :]).astype(jnp.int32)
    cpad = ((cnt + GRAN - 1) // GRAN) * GRAN
    l_end = jnp.cumsum(cpad, axis=1)
    l_start = l_end - cpad
    ldst = jnp.sum(jnp.where(onehot, l_start[:, None, :], 0), axis=2) + rank
    tot = jnp.sum(cpad, axis=0)
    padded = ((tot + EXPERT_TILE - 1) // EXPERT_TILE) * EXPERT_TILE
    ends = jnp.cumsum(padded)
    starts = ends - padded
    g_start = starts[None, :] + jnp.cumsum(cpad, axis=0) - cpad
    n_gran = LOCAL_ROWS // GRAN
    row0 = jnp.arange(n_gran, dtype=jnp.int32) * GRAN
    gran_e = jnp.sum((row0[None, :, None] >= l_end[:, None, :]).astype(jnp.int32), axis=2)
    gran_oh = jnp.minimum(gran_e, n_exp - 1)[:, :, None] == jnp.arange(n_exp, dtype=jnp.int32)[None, None, :]
    gdst = jnp.sum(jnp.where(gran_oh, (g_start - l_start)[:, None, :], 0), axis=2) + row0[None, :]
    used = (l_end[:, -1] // GRAN).astype(jnp.int32)
    gdst = jnp.where(row0[None, :] < l_end[:, -1:], gdst, 0).astype(jnp.int32)
    n_rows = -(-(nb * pairs + nb * n_exp * (GRAN - 1) + n_exp * EXPERT_TILE) // EXPERT_TILE) * EXPERT_TILE
    n_tiles = n_rows // EXPERT_TILE
    n_used = (ends[-1] // EXPERT_TILE).astype(jnp.int32)
    tile_start = jnp.minimum(jnp.arange(n_tiles, dtype=jnp.int32), n_used - 1) * EXPERT_TILE
    tile_expert = jnp.minimum(jnp.sum((tile_start[:, None] >= ends[None, :]).astype(jnp.int32), axis=1), n_exp - 1)
    ldst = ldst.astype(jnp.int32).reshape(bsz, n_blk, top_k, TM)
    return dict(ldst_rows=jnp.transpose(ldst, (0, 2, 1, 3)).reshape(bsz, top_k, ls), ldst_cols=jnp.transpose(ldst, (0, 1, 3, 2)).reshape(bsz, ls, top_k),
                gdst=gdst.reshape(nb, 1, n_gran), used=used, ends=ends.astype(jnp.int32), padded=padded.astype(jnp.int32),
                tile_expert=tile_expert, n_used=n_used.reshape(1), n_rows=n_rows)


def _gran_copy(src, dst, sem):
    return pltpu.make_async_copy(src, dst, sem)


def _dispatch_blocks_kernel(ends_ref, padded_ref, used_ref, gdst_ref, ldst_ref, h_ref, hs_ref, buf_ref, zero_ref, sem):
    n_blk = pl.num_programs(1)
    blk = pl.program_id(0) * n_blk + pl.program_id(1)

    @pl.when(blk == 0)
    def _():
        zero_ref[...] = jnp.zeros_like(zero_ref)
        for e in range(ends_ref.shape[0]):
            @pl.when(padded_ref[e] > 0)
            def _():
                start = pl.multiple_of(ends_ref[e] - EXPERT_TILE, EXPERT_TILE)
                fill = _gran_copy(zero_ref, hs_ref.at[pl.ds(start, EXPERT_TILE), :], sem)
                fill.start()
                fill.wait()

    rows = lax.broadcasted_iota(jnp.int32, (LOCAL_ROWS, TM), 0)
    hit = rows == ldst_ref[0, 0:1, :]
    for k in range(1, TOP_K):
        hit = jnp.logical_or(hit, rows == ldst_ref[0, k:k + 1, :])
    perm = jnp.where(hit, 1.0, 0.0).astype(BF16)
    buf_ref[...] = jnp.dot(perm, h_ref[0].astype(BF16), preferred_element_type=F32)

    def copy(j):
        src = pl.multiple_of(j * GRAN, GRAN)
        dst = pl.multiple_of(gdst_ref[0, 0, j], GRAN)
        return _gran_copy(buf_ref.at[pl.ds(src, GRAN), :], hs_ref.at[pl.ds(dst, GRAN), :], sem)

    def issue(j, carry):
        copy(j).start()
        return carry

    def drain(j, carry):
        copy(j).wait()
        return carry

    lax.fori_loop(0, used_ref[blk], issue, 0)
    lax.fori_loop(0, used_ref[blk], drain, 0)


def _dispatch_blocks(plan, h):
    bsz, ls, d = h.shape
    n_blk = ls // TM
    n_gran = LOCAL_ROWS // GRAN
    grid_spec = pltpu.PrefetchScalarGridSpec(
        num_scalar_prefetch=3,
        grid=(bsz, n_blk),
        in_specs=[pl.BlockSpec((1, 1, n_gran), lambda b, j, *_: (b * n_blk + j, 0, 0), memory_space=pltpu.SMEM),
                  pl.BlockSpec((1, TOP_K, TM), lambda b, j, *_: (b, 0, j)),
                  pl.BlockSpec((1, TM, d), lambda b, j, *_: (b, j, 0))],
        out_specs=pl.BlockSpec(memory_space=pl.ANY),
        scratch_shapes=[pltpu.VMEM((LOCAL_ROWS, d), F32), pltpu.VMEM((EXPERT_TILE, d), F32),
                        pltpu.SemaphoreType.DMA(())],
    )
    return pl.pallas_call(
        _dispatch_blocks_kernel,
        grid_spec=grid_spec,
        out_shape=jax.ShapeDtypeStruct((plan["n_rows"], d), F32),
        compiler_params=_cp(("arbitrary", "arbitrary")),
        name="dispatch_blocks",
    )(plan["ends"], plan["padded"], plan["used"], plan["gdst"], plan["ldst_rows"], h)


def _combine_blocks_kernel(used_ref, gdst_ref, x_ref, ldst_ref, gate_ref, mod_ref, fw_ref, y_hbm, xo_ref, buf_ref, sem,
                           *, n_batch, d, blk_off, final):
    b, j = pl.program_id(0), pl.program_id(1) + blk_off
    blk = pl.program_id(0) * pl.num_programs(1) + pl.program_id(1)

    @pl.when(blk == 0)
    def _():
        buf_ref[...] = jnp.zeros_like(buf_ref)

    def copy(g):
        dst = pl.multiple_of(g * GRAN, GRAN)
        src = pl.multiple_of(gdst_ref[0, 0, g], GRAN)
        return _gran_copy(y_hbm.at[pl.ds(src, GRAN), :], buf_ref.at[pl.ds(dst, GRAN), :], sem)

    def issue(g, carry):
        copy(g).start()
        return carry

    def drain(g, carry):
        copy(g).wait()
        return carry

    lax.fori_loop(0, used_ref[blk], issue, 0)
    lane = lax.broadcasted_iota(jnp.int32, (TM, LOCAL_ROWS), 1)
    ldst, gate = ldst_ref[0], gate_ref[0]
    weight = jnp.where(lane == ldst[:, 0:1], gate[:, 0:1], 0.0)
    for k in range(1, TOP_K):
        weight = weight + jnp.where(lane == ldst[:, k:k + 1], gate[:, k:k + 1], 0.0)
    lax.fori_loop(0, used_ref[blk], drain, 0)
    acc = jnp.dot(weight.astype(BF16), buf_ref[...].astype(BF16), preferred_element_type=F32)
    (g2,) = _mod_rows(mod_ref, b, j, n_batch, d, 5, 1)
    xo = x_ref[0] + g2 * acc
    if final:
        xo = xo * lax.rsqrt(jnp.mean(xo * xo, axis=-1, keepdims=True) + EPS) * fw_ref[...]
    xo_ref[0] = xo


def _combine_blocks(plan, x, gate, mod, fw, y, n_batch, blk_off, final):
    bsz, l, d = x.shape
    n_blk = l // TM - blk_off
    ls = n_blk * TM
    n_gran = LOCAL_ROWS // GRAN
    grid_spec = pltpu.PrefetchScalarGridSpec(
        num_scalar_prefetch=1,
        grid=(bsz, n_blk),
        in_specs=[pl.BlockSpec((1, 1, n_gran), lambda b, j, *_: (b * n_blk + j, 0, 0), memory_space=pltpu.SMEM),
                  pl.BlockSpec((1, TM, d), lambda b, j, *_: (b, j + blk_off, 0)),
                  pl.BlockSpec((1, TM, TOP_K), lambda b, j, *_: (b, j, 0)),
                  pl.BlockSpec((1, TM, TOP_K), lambda b, j, *_: (b, j, 0)),
                  _const_spec(mod.shape), _const_spec(fw.shape),
                  pl.BlockSpec(memory_space=pl.ANY)],
        out_specs=pl.BlockSpec((1, TM, d), lambda b, j, *_: (b, j, 0)),
        scratch_shapes=[pltpu.VMEM((LOCAL_ROWS, d), F32), pltpu.SemaphoreType.DMA(())],
    )
    return pl.pallas_call(
        functools.partial(_combine_blocks_kernel, n_batch=n_batch, d=d, blk_off=blk_off, final=final),
        grid_spec=grid_spec,
        out_shape=jax.ShapeDtypeStruct((bsz, ls, d), F32),
        compiler_params=_cp(("arbitrary", "arbitrary")),
        name="combine_blocks",
    )(plan["used"], plan["gdst"], x, plan["ldst_cols"], gate, mod, fw, y)


def _moe_blocks(x, mod, nw, rw, rb, wup, bup, wdn, bdn, fw, n_batch, blk_off, final):
    n_exp = rw.shape[1]
    d = x.shape[-1]
    h, idx, gate = _route(x, mod, nw, rw.T, rb.reshape(n_exp, 1), n_batch, blk_off)
    plan = _block_plan(idx, n_exp)
    hs = _dispatch_blocks(plan, h)
    y = _experts(plan["tile_expert"], plan["n_used"], hs, wup, bup.reshape(n_exp, 1, -1), wdn, bdn.reshape(n_exp, 1, d))
    return _combine_blocks(plan, x, jnp.transpose(gate, (0, 2, 1)), mod, fw, y, n_batch, blk_off, final)


def _pad_heads(w, n_heads, width):
    lead = w.shape[:-1]
    dh = w.shape[-1] // n_heads
    w = w.reshape(lead + (n_heads, dh))
    w = jnp.pad(w, [(0, 0)] * len(lead) + [(0, 0), (0, width - dh)])
    return w.reshape(lead + (n_heads * width,))


def _layer0_weights(w_in, w2_f, b2_f, w2_b, b2_b):
    d = w_in.shape[0]
    sizes = (A_HEADS * A_DK, A_HEADS * A_DK, A_HEADS * A_DV, A_RANK, A_RANK, A_HEADS * A_DV,
             B_HEADS * B_DK, B_HEADS * B_DK, B_HEADS * B_DK, B_HEADS * B_DV, B_HEADS * B_DV)
    aq, ak, av, ar_f, ar_b, aog, bq, bz_f, bz_b, bi, bog = jnp.split(w_in, np.cumsum(sizes)[:-1].tolist(), axis=1)
    ar = jnp.concatenate([ar_f, ar_b, jnp.zeros((d, HEAD_W - 2 * A_RANK), w_in.dtype)], axis=1)
    w = jnp.concatenate([_pad_heads(aq, A_HEADS, HEAD_W), bq, av, bi, aog, bog,
                         _pad_heads(ak, A_HEADS, HEAD_W), bz_f, bz_b, ar], axis=1).astype(BF16)
    hw = A_HEADS * HEAD_W
    w2 = jnp.zeros((2, HEAD_W, hw), F32)
    w2 = w2.at[0, 0:A_RANK].set(_pad_heads(w2_f, A_HEADS, HEAD_W))
    w2 = w2.at[1, A_RANK:2 * A_RANK].set(_pad_heads(w2_b, A_HEADS, HEAD_W))
    b2 = jnp.stack([_pad_heads(b2_f, A_HEADS, HEAD_W), _pad_heads(b2_b, A_HEADS, HEAD_W)]).reshape(2, 1, hw)
    return w, w2.astype(BF16), b2


QKV_W = 2 * C_HEADS * C_DK + C_HEADS * C_DV
OG_W = C_HEADS * C_DV
DQ_W = D_HEADS * D_HD
DKV_W = D_KV_HEADS * HEAD_W


def _proj1_kernel(x_ref, mod_ref, nw_ref, w_ref, cos_ref, sin_ref,
                  qkv_ref, gb_ref, og_ref, dq_ref, dk_ref, dv_ref, *, n_batch, d):
    b, j = pl.program_id(0), pl.program_id(1)
    shift, scale = _mod_rows(mod_ref, b, j, n_batch, d, 0, 2)
    h = _norm_mod(x_ref[0], nw_ref[...], shift, scale).astype(BF16)

    def mm(c0, c1):
        return jnp.dot(h, w_ref[:, c0:c1], preferred_element_type=F32)

    c = 0
    qkv_ref[0] = mm(c, c + QKV_W); c += QKV_W
    gb_ref[0] = mm(c, c + HEAD_W); c += HEAD_W
    og_ref[0] = mm(c, c + OG_W); c += OG_W
    cos, sin = cos_ref[...], sin_ref[...]
    for s in range(DQ_W // HEAD_W):
        xs = mm(c + s * HEAD_W, c + (s + 1) * HEAD_W)
        xp = mm(c + DQ_W + s * HEAD_W, c + DQ_W + (s + 1) * HEAD_W)
        dq_ref[0, :, s * HEAD_W:(s + 1) * HEAD_W] = (xs * cos + xp * sin) * (D_HD ** -0.5)
    c += 2 * DQ_W
    for s in range(DKV_W // HEAD_W):
        xs = mm(c + s * HEAD_W, c + (s + 1) * HEAD_W)
        xp = mm(c + DKV_W + s * HEAD_W, c + DKV_W + (s + 1) * HEAD_W)
        dk_ref[0, :, s * HEAD_W:(s + 1) * HEAD_W] = xs * cos + xp * sin
    c += 2 * DKV_W
    dv_ref[0] = mm(c, c + DKV_W)


def _proj1(x, mod, nw, w, cos, sin, n_batch):
    bsz, l, d = x.shape
    widths = (QKV_W, HEAD_W, OG_W, DQ_W, DKV_W, DKV_W)
    return pl.pallas_call(
        functools.partial(_proj1_kernel, n_batch=n_batch, d=d),
        grid=(bsz, l // TM),
        in_specs=[pl.BlockSpec((1, TM, d), lambda b, j: (b, j, 0)),
                  _const_spec(mod.shape), _const_spec(nw.shape), _const_spec(w.shape),
                  pl.BlockSpec((TM, HEAD_W), lambda b, j: (j, 0)),
                  pl.BlockSpec((TM, HEAD_W), lambda b, j: (j, 0))],
        out_specs=[pl.BlockSpec((1, TM, wd), lambda b, j: (b, j, 0)) for wd in widths],
        out_shape=[jax.ShapeDtypeStruct((bsz, l, wd), F32) for wd in widths],
        compiler_params=_cp(("parallel", "arbitrary")),
        name="proj1",
    )(x, mod, nw, w, cos, sin)


HALO = 8


def _conv1_kernel(cur_ref, prev_ref, next_ref, cw_ref, gbr_ref, alog_ref, dtb_ref,
                  q_ref, k_ref, v_ref, gb_ref, xe_ref):
    j, n_blk = pl.program_id(1), pl.num_programs(1)
    prev_ok = j >= 2
    next_ok = jnp.logical_and(j >= 1, j < n_blk - 1)
    xe_ref[0:HALO, :] = jnp.where(prev_ok, prev_ref[0], 0.0)
    xe_ref[HALO:HALO + TM, :] = cur_ref[0]
    xe_ref[HALO + TM:2 * HALO + TM, :] = jnp.where(next_ok, next_ref[0], 0.0)
    pad = CONV_W // 2
    hw = C_HEADS * HEAD_W
    for g, o_ref in enumerate((q_ref, k_ref, v_ref)):
        cols = slice(g * hw, (g + 1) * hw)
        acc = cw_ref[0:1, cols] * xe_ref[HALO - pad:HALO - pad + TM, cols]
        for i in range(1, CONV_W):
            acc = acc + cw_ref[i:i + 1, cols] * xe_ref[HALO - pad + i:HALO - pad + i + TM, cols]
        y = _silu(acc)
        if g < 2:
            parts = []
            for hh in range(C_HEADS):
                yh = y[:, hh * HEAD_W:(hh + 1) * HEAD_W]
                yh = yh * lax.rsqrt(jnp.sum(yh * yh, axis=-1, keepdims=True) + EPS)
                parts.append(yh * (C_DK ** -0.5) if g == 0 else yh)
            y = jnp.concatenate(parts, axis=-1)
        o_ref[0] = y
    raw = gbr_ref[0]
    z = raw + dtb_ref[...]
    softplus = jnp.maximum(z, 0.0) + _log1p_exp_neg_abs(z)
    lane = lax.broadcasted_iota(jnp.int32, raw.shape, 1)
    gb_ref[0] = jnp.where(lane < 2 * C_HEADS, jax.nn.sigmoid(raw), -jnp.exp(alog_ref[...]) * softplus)


def _conv1(qkv, conv_w, gb_raw, alog, dtb):
    bsz, l, wd = qkv.shape
    n_blk = l // TM
    per = TM // HALO
    last = l // HALO - 1
    hw = C_HEADS * HEAD_W
    ob = pl.BlockSpec((1, TM, hw), lambda b, j: (b, j, 0))
    gbs = pl.BlockSpec((1, TM, HEAD_W), lambda b, j: (b, j, 0))
    return pl.pallas_call(
        _conv1_kernel,
        grid=(bsz, n_blk),
        in_specs=[pl.BlockSpec((1, TM, wd), lambda b, j: (b, j, 0)),
                  pl.BlockSpec((1, HALO, wd), lambda b, j: (b, jnp.maximum(j * per - 1, 0), 0)),
                  pl.BlockSpec((1, HALO, wd), lambda b, j: (b, jnp.minimum((j + 1) * per, last), 0)),
                  _const_spec(conv_w.shape), gbs, _const_spec(alog.shape), _const_spec(dtb.shape)],
        out_specs=[ob, ob, ob, gbs],
        out_shape=[jax.ShapeDtypeStruct((bsz, l, hw), F32)] * 3 + [jax.ShapeDtypeStruct((bsz, l, HEAD_W), F32)],
        scratch_shapes=[pltpu.VMEM((TM + 2 * HALO, wd), F32)],
        compiler_params=_cp(("parallel", "arbitrary")),
        name="conv1",
    )(qkv, qkv, qkv, conv_w, gb_raw, alog, dtb)


_MERGE = (16, 32, 64)
_BASE = 8


def _delta_masks():
    t = np.arange(CHUNK)[:, None]
    s = np.arange(CHUNK)[None, :]
    out = [s <= t, s < t, s >= t, s > t, (t // _BASE) == (s // _BASE)]
    for m2 in _MERGE:
        out.append(((t // m2) == (s // m2)) & ((t // (m2 // 2)) != (s // (m2 // 2))))
    return np.stack(out).astype(np.float32)


def _bdot(a, b):
    return jnp.dot(a.astype(BF16), b.astype(BF16), preferred_element_type=F32)


DELTA_HEADS = 4


def _delta_prepare(chains, dm):
    n = range(len(chains))
    qs, ks, vs, gs, betas, revs = zip(*chains)
    incl = [dm[2] if r else dm[0] for r in revs]
    strict = [dm[3] if r else dm[1] for r in revs]
    cum = [_cumsum_rows(jnp.broadcast_to(gs[i], qs[i].shape), revs[i]) for i in n]
    decay = [incl[i] * jnp.exp(jnp.where(incl[i] > 0, cum[i][:, 0:CHUNK] - cum[i].T[0:CHUNK, :], 0.0)) for i in n]
    kbeta = [ks[i] * betas[i] for i in n]
    kbf = [ks[i].astype(BF16) for i in n]
    a = [strict[i] * _dot_nt(kbeta[i].astype(BF16), kbf[i]) * decay[i] for i in n]
    eye = dm[0] * dm[2]
    n0 = [-(a[i] * dm[4]) for i in n]
    n2 = [_bdot(n0[i], n0[i]) for i in n]
    n4 = [_bdot(n2[i], n2[i]) for i in n]
    t = [eye + n0[i] for i in n]
    t = [t[i] + _bdot(t[i], n2[i]) for i in n]
    t = [t[i] + _bdot(t[i], n4[i]) for i in n]
    for li in range(len(_MERGE)):
        inner = [_bdot(a[i] * dm[5 + li], t[i]) for i in n]
        t = [t[i] - _bdot(t[i], inner[i]) for i in n]
    uw = [_bdot(t[i], jnp.concatenate([vs[i] * betas[i], kbeta[i] * jnp.exp(cum[i])], axis=1)) for i in n]
    scores = [(_dot_nt(qs[i].astype(BF16), kbf[i]) * decay[i]).astype(BF16) for i in n]
    edge = [cum[i][0:1] if revs[i] else cum[i][CHUNK - 1:CHUNK] for i in n]
    q_in = [(qs[i] * jnp.exp(cum[i])).astype(BF16) for i in n]
    k_out = [(ks[i] * jnp.exp(edge[i] - cum[i])).astype(BF16) for i in n]
    carry = [jnp.exp(edge[i]) for i in n]
    return [(uw[i][:, 0:HEAD_W], uw[i][:, HEAD_W:].astype(BF16), scores[i], q_in[i], k_out[i], carry[i]) for i in n]


def _delta_step(prep, states):
    n = range(len(prep))
    sb = [states[i].astype(BF16) for i in n]
    v_new = [prep[i][0] - jnp.dot(prep[i][1], sb[i], preferred_element_type=F32) for i in n]
    vb = [v_new[i].astype(BF16) for i in n]
    o = [jnp.dot(prep[i][3], sb[i], preferred_element_type=F32)
         + jnp.dot(prep[i][2], vb[i], preferred_element_type=F32) for i in n]
    new = [states[i] * prep[i][5] + _dot_tn(prep[i][4], vb[i]) for i in n]
    return o, new


def _lane_col(x, lane_idx):
    lane = lax.broadcasted_iota(jnp.int32, x.shape, 1)
    return jnp.sum(jnp.where(lane == lane_idx, x, 0.0), axis=-1, keepdims=True)


def _delta_kernel(dm_ref, qf_ref, kf_ref, vf_ref, gbf_ref, qb_ref, kb_ref, vb_ref, gbb_ref,
                  of_ref, ob_ref, st_ref):
    @pl.when(pl.program_id(2) == 0)
    def _():
        st_ref[...] = jnp.zeros_like(st_ref)

    dm = [dm_ref[i] for i in range(dm_ref.shape[0])]
    n_ch = TM // CHUNK
    gbf, gbb = gbf_ref[0], gbb_ref[0]
    chains = []
    cols = []
    for hh in range(DELTA_HEADS):
        h = pl.program_id(1) * DELTA_HEADS + hh
        cols.append((_lane_col(gbf, h), _lane_col(gbf, 2 * C_HEADS + h),
                     _lane_col(gbb, C_HEADS + h), _lane_col(gbb, 3 * C_HEADS + h)))
    for c in range(n_ch):
        rf = slice(c * CHUNK, (c + 1) * CHUNK)
        rb = slice((n_ch - 1 - c) * CHUNK, (n_ch - c) * CHUNK)
        for hh in range(DELTA_HEADS):
            lanes = slice(hh * HEAD_W, (hh + 1) * HEAD_W)
            beta_f, g_f, beta_b, g_b = cols[hh]
            chains.append((qf_ref[0, rf, lanes], kf_ref[0, rf, lanes], vf_ref[0, rf, lanes], g_f[rf], beta_f[rf], False))
            chains.append((qb_ref[0, rb, lanes], kb_ref[0, rb, lanes], vb_ref[0, rb, lanes], g_b[rb], beta_b[rb], True))
    prep = _delta_prepare(chains, dm)
    per = 2 * DELTA_HEADS
    states = [st_ref[i] for i in range(per)]
    for c in range(n_ch):
        rf = slice(c * CHUNK, (c + 1) * CHUNK)
        rb = slice((n_ch - 1 - c) * CHUNK, (n_ch - c) * CHUNK)
        outs, states = _delta_step(prep[c * per:(c + 1) * per], states)
        for hh in range(DELTA_HEADS):
            lanes = slice(hh * HEAD_W, (hh + 1) * HEAD_W)
            of_ref[0, rf, lanes] = outs[2 * hh]
            ob_ref[0, rb, lanes] = outs[2 * hh + 1]
    for i in range(per):
        st_ref[i] = states[i]


def _delta(dmasks, q, k, v, gb):
    bsz, l, wide = q.shape
    n_blk = l // TM
    bw = DELTA_HEADS * HEAD_W
    fwd = pl.BlockSpec((1, TM, bw), lambda b, h, i: (b, i, h))
    bwd = pl.BlockSpec((1, TM, bw), lambda b, h, i: (b, _bwd_block(i, n_blk), h))
    gf = pl.BlockSpec((1, TM, HEAD_W), lambda b, h, i: (b, i, 0))
    gbw = pl.BlockSpec((1, TM, HEAD_W), lambda b, h, i: (b, _bwd_block(i, n_blk), 0))
    return pl.pallas_call(
        _delta_kernel,
        grid=(bsz, wide // bw, n_blk),
        in_specs=[_const_spec(dmasks.shape), fwd, fwd, fwd, gf, bwd, bwd, bwd, gbw],
        out_specs=[fwd, bwd],
        out_shape=[jax.ShapeDtypeStruct((bsz, l, wide), F32)] * 2,
        scratch_shapes=[pltpu.VMEM((2 * DELTA_HEADS, HEAD_W, HEAD_W), F32)],
        compiler_params=_cp(("parallel", "parallel", "arbitrary")),
        name="delta",
    )(dmasks, q, k, v, gb, q, k, v, gb)


def _attn_kernel(q_ref, kp_ref, kc_ref, kn_ref, vp_ref, vc_ref, vn_ref, kx_ref, vx_ref, sink_ref, o_ref, *, t_len):
    i = pl.program_id(1)
    q = q_ref[0]
    kl = jnp.concatenate([kp_ref[0], kc_ref[0], kn_ref[0]], axis=0).astype(BF16)
    vl = jnp.concatenate([vp_ref[0], vc_ref[0], vn_ref[0]], axis=0).astype(BF16)
    kx, vx = kx_ref[0].astype(BF16), vx_ref[0].astype(BF16)
    qpos = lax.broadcasted_iota(jnp.int32, (WINDOW, 3 * WINDOW), 0)
    kpos = lax.broadcasted_iota(jnp.int32, (WINDOW, 3 * WINDOW), 1) - WINDOW
    k_abs = i * WINDOW + kpos
    valid = (jnp.abs(kpos - qpos) <= WINDOW) & (k_abs >= 0) & (k_abs < t_len)
    low = lax.broadcasted_iota(jnp.int32, (WINDOW, HEAD_W), 1) < D_HD
    group = D_HEADS // D_KV_HEADS
    heads = range(D_HEADS)
    cols = [slice((h // group) * HEAD_W, (h // group + 1) * HEAD_W) for h in heads]
    qm = [jnp.where(low if h % 2 == 0 else jnp.logical_not(low), q[:, (h // 2) * HEAD_W:(h // 2 + 1) * HEAD_W], 0.0)
          .astype(BF16) for h in heads]
    s_l = [jnp.where(valid, _dot_nt(qm[h], kl[:, cols[h]]), -jnp.inf) for h in heads]
    s_x = [_dot_nt(qm[h], kx[:, cols[h]]) for h in heads]
    sink = [sink_ref[:, h:h + 1] for h in heads]
    def fold(a, b, op):
        slabs = [a[:, c:c + HEAD_W] for c in range(0, a.shape[1], HEAD_W)]
        slabs += [b[:, c:c + HEAD_W] for c in range(0, b.shape[1], HEAD_W)]
        acc = slabs[0]
        for s in slabs[1:]:
            acc = op(acc, s)
        return acc

    m = [jnp.maximum(jnp.max(fold(s_l[h], s_x[h], jnp.maximum), axis=-1, keepdims=True), sink[h]) for h in heads]
    p_l = [jnp.exp(s_l[h] - m[h]) for h in heads]
    p_x = [jnp.exp(s_x[h] - m[h]) for h in heads]
    den = [jnp.sum(fold(p_l[h], p_x[h], jnp.add), axis=-1, keepdims=True) + jnp.exp(sink[h] - m[h]) for h in heads]
    o = [(jnp.dot(p_l[h].astype(BF16), vl[:, cols[h]], preferred_element_type=F32)
          + jnp.dot(p_x[h].astype(BF16), vx[:, cols[h]], preferred_element_type=F32)) / den[h] for h in heads]
    for p in range(D_HEADS // 2):
        o_ref[0, :, p * HEAD_W:(p + 1) * HEAD_W] = jnp.where(low, o[2 * p], o[2 * p + 1])


def _attn(dq, dk, dv, sinks, n_ctx):
    bsz, l, _ = dq.shape
    t_len = l - n_ctx
    nq = t_len // WINDOW
    off = n_ctx // WINDOW

    def kv(delta):
        return pl.BlockSpec((1, WINDOW, DKV_W), lambda b, i: (b, off + jnp.clip(i + delta, 0, nq - 1), 0))

    ctx = pl.BlockSpec((1, n_ctx, DKV_W), lambda b, i: (b, 0, 0))
    return pl.pallas_call(
        functools.partial(_attn_kernel, t_len=t_len),
        grid=(bsz, nq),
        in_specs=[pl.BlockSpec((1, WINDOW, DQ_W), lambda b, i: (b, off + i, 0)),
                  kv(-1), kv(0), kv(1), kv(-1), kv(0), kv(1), ctx, ctx, _const_spec(sinks.shape)],
        out_specs=pl.BlockSpec((1, WINDOW, DQ_W), lambda b, i: (b, i, 0)),
        out_shape=jax.ShapeDtypeStruct((bsz, t_len, DQ_W), F32),
        compiler_params=_cp(("parallel", "arbitrary")),
        name="attn",
    )(dq, dk, dk, dk, dv, dv, dv, dk, dv, sinks)


def _read1_kernel(x_ref, of_ref, ob_ref, og_ref, od_ref, nw_ref, wout_ref, mod_ref, xo_ref, *, n_batch, d):
    b, j = pl.program_id(0), pl.program_id(1) + 1
    (g1,) = _mod_rows(mod_ref, b, j, n_batch, d, 2, 1)
    o = of_ref[0] + ob_ref[0]
    parts = [_head_rms(o[:, hh * HEAD_W:(hh + 1) * HEAD_W], nw_ref[...]) for hh in range(C_HEADS)]
    y = jnp.concatenate(parts, axis=-1) * _silu(og_ref[0])
    y = jnp.concatenate([y, od_ref[0]], axis=-1)
    xo_ref[0] = x_ref[0] + g1 * jnp.dot(y.astype(BF16), wout_ref[...], preferred_element_type=F32)


def _read1(x, of, ob, og, od, nw, wout, mod, n_batch):
    bsz, l, d = x.shape
    lat = lambda wd: pl.BlockSpec((1, TM, wd), lambda b, j: (b, j + 1, 0))
    return pl.pallas_call(
        functools.partial(_read1_kernel, n_batch=n_batch, d=d),
        grid=(bsz, l // TM - 1),
        in_specs=[lat(d), lat(OG_W), lat(OG_W), lat(OG_W),
                  pl.BlockSpec((1, TM, DQ_W), lambda b, j: (b, j, 0)),
                  _const_spec(nw.shape), _const_spec(wout.shape), _const_spec(mod.shape)],
        out_specs=lat(d),
        out_shape=jax.ShapeDtypeStruct(x.shape, F32),
        input_output_aliases={0: 0},
        compiler_params=_cp(("parallel", "arbitrary")),
        name="read1",
    )(x, of, ob, og, od, nw, wout, mod)


def _swap_halves(w, n_heads):
    lead = w.shape[:-1]
    dh = w.shape[-1] // n_heads
    w = w.reshape(lead + (n_heads, 2, dh // 2))
    return w[..., ::-1, :].reshape(lead + (n_heads * dh,))


def _dup_heads(w, n_heads):
    lead = w.shape[:-1]
    dh = w.shape[-1] // n_heads
    w = w.reshape(lead + (n_heads, 1, dh))
    return jnp.concatenate([w, w], axis=-2).reshape(lead + (2 * n_heads * dh,))


def _layer1_weights(w_in):
    d = w_in.shape[0]
    sizes = (C_HEADS * C_DK, C_HEADS * C_DK, C_HEADS * C_DV, C_HEADS, C_HEADS, C_HEADS, C_HEADS,
             C_HEADS * C_DV, D_HEADS * D_HD, D_KV_HEADS * D_HD, D_KV_HEADS * D_HD)
    cq, ck, cv, bt_f, bt_b, a_f, a_b, og, dq, dk, dv = jnp.split(w_in, np.cumsum(sizes)[:-1].tolist(), axis=1)
    gates = jnp.concatenate([bt_f, bt_b, a_f, a_b, jnp.zeros((d, HEAD_W - 4 * C_HEADS), w_in.dtype)], axis=1)
    return jnp.concatenate([cq, ck, cv, gates, og, dq, _swap_halves(dq, D_HEADS),
                            _dup_heads(dk, D_KV_HEADS), _dup_heads(_swap_halves(dk, D_KV_HEADS), D_KV_HEADS),
                            _dup_heads(dv, D_KV_HEADS)], axis=1).astype(BF16)


def _rope_tables(n_ctx, t_len):
    rows = t_len // GRID_W
    row = jnp.repeat(jnp.arange(rows, dtype=F32), GRID_W)
    col = jnp.tile(jnp.arange(GRID_W, dtype=F32), rows)
    n_freq = D_HD // 4
    inv = ROPE_BASE ** (-jnp.arange(n_freq, dtype=F32) / n_freq)
    ang = jnp.concatenate([row[:, None] * inv, col[:, None] * inv], axis=-1)
    cos, sin = jnp.cos(ang), jnp.sin(ang)
    cos = jnp.concatenate([jnp.ones((n_ctx, D_HD // 2), F32), cos], axis=0)
    sin = jnp.concatenate([jnp.zeros((n_ctx, D_HD // 2), F32), sin], axis=0)
    return jnp.concatenate([cos] * 4, axis=1), jnp.concatenate([-sin, sin] * 2, axis=1)


def _mixer1(xx, mod, nw, w_in, w_out, conv_w, a_log_f, dt_bias_f, a_log_b, dt_bias_b, dn_norm_w, sinks, n_batch, n_ctx):
    l = xx.shape[1]
    cos, sin = _rope_tables(n_ctx, l - n_ctx)
    qkv, gb_raw, og, dq, dk, dv = _proj1(xx, mod, nw, _layer1_weights(w_in), cos, sin, n_batch)
    zero4 = jnp.zeros((C_HEADS,), F32)
    fill = jnp.zeros((HEAD_W - 4 * C_HEADS,), F32)
    alog = jnp.concatenate([zero4, zero4, a_log_f, a_log_b, fill]).reshape(1, HEAD_W)
    dtb = jnp.concatenate([zero4, zero4, dt_bias_f, dt_bias_b, fill]).reshape(1, HEAD_W)
    cq, ck, cv, gb = _conv1(qkv, conv_w, gb_raw, alog, dtb)
    of, ob = _delta(jnp.asarray(_delta_masks()), cq, ck, cv, gb)
    sink_row = jnp.concatenate([sinks, jnp.zeros((HEAD_W - D_HEADS,), F32)]).reshape(1, HEAD_W)
    od = _attn(dq, dk, dv, sink_row, n_ctx)
    return _read1(xx, of, ob, og, od, dn_norm_w.reshape(1, -1), w_out.astype(BF16), mod, n_batch)


def kernel(x, c, ctx, c_ctx, l0_ada_w, l0_ada_b, l0_norm_mix_w, l0_w_in, l0_w_out, l0_gla_w2_f, l0_gla_b_f, l0_gla_w2_b, l0_gla_b_b, l0_gla_norm_w, l0_hgrn_norm_w, hgrn_lb_logits, l0_norm_ffn_w, l0_router_w, l0_router_b, l0_w_up, l0_b_up, l0_w_down, l0_b_down, l1_ada_w, l1_ada_b, l1_norm_mix_w, l1_w_in, l1_w_out, l1_conv_w, l1_a_log_f, l1_dt_bias_f, l1_a_log_b, l1_dt_bias_b, l1_dn_norm_w, l1_sinks, l1_norm_ffn_w, l1_router_w, l1_router_b, l1_w_up, l1_b_up, l1_w_down, l1_b_down, final_norm_w):
    bsz, t, d = x.shape
    n_ctx = ctx.shape[1]
    assert n_ctx == TM and t % TM == 0
    xx = jnp.concatenate([ctx, x], axis=1)
    mod_rows = -(-(bsz + 1) // 8) * 8
    c_all = jnp.zeros((mod_rows, d), F32).at[:bsz].set(c).at[bsz].set(c_ctx)
    ones = jnp.ones((1, d), F32)

    mod0 = _ada_table(c_all, l0_ada_w, l0_ada_b)
    w0, w2, b2 = _layer0_weights(l0_w_in, l0_gla_w2_f, l0_gla_b_f, l0_gla_w2_b, l0_gla_b_b)
    q, kf, kb, v, lff, lfb, og = _proj0(xx, mod0, l0_norm_mix_w.reshape(1, d), w0, w2, b2, hgrn_lb_logits, bsz)
    of, ob = _scan0(jnp.asarray(_level_masks()), jnp.asarray(_tri3(), BF16), q, kf, kb, v, lff, lfb)
    xx = _read0(xx, of, ob, og, l0_gla_norm_w.reshape(1, -1), l0_hgrn_norm_w.reshape(1, -1), l0_w_out.astype(BF16), mod0, bsz)
    xx = _moe_blocks(xx, mod0, l0_norm_ffn_w.reshape(1, d), l0_router_w, l0_router_b, l0_w_up, l0_b_up, l0_w_down, l0_b_down,
              ones, bsz, 0, False)

    mod1 = _ada_table(c_all, l1_ada_w, l1_ada_b)
    xx = _mixer1(xx, mod1, l1_norm_mix_w.reshape(1, d), l1_w_in, l1_w_out, l1_conv_w, l1_a_log_f, l1_dt_bias_f,
                 l1_a_log_b, l1_dt_bias_b, l1_dn_norm_w, l1_sinks, bsz, n_ctx)
    return _moe_blocks(xx, mod1, l1_norm_ffn_w.reshape(1, d), l1_router_w, l1_router_b, l1_w_up, l1_b_up, l1_w_down, l1_b_down,
                final_norm_w.reshape(1, d), bsz, 1, True)
```
